```python
import jax, jax.numpy as jnp
from jax import lax
import numpy as np

D_MODEL = 1024
BATCH = 2
SEQ = 8192
DEPTH = 2
DEC_BATCH = 128
DEC_SEQ = 8
PAST_LEN = 8192
PAGE_SIZE = 128

RWKV_HEAD = 64
RWKV_HEADS = D_MODEL // RWKV_HEAD
DECAY_LORA = 64
ICL_LORA = 64
GN_EPS = 64e-5
ATT_HEAD_DIM = 64
ATT_HEADS = D_MODEL // ATT_HEAD_DIM
ATT_KV_HEADS = 4
ATT_GROUP = ATT_HEADS // ATT_KV_HEADS
WINDOW = 128
BLOCK = WINDOW
ATT_SCALE = ATT_HEAD_DIM ** -0.5
NEG = -1e30
N_A = (DEPTH + 1) // 2
N_B = DEPTH // 2
ALPHA = (2 * DEPTH) ** 0.25
BETA = (8 * DEPTH) ** -0.25
LN_EPS = 1e-5

kernel_name = 'rwkv7_swa_sink_alibi_deepnorm_step'


def layer_norm(x, g, b):
    xf = x.astype(jnp.float32)
    mu = jnp.mean(xf, -1, keepdims=True)
    var = jnp.mean(jnp.square(xf - mu), -1, keepdims=True)
    return ((xf - mu) * lax.rsqrt(var + LN_EPS) * g.astype(jnp.float32) + b.astype(jnp.float32)).astype(x.dtype)


def rwkv7_mixer(x, shift_prev, wkv0, mu, w_in, w0, w1, w2, a0, a1, a2, k_k, k_a, r_k, gn_g, gn_b, w_out):
    B, T, D = x.shape
    H, N = RWKV_HEADS, RWKV_HEAD
    f32 = jnp.float32
    x_prev = jnp.concatenate([shift_prev[:, None, :].astype(x.dtype), x[:, :-1]], axis=1)
    xx = x_prev - x
    xs = x[None] + xx[None] * mu[:, None, None, :]
    rkvg = jnp.einsum('pbtd,pde->pbte', xs[:4], w_in)
    r, k, v, g = rkvg[0], rkvg[1], rkvg[2], rkvg[3]
    wl = (w0 + jnp.tanh(xs[4] @ w1) @ w2).astype(f32)
    decay = jnp.exp(-jnp.exp(-jax.nn.softplus(-wl) - 0.5))
    a = jax.nn.sigmoid((a0 + (xs[5] @ a1) @ a2).astype(f32))
    heads = lambda t: t.reshape(B, T, H, N)
    kk = heads(k.astype(f32) * k_k.astype(f32))
    kk = kk / jnp.maximum(jnp.sqrt(jnp.sum(kk * kk, -1, keepdims=True)), 1e-12)
    kf = k.astype(f32) * (1.0 + (a - 1.0) * k_a.astype(f32))
    rh, kh, vh, ah, wh = heads(r.astype(f32)), heads(kf), heads(v.astype(f32)), heads(a), heads(decay)
    seq = (jnp.moveaxis(rh, 1, 0), jnp.moveaxis(wh, 1, 0), jnp.moveaxis(kh, 1, 0),
           jnp.moveaxis(vh, 1, 0), jnp.moveaxis(kk, 1, 0), jnp.moveaxis(kk * ah, 1, 0))

    def step(S, inp):
        r_t, w_t, k_t, v_t, kk_t, b_t = inp
        sa = jnp.einsum('bhvk,bhk->bhv', S, kk_t)
        S = S * w_t[:, :, None, :] - sa[..., None] * b_t[:, :, None, :] + v_t[..., None] * k_t[:, :, None, :]
        return S, jnp.einsum('bhvk,bhk->bhv', S, r_t)

    S_fin, ys = lax.scan(step, wkv0.astype(f32), seq)
    y = jnp.moveaxis(ys, 0, 1)
    m = jnp.mean(y, -1, keepdims=True)
    var = jnp.mean(jnp.square(y - m), -1, keepdims=True)
    y = (y - m) * lax.rsqrt(var + GN_EPS) * gn_g.astype(f32).reshape(H, N) + gn_b.astype(f32).reshape(H, N)
    y = y + jnp.sum(rh * kh * r_k.astype(f32), -1, keepdims=True) * vh
    out = (y.reshape(B, T, D).astype(x.dtype) * jax.nn.silu(g)) @ w_out
    return out, x[:, -1], S_fin.astype(x.dtype)


def swa_project(x, w_in):
    B, T, _ = x.shape
    HQ, HKV = ATT_HEADS * ATT_HEAD_DIM, ATT_KV_HEADS * ATT_HEAD_DIM
    z = x @ w_in
    q = z[..., :HQ].reshape(B, T, ATT_KV_HEADS, ATT_GROUP, ATT_HEAD_DIM)
    k = z[..., HQ:HQ + HKV].reshape(B, T, ATT_KV_HEADS, ATT_HEAD_DIM)
    v = z[..., HQ + HKV:HQ + 2 * HKV].reshape(B, T, ATT_KV_HEADS, ATT_HEAD_DIM)
    gate = z[..., HQ + 2 * HKV:]
    return q, k, v, gate


def alibi_slopes():
    h = jnp.arange(1, ATT_HEADS + 1, dtype=jnp.float32)
    return jnp.exp2(-8.0 * h / ATT_HEADS).reshape(ATT_KV_HEADS, ATT_GROUP)


def window_attend(q, k, v, q_pos, k_pos, sinks):
    s = jnp.einsum('bnqhgd,bnshd->bnhgqs', q, k, preferred_element_type=jnp.float32) * ATT_SCALE
    dist = q_pos[:, :, None] - k_pos[:, None, :]
    live = (dist >= 0) & (dist < WINDOW) & (k_pos[:, None, :] >= 0)
    s = s - alibi_slopes()[None, None, :, :, None, None] * dist.astype(jnp.float32)[None, :, None, None]
    s = jnp.where(live[None, :, None, None], s, NEG)
    sink = jnp.broadcast_to(sinks.astype(jnp.float32).reshape(ATT_KV_HEADS, ATT_GROUP)[None, None, :, :, None, None],
                            s.shape[:-1] + (1,))
    p = jax.nn.softmax(jnp.concatenate([s, sink], -1), axis=-1)[..., :-1]
    return jnp.einsum('bnhgqs,bnshd->bnqhgd', p.astype(v.dtype), v)


def swa_prompt(x, w_in, sinks, w_out):
    B, T, _ = x.shape
    nb = T // BLOCK
    q, k, v, gate = swa_project(x, w_in)
    qb = q.reshape(B, nb, BLOCK, ATT_KV_HEADS, ATT_GROUP, ATT_HEAD_DIM)
    kb = k.reshape(B, nb, BLOCK, ATT_KV_HEADS, ATT_HEAD_DIM)
    vb = v.reshape(B, nb, BLOCK, ATT_KV_HEADS, ATT_HEAD_DIM)
    band = lambda t: jnp.concatenate([jnp.concatenate([jnp.zeros_like(t[:, :1]), t[:, :-1]], 1), t], 2)
    pos = jnp.arange(T, dtype=jnp.int32).reshape(nb, BLOCK)
    k_pos = jnp.concatenate([pos - BLOCK, pos], 1)
    o = window_attend(qb, band(kb), band(vb), pos, k_pos, sinks).reshape(B, T, -1)
    out = (o * jax.nn.silu(gate)) @ w_out
    keep = min(WINDOW, T)
    return out, k[:, T - keep:], v[:, T - keep:]


def swa_sample(x, buf_k, buf_v, w_in, sinks, w_out):
    B, T, _ = x.shape
    wb = buf_k.shape[1]
    q, k, v, gate = swa_project(x, w_in)
    keys = jnp.concatenate([buf_k.astype(k.dtype), k], 1)
    vals = jnp.concatenate([buf_v.astype(v.dtype), v], 1)
    q_pos = PAST_LEN + jnp.arange(T, dtype=jnp.int32)
    k_pos = jnp.concatenate([PAST_LEN - wb + jnp.arange(wb, dtype=jnp.int32), q_pos])
    o = window_attend(q[:, None], keys[:, None], vals[:, None], q_pos[None], k_pos[None], sinks)[:, 0]
    out = (o.reshape(B, T, -1) * jax.nn.silu(gate)) @ w_out
    return out, keys[:, T:], vals[:, T:]


def setup_inputs(seed: int = 0) -> dict:
    key = jax.random.key(seed)
    ks = jax.random.split(key, 26)
    f32 = jnp.float32
    D, H, N = D_MODEL, RWKV_HEADS, RWKV_HEAD
    HQ, HKV = ATT_HEADS * ATT_HEAD_DIM, ATT_KV_HEADS * ATT_HEAD_DIM
    wb = min(WINDOW, PAST_LEN)
    nrm = lambda k, shape, s: s * jax.random.normal(k, shape, f32)
    att_cols = jnp.concatenate([jnp.ones((HQ + HKV,), f32), jnp.full((HKV,), BETA, f32), jnp.ones((HQ,), f32)])
    return {
        'x_prompt': nrm(ks[0], (BATCH, SEQ, D), 1.0),
        'x_sample': nrm(ks[1], (DEC_BATCH, DEC_SEQ, D), 1.0),
        'state_wkv': nrm(ks[2], (N_A, DEC_BATCH, H, N, N), 0.3),
        'state_shift': nrm(ks[3], (N_A, DEC_BATCH, D), 1.0),
        'cache_win_k': nrm(ks[4], (N_B, DEC_BATCH, wb, ATT_KV_HEADS, ATT_HEAD_DIM), 1.0),
        'cache_win_v': nrm(ks[5], (N_B, DEC_BATCH, wb, ATT_KV_HEADS, ATT_HEAD_DIM), 1.0),
        'ln_g': 1.0 + nrm(ks[6], (DEPTH, D), 0.05),
        'ln_b': nrm(ks[7], (DEPTH, D), 0.02),
        'rwkv_mu': jax.random.uniform(ks[8], (N_A, 6, D), f32),
        'rwkv_w_in': nrm(ks[9], (N_A, 4, D, D), D ** -0.5) * jnp.array([1.0, 1.0, BETA, 1.0], f32)[None, :, None, None],
        'rwkv_w0': jax.random.uniform(ks[10], (N_A, D), f32, -6.0, 1.0),
        'rwkv_w1': nrm(ks[11], (N_A, D, DECAY_LORA), D ** -0.5),
        'rwkv_w2': nrm(ks[12], (N_A, DECAY_LORA, D), 0.5 * DECAY_LORA ** -0.5),
        'rwkv_a0': nrm(ks[13], (N_A, D), 0.3),
        'rwkv_a1': nrm(ks[14], (N_A, D, ICL_LORA), D ** -0.5),
        'rwkv_a2': nrm(ks[15], (N_A, ICL_LORA, D), 0.5 * ICL_LORA ** -0.5),
        'rwkv_k_k': 0.85 + nrm(ks[16], (N_A, D), 0.05),
        'rwkv_k_a': 1.0 + nrm(ks[17], (N_A, D), 0.05),
        'rwkv_r_k': nrm(ks[18], (N_A, H, N), 0.1),
        'rwkv_gn_g': 1.0 + nrm(ks[19], (N_A, D), 0.05),
        'rwkv_gn_b': nrm(ks[20], (N_A, D), 0.02),
        'rwkv_w_out': nrm(ks[21], (N_A, D, D), BETA * D ** -0.5),
        'att_w_in': nrm(ks[22], (N_B, D, 2 * HQ + 2 * HKV), D ** -0.5) * att_cols,
        'att_sinks': nrm(ks[23], (N_B, ATT_HEADS), 1.0),
        'att_w_out': nrm(ks[24], (N_B, HQ, D), BETA * HQ ** -0.5),
    }


def reference(x_prompt, x_sample, state_wkv, state_shift, cache_win_k, cache_win_v, ln_g, ln_b,
              rwkv_mu, rwkv_w_in, rwkv_w0, rwkv_w1, rwkv_w2, rwkv_a0, rwkv_a1, rwkv_a2, rwkv_k_k,
              rwkv_k_a, rwkv_r_k, rwkv_gn_g, rwkv_gn_b, rwkv_w_out, att_w_in, att_sinks, att_w_out):
    xp, xs = x_prompt, x_sample
    bp = xp.shape[0]
    wkv_p, sh_p, wk_p, wv_p = [], [], [], []
    wkv_s, sh_s, wk_s, wv_s = [], [], [], []
    for i in range(DEPTH):
        j = i // 2
        if i % 2 == 0:
            prm = (rwkv_mu[j], rwkv_w_in[j], rwkv_w0[j], rwkv_w1[j], rwkv_w2[j], rwkv_a0[j], rwkv_a1[j],
                   rwkv_a2[j], rwkv_k_k[j], rwkv_k_a[j], rwkv_r_k[j], rwkv_gn_g[j], rwkv_gn_b[j], rwkv_w_out[j])
            zero_shift = jnp.zeros((bp, D_MODEL), xp.dtype)
            zero_wkv = jnp.zeros((bp, RWKV_HEADS, RWKV_HEAD, RWKV_HEAD), xp.dtype)
            dp, shp, Sp = rwkv7_mixer(xp, zero_shift, zero_wkv, *prm)
            ds, shs, Ss = rwkv7_mixer(xs, state_shift[j], state_wkv[j], *prm)
            wkv_p.append(Sp); sh_p.append(shp); wkv_s.append(Ss); sh_s.append(shs)
        else:
            dp, kp, vp = swa_prompt(xp, att_w_in[j], att_sinks[j], att_w_out[j])
            ds, ksn, vsn = swa_sample(xs, cache_win_k[j], cache_win_v[j], att_w_in[j], att_sinks[j], att_w_out[j])
            wk_p.append(kp); wv_p.append(vp); wk_s.append(ksn); wv_s.append(vsn)
        xp = layer_norm(ALPHA * xp + dp, ln_g[i], ln_b[i])
        xs = layer_norm(ALPHA * xs + ds, ln_g[i], ln_b[i])
    new_wkv_prompt = jnp.stack(wkv_p)
    new_shift_prompt = jnp.stack(sh_p)
    new_win_k_prompt = jnp.stack(wk_p)
    new_win_v_prompt = jnp.stack(wv_p)
    new_wkv_sample = jnp.stack(wkv_s)
    new_shift_sample = jnp.stack(sh_s)
    new_win_k_sample = jnp.stack(wk_s)
    new_win_v_sample = jnp.stack(wv_s)
    return (xp, xs, new_wkv_prompt, new_shift_prompt, new_win_k_prompt, new_win_v_prompt,
            new_wkv_sample, new_shift_sample, new_win_k_sample, new_win_v_sample)
```

```python
import functools

import jax
import jax.numpy as jnp
from jax import lax
from jax.experimental import pallas as pl
from jax.experimental.pallas import tpu as pltpu

F32 = jnp.float32
BF16 = jnp.bfloat16

RWKV_HEAD = 64
ATT_HEAD_DIM = 64
ATT_KV_HEADS = 4
ATT_GROUP = 4
WINDOW = 128
ATT_SCALE = ATT_HEAD_DIM ** -0.5
NEG = -1e30
DEPTH = 2
ALPHA = (2 * DEPTH) ** 0.25
LN_EPS = 1e-5
GN_EPS = 64e-5

V7X_LANES = 128
V7X_VMEM_LIMIT_BYTES = 56 * 1024 * 1024

HEADS_PER_VREG = V7X_LANES // RWKV_HEAD
STACK_ROWS = 128
WKV_ROWS = STACK_ROWS // HEADS_PER_VREG
PROMPT_CHUNK = WKV_ROWS

WKV_PRECISION = lax.Precision.HIGHEST


def _cparams(*sem):
    return pltpu.CompilerParams(dimension_semantics=sem, vmem_limit_bytes=V7X_VMEM_LIMIT_BYTES)


def _const_spec(shape):
    nd = len(shape)
    return pl.BlockSpec(shape, lambda *_: (0,) * nd)


def _rwkv_pre_kernel(x_ref, xp_ref, mu_ref, win_ref, w0_ref, w1_ref, w2_ref, a0_ref, a1_ref, a2_ref,
                     r_ref, k_ref, v_ref, g_ref, lw_ref, a_ref):
    x = x_ref[...]
    xx = xp_ref[...] - x

    def mix(p):
        return (x + xx * mu_ref[p:p + 1, :]).astype(BF16)

    def proj(p):
        return jnp.dot(mix(p), win_ref[p], preferred_element_type=F32)

    r_ref[...] = proj(0)
    k_ref[...] = proj(1)
    v_ref[...] = proj(2)
    g_ref[...] = proj(3)
    hw = jnp.tanh(jnp.dot(mix(4), w1_ref[...], preferred_element_type=F32))
    wl = w0_ref[...] + jnp.dot(hw.astype(BF16), w2_ref[...], preferred_element_type=F32)
    z = -wl
    softplus = jnp.maximum(z, 0.0) + jnp.log(1.0 + jnp.exp(-jnp.abs(z)))
    lw_ref[...] = -jnp.exp(-softplus - 0.5)
    ha = jnp.dot(mix(5), a1_ref[...], preferred_element_type=F32)
    al = a0_ref[...] + jnp.dot(ha.astype(BF16), a2_ref[...], preferred_element_type=F32)
    a_ref[...] = 1.0 / (1.0 + jnp.exp(-al))


def _rwkv_pre(x2, xp2, mu, w_in, w0, w1, w2, a0, a1, a2, tm):
    rows, d = x2.shape
    lora = w1.shape[1]
    row_spec = pl.BlockSpec((tm, d), lambda i: (i, 0))
    out = jax.ShapeDtypeStruct((rows, d), F32)
    return pl.pallas_call(
        _rwkv_pre_kernel,
        grid=(rows // tm,),
        in_specs=[row_spec, row_spec, _const_spec((6, d)), _const_spec((4, d, d)), _const_spec((1, d)),
                  _const_spec((d, lora)), _const_spec((lora, d)), _const_spec((1, d)),
                  _const_spec((d, lora)), _const_spec((lora, d))],
        out_specs=[row_spec] * 6,
        out_shape=[out] * 6,
        compiler_params=_cparams("parallel"),
        name="rwkv_pre",
    )(x2, xp2, mu, w_in.astype(BF16), w0.reshape(1, d), w1.astype(BF16), w2.astype(BF16),
      a0.reshape(1, d), a1.astype(BF16), a2.astype(BF16))


def _mm(a, b, dims):
    return lax.dot_general(a, b, (dims, ((), ())), precision=WKV_PRECISION, preferred_element_type=F32)


def _mm_nn(a, b):
    return _mm(a, b, ((1,), (0,)))


def _mm_nt(a, b):
    return _mm(a, b, ((1,), (1,)))


def _mm_tn(a, b):
    return _mm(a, b, ((0,), (0,)))


def _wkv_pair(r, k, v, lw, a, k_k, k_a, r_k, gn_g, gn_b, states, nb, c):
    rows = nb * c
    lane = lax.broadcasted_iota(jnp.int32, (1, V7X_LANES), 1)
    head0 = lane < RWKV_HEAD
    mk = (jnp.where(head0, 1.0, 0.0).astype(F32), jnp.where(head0, 0.0, 1.0).astype(F32))

    def head_sum(t):
        s0 = jnp.sum(t * mk[0], axis=-1, keepdims=True)
        s1 = jnp.sum(t * mk[1], axis=-1, keepdims=True)
        return jnp.where(head0, s0, s1)

    kkr = k * k_k
    kk = kkr / jnp.maximum(jnp.sqrt(head_sum(kkr * kkr)), 1e-12)
    kf = k * (1.0 + (a - 1.0) * k_a)
    bb = kk * a

    ri = lax.broadcasted_iota(jnp.int32, (rows, rows), 0)
    ci = lax.broadcasted_iota(jnp.int32, (rows, rows), 1)
    tri = jnp.where((ri // c == ci // c) & (ri >= ci), 1.0, 0.0).astype(F32)
    cum = _mm_nn(tri, lw)
    g_incl = jnp.exp(cum)
    g_excl = jnp.exp(cum - lw)
    g_inv = jnp.exp(-cum)

    at = -kk * g_excl
    rt = r * g_incl
    bt = bb * g_inv
    kt = kf * g_inv

    def stack(t):
        parts = []
        for b in range(nb):
            tb = t[b * c:(b + 1) * c]
            parts += [tb * mk[0], tb * mk[1]]
        return jnp.concatenate(parts, axis=0)

    at_s, rt_s, bt_s, kt_s, v_s = stack(at), stack(rt), stack(bt), stack(kt), stack(v)
    xa = jnp.concatenate([at_s, rt_s], axis=0)
    n = STACK_ROWS
    si = lax.broadcasted_iota(jnp.int32, (n, n), 0)
    sj = lax.broadcasted_iota(jnp.int32, (n, n), 1)
    same = si // c == sj // c
    strict = same & (si > sj)
    incl = same & (si >= sj)

    gb = _mm_nt(xa, bt_s)
    gk = _mm_nt(xa, kt_s)
    a_ab = jnp.where(strict, gb[:n], 0.0)
    a_rb = jnp.where(incl, gb[n:], 0.0)
    a_ak = jnp.where(strict, gk[:n], 0.0)
    a_rk = jnp.where(incl, gk[n:], 0.0)

    eye = jnp.where(si == sj, 1.0, 0.0).astype(F32)
    tinv = eye + a_ab
    pw = a_ab
    span = 2
    while span < c:
        pw = _mm_nn(pw, pw)
        tinv = tinv + _mm_nn(tinv, pw)
        span *= 2

    hb = 2 * c
    ah_a, ah_r = [], []
    for b in range(nb):
        ah_a.append(_mm_nt(at_s[b * hb:(b + 1) * hb], states[b]))
        ah_r.append(_mm_nt(rt_s[b * hb:(b + 1) * hb], states[b]))
    ah_a = jnp.concatenate(ah_a, axis=0) if nb > 1 else ah_a[0]
    ah_r = jnp.concatenate(ah_r, axis=0) if nb > 1 else ah_r[0]

    u = _mm_nn(tinv, ah_a + _mm_nn(a_ak, v_s))
    y_s = ah_r + _mm_nn(a_rb, u) + _mm_nn(a_rk, v_s)

    new_states = []
    ys = []
    for b in range(nb):
        sl = slice(b * hb, (b + 1) * hb)
        upd = _mm_tn(u[sl], bt_s[sl]) + _mm_tn(v_s[sl], kt_s[sl])
        g_last = g_incl[(b + 1) * c - 1:(b + 1) * c]
        new_states.append((states[b] + upd) * g_last)
        ys.append(y_s[b * hb:b * hb + c] + y_s[b * hb + c:(b + 1) * hb])
    y = jnp.concatenate(ys, axis=0) if nb > 1 else ys[0]

    inv_n = 1.0 / RWKV_HEAD
    mean = head_sum(y) * inv_n
    yc = y - mean
    var = head_sum(yc * yc) * inv_n
    yn = yc * lax.rsqrt(var + GN_EPS) * gn_g + gn_b
    yn = yn + head_sum(r * kf * r_k) * v
    return yn, new_states


def _pair_lanes(p):
    return slice(p * V7X_LANES, (p + 1) * V7X_LANES)


def _state_to_blockdiag(s0, s1):
    z = jnp.zeros_like(s0)
    return jnp.concatenate([jnp.concatenate([s0, z], axis=1), jnp.concatenate([z, s1], axis=1)], axis=0)


def _wkv_prompt_kernel(r_ref, k_ref, v_ref, lw_ref, a_ref, kk_ref, ka_ref, rk_ref, gg_ref, gb_ref,
                       y_ref, sfin_ref, s_scr, *, n_pairs, c):
    ci = pl.program_id(1)

    @pl.when(ci == 0)
    def _():
        s_scr[...] = jnp.zeros_like(s_scr)

    for p in range(n_pairs):
        ln = _pair_lanes(p)
        yn, new = _wkv_pair(r_ref[:, ln], k_ref[:, ln], v_ref[:, ln], lw_ref[:, ln], a_ref[:, ln],
                            kk_ref[:, ln], ka_ref[:, ln], rk_ref[:, ln], gg_ref[:, ln], gb_ref[:, ln],
                            [s_scr[p]], 1, c)
        y_ref[:, ln] = yn
        s_scr[p] = new[0]

    @pl.when(ci == pl.num_programs(1) - 1)
    def _():
        for p in range(n_pairs):
            s = s_scr[p]
            sfin_ref[0, 2 * p] = s[:RWKV_HEAD, :RWKV_HEAD]
            sfin_ref[0, 2 * p + 1] = s[RWKV_HEAD:, RWKV_HEAD:]


def _wkv_sample_kernel(r_ref, k_ref, v_ref, lw_ref, a_ref, kk_ref, ka_ref, rk_ref, gg_ref, gb_ref,
                       s0_ref, y_ref, sfin_ref, *, n_pairs, nb, c):
    for p in range(n_pairs):
        ln = _pair_lanes(p)
        states = [_state_to_blockdiag(s0_ref[b, 2 * p], s0_ref[b, 2 * p + 1]) for b in range(nb)]
        yn, new = _wkv_pair(r_ref[:, ln], k_ref[:, ln], v_ref[:, ln], lw_ref[:, ln], a_ref[:, ln],
                            kk_ref[:, ln], ka_ref[:, ln], rk_ref[:, ln], gg_ref[:, ln], gb_ref[:, ln],
                            states, nb, c)
        y_ref[:, ln] = yn
        for b in range(nb):
            sfin_ref[b, 2 * p] = new[b][:RWKV_HEAD, :RWKV_HEAD]
            sfin_ref[b, 2 * p + 1] = new[b][RWKV_HEAD:, RWKV_HEAD:]


def _wkv_prompt(r, k, v, lw, a, k_k, k_a, r_k, gn_g, gn_b, batch, seq):
    rows, d = r.shape
    heads = d // RWKV_HEAD
    c = PROMPT_CHUNK
    nc = seq // c
    row_spec = pl.BlockSpec((c, d), lambda b, i: (b * nc + i, 0))
    vec = _const_spec((1, d))
    kern = functools.partial(_wkv_prompt_kernel, n_pairs=heads // HEADS_PER_VREG, c=c)
    return pl.pallas_call(
        kern,
        grid=(batch, nc),
        in_specs=[row_spec] * 5 + [vec] * 5,
        out_specs=[row_spec, pl.BlockSpec((1, heads, RWKV_HEAD, RWKV_HEAD), lambda b, i: (b, 0, 0, 0))],
        out_shape=[jax.ShapeDtypeStruct((rows, d), F32),
                   jax.ShapeDtypeStruct((batch, heads, RWKV_HEAD, RWKV_HEAD), F32)],
        scratch_shapes=[pltpu.VMEM((heads // HEADS_PER_VREG, V7X_LANES, V7X_LANES), F32)],
        compiler_params=_cparams("parallel", "arbitrary"),
        name="wkv_prompt",
    )(r, k, v, lw, a, k_k, k_a, r_k, gn_g, gn_b)


def _wkv_sample(r, k, v, lw, a, k_k, k_a, r_k, gn_g, gn_b, state, batch, seq):
    rows, d = r.shape
    heads = d // RWKV_HEAD
    nb = WKV_ROWS // seq
    row_spec = pl.BlockSpec((WKV_ROWS, d), lambda i: (i, 0))
    st_spec = pl.BlockSpec((nb, heads, RWKV_HEAD, RWKV_HEAD), lambda i: (i, 0, 0, 0))
    vec = _const_spec((1, d))
    kern = functools.partial(_wkv_sample_kernel, n_pairs=heads // HEADS_PER_VREG, nb=nb, c=seq)
    return pl.pallas_call(
        kern,
        grid=(batch // nb,),
        in_specs=[row_spec] * 5 + [vec] * 5 + [st_spec],
        out_specs=[row_spec, st_spec],
        out_shape=[jax.ShapeDtypeStruct((rows, d), F32),
                   jax.ShapeDtypeStruct((batch, heads, RWKV_HEAD, RWKV_HEAD), F32)],
        compiler_params=_cparams("parallel"),
        name="wkv_sample",
    )(r, k, v, lw, a, k_k, k_a, r_k, gn_g, gn_b, state)


def _post_kernel(y_ref, g_ref, x_ref, w_ref, lng_ref, lnb_ref, o_ref):
    g = g_ref[...]
    h = (y_ref[...] * (g / (1.0 + jnp.exp(-g)))).astype(BF16)
    z = ALPHA * x_ref[...] + jnp.dot(h, w_ref[...], preferred_element_type=F32)
    mu = jnp.mean(z, axis=-1, keepdims=True)
    zc = z - mu
    var = jnp.mean(zc * zc, axis=-1, keepdims=True)
    o_ref[...] = zc * lax.rsqrt(var + LN_EPS) * lng_ref[...] + lnb_ref[...]


def _post(y, g, x2, w_out, ln_g, ln_b, tm):
    rows, d = x2.shape
    din = y.shape[1]
    in_spec = pl.BlockSpec((tm, din), lambda i: (i, 0))
    row_spec = pl.BlockSpec((tm, d), lambda i: (i, 0))
    return pl.pallas_call(
        _post_kernel,
        grid=(rows // tm,),
        in_specs=[in_spec, in_spec, row_spec, _const_spec((din, d)), _const_spec((1, d)), _const_spec((1, d))],
        out_specs=row_spec,
        out_shape=jax.ShapeDtypeStruct((rows, d), F32),
        compiler_params=_cparams("parallel"),
        name="branch_post",
    )(y, g, x2, w_out.astype(BF16), ln_g.reshape(1, d), ln_b.reshape(1, d))


def _att_pre_kernel(x_ref, w_ref, q_ref, k_ref, v_ref, g_ref, *, hq, hkv):
    z = jnp.dot(x_ref[...].astype(BF16), w_ref[...], preferred_element_type=F32)
    q_ref[...] = z[:, :hq]
    k_ref[...] = z[:, hq:hq + hkv]
    v_ref[...] = z[:, hq + hkv:hq + 2 * hkv]
    g_ref[...] = z[:, hq + 2 * hkv:]


def _att_pre(x2, w_in, tm):
    rows, d = x2.shape
    hq = ATT_KV_HEADS * ATT_GROUP * ATT_HEAD_DIM
    hkv = ATT_KV_HEADS * ATT_HEAD_DIM
    wide = pl.BlockSpec((tm, hq), lambda i: (i, 0))
    narrow = pl.BlockSpec((tm, hkv), lambda i: (i, 0))
    return pl.pallas_call(
        functools.partial(_att_pre_kernel, hq=hq, hkv=hkv),
        grid=(rows // tm,),
        in_specs=[pl.BlockSpec((tm, d), lambda i: (i, 0)), _const_spec(w_in.shape)],
        out_specs=[wide, narrow, narrow, wide],
        out_shape=[jax.ShapeDtypeStruct((rows, hq), F32), jax.ShapeDtypeStruct((rows, hkv), F32),
                   jax.ShapeDtypeStruct((rows, hkv), F32), jax.ShapeDtypeStruct((rows, hq), F32)],
        compiler_params=_cparams("parallel"),
        name="att_pre",
    )(x2, w_in.astype(BF16))


def _alibi_slope(h):
    n_heads = ATT_KV_HEADS * ATT_GROUP
    return 2.0 ** (-8.0 * (h + 1) / n_heads)


def _attend(q, kcat, vcat, slope, sink, dist, live):
    s = lax.dot_general(q, kcat, (((1,), (1,)), ((), ())), preferred_element_type=F32) * ATT_SCALE
    s = jnp.where(live, s - slope * dist, NEG)
    m = jnp.maximum(jnp.max(s, axis=-1, keepdims=True), sink)
    p = jnp.exp(s - m)
    den = jnp.sum(p, axis=-1, keepdims=True) + jnp.exp(sink - m)
    return jnp.dot(p.astype(BF16), vcat, preferred_element_type=F32) / den


def _att_prompt_kernel(sink_ref, q_ref, kc_ref, vc_ref, kp_ref, vp_ref, o_ref):
    blk = pl.program_id(1)
    qi = lax.broadcasted_iota(jnp.int32, (WINDOW, 2 * WINDOW), 0)
    kj = lax.broadcasted_iota(jnp.int32, (WINDOW, 2 * WINDOW), 1)
    dist_i = qi + WINDOW - kj
    live = (dist_i >= 0) & (dist_i < WINDOW) & ((kj >= WINDOW) | (blk > 0))
    dist = dist_i.astype(F32)
    hd = ATT_HEAD_DIM
    for kvh in range(ATT_KV_HEADS):
        ks = slice(kvh * hd, (kvh + 1) * hd)
        kcat = jnp.concatenate([kp_ref[:, ks], kc_ref[:, ks]], axis=0).astype(BF16)
        vcat = jnp.concatenate([vp_ref[:, ks], vc_ref[:, ks]], axis=0).astype(BF16)
        for g in range(ATT_GROUP):
            h = kvh * ATT_GROUP + g
            hs = slice(h * hd, (h + 1) * hd)
            o_ref[:, hs] = _attend(q_ref[:, hs].astype(BF16), kcat, vcat, _alibi_slope(h), sink_ref[h],
                                   dist, live)


def _att_prompt(q, k, v, sinks, batch, seq):
    rows, hq = q.shape
    hkv = k.shape[1]
    nb = seq // WINDOW
    cur = lambda b, n: (b * nb + n, 0)
    prev = lambda b, n: (b * nb + jnp.maximum(n - 1, 0), 0)
    return pl.pallas_call(
        _att_prompt_kernel,
        grid=(batch, nb),
        in_specs=[pl.BlockSpec(memory_space=pltpu.SMEM),
                  pl.BlockSpec((WINDOW, hq), cur),
                  pl.BlockSpec((WINDOW, hkv), cur), pl.BlockSpec((WINDOW, hkv), cur),
                  pl.BlockSpec((WINDOW, hkv), prev), pl.BlockSpec((WINDOW, hkv), prev)],
        out_specs=pl.BlockSpec((WINDOW, hq), cur),
        out_shape=jax.ShapeDtypeStruct((rows, hq), F32),
        compiler_params=_cparams("parallel", "parallel"),
        name="att_prompt",
    )(sinks, q, k, v, k, v)


def _att_sample_kernel(sink_ref, q_ref, kn_ref, vn_ref, ck_ref, cv_ref, o_ref, *, nb, t):
    m = ATT_GROUP * t
    ri = lax.broadcasted_iota(jnp.int32, (m, 1), 0)
    grp = ri // t
    qi = lax.broadcasted_iota(jnp.int32, (m, 2 * WINDOW), 0) % t
    kj = lax.broadcasted_iota(jnp.int32, (m, 2 * WINDOW), 1)
    dist_i = qi + WINDOW - kj
    live = (dist_i >= 0) & (dist_i < WINDOW)
    dist = dist_i.astype(F32)
    hd = ATT_HEAD_DIM
    pad = jnp.zeros((WINDOW - t, hd), F32)
    for kvh in range(ATT_KV_HEADS):
        ks = slice(kvh * hd, (kvh + 1) * hd)
        slope = jnp.zeros((m, 1), F32)
        sink = jnp.zeros((m, 1), F32)
        for g in range(ATT_GROUP):
            h = kvh * ATT_GROUP + g
            slope = jnp.where(grp == g, _alibi_slope(h), slope)
            sink = jnp.where(grp == g, sink_ref[h], sink)
        for b in range(nb):
            rs = slice(b * t, (b + 1) * t)
            kcat = jnp.concatenate([ck_ref[b, :, ks], kn_ref[rs, ks], pad], axis=0).astype(BF16)
            vcat = jnp.concatenate([cv_ref[b, :, ks], vn_ref[rs, ks], pad], axis=0).astype(BF16)
            qs = jnp.concatenate(
                [q_ref[rs, (kvh * ATT_GROUP + g) * hd:(kvh * ATT_GROUP + g + 1) * hd] for g in range(ATT_GROUP)],
                axis=0).astype(BF16)
            o = _attend(qs, kcat, vcat, slope, sink, dist, live)
            for g in range(ATT_GROUP):
                h = kvh * ATT_GROUP + g
                o_ref[rs, h * hd:(h + 1) * hd] = o[g * t:(g + 1) * t]


def _att_sample(q, k, v, cache_k, cache_v, sinks, batch, t, nb):
    rows, hq = q.shape
    hkv = k.shape[1]
    row = lambda i: (i, 0)
    cache_spec = pl.BlockSpec((nb, WINDOW, hkv), lambda i: (i, 0, 0))
    return pl.pallas_call(
        functools.partial(_att_sample_kernel, nb=nb, t=t),
        grid=(batch // nb,),
        in_specs=[pl.BlockSpec(memory_space=pltpu.SMEM),
                  pl.BlockSpec((nb * t, hq), row), pl.BlockSpec((nb * t, hkv), row),
                  pl.BlockSpec((nb * t, hkv), row), cache_spec, cache_spec],
        out_specs=pl.BlockSpec((nb * t, hq), row),
        out_shape=jax.ShapeDtypeStruct((rows, hq), F32),
        compiler_params=_cparams("parallel"),
        name="att_sample",
    )(sinks, q, k, v, cache_k, cache_v)


def _row_tile(rows):
    return 256 if rows % 256 == 0 else rows


def _shifted(x, first):
    return jnp.concatenate([first[:, None, :], x[:, :-1]], axis=1)


def kernel(x_prompt, x_sample, state_wkv, state_shift, cache_win_k, cache_win_v, ln_g, ln_b, rwkv_mu, rwkv_w_in, rwkv_w0, rwkv_w1, rwkv_w2, rwkv_a0, rwkv_a1, rwkv_a2, rwkv_k_k, rwkv_k_a, rwkv_r_k, rwkv_gn_g, rwkv_gn_b, rwkv_w_out, att_w_in, att_sinks, att_w_out):
    bp, tp, d = x_prompt.shape
    bs, ts, _ = x_sample.shape
    assert tp % PROMPT_CHUNK == 0 and tp % WINDOW == 0 and WKV_ROWS % ts == 0 and bs % (WKV_ROWS // ts) == 0
    assert cache_win_k.shape[2] == WINDOW
    heads = d // RWKV_HEAD
    hkv = ATT_KV_HEADS * ATT_HEAD_DIM

    j = 0
    vec = lambda t: t.reshape(1, d)
    pre_w = (rwkv_mu[j], rwkv_w_in[j], rwkv_w0[j], rwkv_w1[j], rwkv_w2[j], rwkv_a0[j], rwkv_a1[j], rwkv_a2[j])
    head_w = (vec(rwkv_k_k[j]), vec(rwkv_k_a[j]), vec(rwkv_r_k[j]), vec(rwkv_gn_g[j]), vec(rwkv_gn_b[j]))

    xp2 = x_prompt.reshape(bp * tp, d)
    xs2 = x_sample.reshape(bs * ts, d)
    xp_prev = _shifted(x_prompt, jnp.zeros((bp, d), F32)).reshape(bp * tp, d)
    xs_prev = _shifted(x_sample, state_shift[j]).reshape(bs * ts, d)

    r, k, v, g, lw, a = _rwkv_pre(xp2, xp_prev, *pre_w, tm=_row_tile(bp * tp))
    yp, wkv_p = _wkv_prompt(r, k, v, lw, a, *head_w, batch=bp, seq=tp)
    xp1 = _post(yp, g, xp2, rwkv_w_out[j], ln_g[0], ln_b[0], tm=_row_tile(bp * tp))

    r, k, v, g, lw, a = _rwkv_pre(xs2, xs_prev, *pre_w, tm=_row_tile(bs * ts))
    ys, wkv_s = _wkv_sample(r, k, v, lw, a, *head_w, state_wkv[j], batch=bs, seq=ts)
    xs1 = _post(ys, g, xs2, rwkv_w_out[j], ln_g[0], ln_b[0], tm=_row_tile(bs * ts))

    q, k, v, gate = _att_pre(xp1, att_w_in[j], tm=_row_tile(bp * tp))
    o = _att_prompt(q, k, v, att_sinks[j], batch=bp, seq=tp)
    y_prompt = _post(o, gate, xp1, att_w_out[j], ln_g[1], ln_b[1], tm=_row_tile(bp * tp)).reshape(bp, tp, d)
    win_k_p = k.reshape(bp, tp, ATT_KV_HEADS, ATT_HEAD_DIM)[:, tp - WINDOW:]
    win_v_p = v.reshape(bp, tp, ATT_KV_HEADS, ATT_HEAD_DIM)[:, tp - WINDOW:]

    q, k, v, gate = _att_pre(xs1, att_w_in[j], tm=_row_tile(bs * ts))
    ck = cache_win_k[j].reshape(bs, WINDOW, hkv)
    cv = cache_win_v[j].reshape(bs, WINDOW, hkv)
    o = _att_sample(q, k, v, ck, cv, att_sinks[j], batch=bs, t=ts, nb=WKV_ROWS // ts)
    y_sample = _post(o, gate, xs1, att_w_out[j], ln_g[1], ln_b[1], tm=_row_tile(bs * ts)).reshape(bs, ts, d)
    win_k_s = jnp.concatenate([ck[:, ts:], k.reshape(bs, ts, hkv)], axis=1).reshape(bs, WINDOW, ATT_KV_HEADS, ATT_HEAD_DIM)
    win_v_s = jnp.concatenate([cv[:, ts:], v.reshape(bs, ts, hkv)], axis=1).reshape(bs, WINDOW, ATT_KV_HEADS, ATT_HEAD_DIM)

    return (y_prompt, y_sample,
            wkv_p[None], x_prompt[:, -1][None], win_k_p[None], win_v_p[None],
            wkv_s[None], x_sample[:, -1][None], win_k_s[None], win_v_s[None])
```

```python
import functools

import jax
import jax.numpy as jnp
from jax import lax
from jax.experimental import pallas as pl
from jax.experimental.pallas import tpu as pltpu

F32 = jnp.float32
BF16 = jnp.bfloat16

RWKV_HEAD = 64
ATT_HEAD_DIM = 64
ATT_KV_HEADS = 4
ATT_GROUP = 4
WINDOW = 128
ATT_SCALE = ATT_HEAD_DIM ** -0.5
NEG = -1e30
DEPTH = 2
ALPHA = (2 * DEPTH) ** 0.25
LN_EPS = 1e-5
GN_EPS = 64e-5

V7X_LANES = 128
V7X_VMEM_LIMIT_BYTES = 56 * 1024 * 1024

HEADS_PER_VREG = V7X_LANES // RWKV_HEAD
STACK_ROWS = 128
WKV_ROWS = STACK_ROWS // HEADS_PER_VREG
PROMPT_CHUNK = WKV_ROWS


def _cparams(*sem):
    return pltpu.CompilerParams(dimension_semantics=sem, vmem_limit_bytes=V7X_VMEM_LIMIT_BYTES)


def _const_spec(shape):
    nd = len(shape)
    return pl.BlockSpec(shape, lambda *_: (0,) * nd)


def _rwkv_pre_kernel(x_ref, xp_ref, mu_ref, win_ref, w0_ref, w1_ref, w2_ref, a0_ref, a1_ref, a2_ref,
                     r_ref, k_ref, v_ref, g_ref, lw_ref, a_ref):
    x = x_ref[...]
    xx = xp_ref[...] - x

    def mix(p):
        return (x + xx * mu_ref[p:p + 1, :]).astype(BF16)

    def proj(p):
        return jnp.dot(mix(p), win_ref[p], preferred_element_type=F32)

    r_ref[...] = proj(0)
    k_ref[...] = proj(1)
    v_ref[...] = proj(2)
    g_ref[...] = proj(3)
    hw = jnp.tanh(jnp.dot(mix(4), w1_ref[...], preferred_element_type=F32))
    wl = w0_ref[...] + jnp.dot(hw.astype(BF16), w2_ref[...], preferred_element_type=F32)
    z = -wl
    softplus = jnp.maximum(z, 0.0) + jnp.log(1.0 + jnp.exp(-jnp.abs(z)))
    lw_ref[...] = -jnp.exp(-softplus - 0.5)
    ha = jnp.dot(mix(5), a1_ref[...], preferred_element_type=F32)
    al = a0_ref[...] + jnp.dot(ha.astype(BF16), a2_ref[...], preferred_element_type=F32)
    a_ref[...] = 1.0 / (1.0 + jnp.exp(-al))


def _rwkv_pre(x2, xp2, mu, w_in, w0, w1, w2, a0, a1, a2, tm):
    rows, d = x2.shape
    lora = w1.shape[1]
    row_spec = pl.BlockSpec((tm, d), lambda i: (i, 0))
    out = jax.ShapeDtypeStruct((rows, d), F32)
    return pl.pallas_call(
        _rwkv_pre_kernel,
        grid=(rows // tm,),
        in_specs=[row_spec, row_spec, _const_spec((6, d)), _const_spec((4, d, d)), _const_spec((1, d)),
                  _const_spec((d, lora)), _const_spec((lora, d)), _const_spec((1, d)),
                  _const_spec((d, lora)), _const_spec((lora, d))],
        out_specs=[row_spec] * 6,
        out_shape=[out] * 6,
        compiler_params=_cparams("parallel"),
        name="rwkv_pre",
    )(x2, xp2, mu, w_in.astype(BF16), w0.reshape(1, d), w1.astype(BF16), w2.astype(BF16),
      a0.reshape(1, d), a1.astype(BF16), a2.astype(BF16))


def _mm(a, b, dims):
    return lax.dot_general(a.astype(BF16), b.astype(BF16), (dims, ((), ())), preferred_element_type=F32)


def _mm_nn(a, b):
    return _mm(a, b, ((1,), (0,)))


def _mm_nt(a, b):
    return _mm(a, b, ((1,), (1,)))


def _mm_tn(a, b):
    return _mm(a, b, ((0,), (0,)))


def _wkv_chunk(r, k, v, lw, a, k_k, k_a, r_k, gn_g, gn_b, states, nb, c):
    pairs = range(len(r))
    rows = nb * c
    n = STACK_ROWS
    hb = 2 * c
    lane = lax.broadcasted_iota(jnp.int32, (1, V7X_LANES), 1)
    head0 = lane < RWKV_HEAD
    mk = (jnp.where(head0, 1.0, 0.0).astype(F32), jnp.where(head0, 0.0, 1.0).astype(F32))

    def head_sum(t):
        s0 = jnp.sum(t * mk[0], axis=-1, keepdims=True)
        s1 = jnp.sum(t * mk[1], axis=-1, keepdims=True)
        return jnp.where(head0, s0, s1)

    def stack(t):
        parts = []
        for b in range(nb):
            tb = t[b * c:(b + 1) * c]
            parts += [tb * mk[0], tb * mk[1]]
        return jnp.concatenate(parts, axis=0).astype(BF16)

    ri = lax.broadcasted_iota(jnp.int32, (rows, rows), 0)
    ci = lax.broadcasted_iota(jnp.int32, (rows, rows), 1)
    tri = jnp.where((ri // c == ci // c) & (ri >= ci), 1.0, 0.0).astype(BF16)
    lw_hi = [lw[p].astype(BF16) for p in pairs]
    cum = [_mm_nn(tri, lw_hi[p]) + _mm_nn(tri, lw[p] - lw_hi[p].astype(F32)) for p in pairs]

    kkr = [k[p] * k_k[p] for p in pairs]
    kk = [kkr[p] / jnp.maximum(jnp.sqrt(head_sum(kkr[p] * kkr[p])), 1e-12) for p in pairs]
    kf = [k[p] * (1.0 + (a[p] - 1.0) * k_a[p]) for p in pairs]
    g_incl = [jnp.exp(cum[p]) for p in pairs]
    g_inv = [jnp.exp(-cum[p]) for p in pairs]
    at_s = [stack(-kk[p] * jnp.exp(cum[p] - lw[p])) for p in pairs]
    rt_s = [stack(r[p] * g_incl[p]) for p in pairs]
    bt_s = [stack(kk[p] * a[p] * g_inv[p]) for p in pairs]
    kt_s = [stack(kf[p] * g_inv[p]) for p in pairs]
    v_s = [stack(v[p]) for p in pairs]
    xa = [jnp.concatenate([at_s[p], rt_s[p]], axis=0) for p in pairs]
    yb = [jnp.concatenate([bt_s[p], kt_s[p]], axis=0) for p in pairs]

    si = lax.broadcasted_iota(jnp.int32, (n, n), 0)
    sj = lax.broadcasted_iota(jnp.int32, (n, n), 1)
    same = si // c == sj // c
    strict = same & (si > sj)
    incl = same & (si >= sj)
    eye = jnp.where(si == sj, 1.0, 0.0).astype(F32)

    gm = [_mm_nt(xa[p], yb[p]) for p in pairs]
    a_ab = [jnp.where(strict, gm[p][:n, :n], 0.0) for p in pairs]
    a_kr = [jnp.concatenate([jnp.where(strict, gm[p][:n, n:], 0.0),
                             jnp.where(incl, gm[p][n:, n:], 0.0)], axis=0).astype(BF16) for p in pairs]
    a_rb = [jnp.where(incl, gm[p][n:, :n], 0.0).astype(BF16) for p in pairs]

    tinv = [eye + a_ab[p] for p in pairs]
    pw = [a_ab[p].astype(BF16) for p in pairs]
    span = 2
    while span < c:
        pw = [_mm_nn(pw[p], pw[p]).astype(BF16) for p in pairs]
        tinv = [tinv[p] + _mm_nn(tinv[p], pw[p]) for p in pairs]
        span *= 2

    av = [_mm_nn(a_kr[p], v_s[p]) for p in pairs]
    if nb == 1:
        ah = [_mm_nt(xa[p], states[p][0]) for p in pairs]
        ah_a = [ah[p][:n] for p in pairs]
        ah_r = [ah[p][n:] for p in pairs]
    else:
        ah_a, ah_r = [], []
        for p in pairs:
            parts = []
            for b in range(nb):
                sl = slice(b * hb, (b + 1) * hb)
                parts.append(_mm_nt(jnp.concatenate([at_s[p][sl], rt_s[p][sl]], axis=0), states[p][b]))
            ah_a.append(jnp.concatenate([t[:hb] for t in parts], axis=0))
            ah_r.append(jnp.concatenate([t[hb:] for t in parts], axis=0))

    u = [_mm_nn(tinv[p], ah_a[p] + av[p][:n]).astype(BF16) for p in pairs]
    y_s = [ah_r[p] + av[p][n:] + _mm_nn(a_rb[p], u[p]) for p in pairs]

    new_states = []
    for p in pairs:
        per_batch = []
        for b in range(nb):
            sl = slice(b * hb, (b + 1) * hb)
            upd = _mm_tn(jnp.concatenate([u[p][sl], v_s[p][sl]], axis=0),
                         jnp.concatenate([bt_s[p][sl], kt_s[p][sl]], axis=0))
            g_last = g_incl[p][(b + 1) * c - 1:(b + 1) * c]
            per_batch.append((states[p][b] + upd) * g_last)
        new_states.append(per_batch)

    inv_n = 1.0 / RWKV_HEAD
    out = []
    for p in pairs:
        y = jnp.concatenate([y_s[p][b * hb:b * hb + c] + y_s[p][b * hb + c:(b + 1) * hb] for b in range(nb)],
                            axis=0)
        mean = head_sum(y) * inv_n
        yc = y - mean
        var = head_sum(yc * yc) * inv_n
        yn = yc * lax.rsqrt(var + GN_EPS) * gn_g[p] + gn_b[p]
        out.append(yn + head_sum(r[p] * kf[p] * r_k[p]) * v[p])
    return out, new_states


def _pair_lanes(p):
    return slice(p * V7X_LANES, (p + 1) * V7X_LANES)


def _state_to_blockdiag(s0, s1):
    z = jnp.zeros_like(s0)
    return jnp.concatenate([jnp.concatenate([s0, z], axis=1), jnp.concatenate([z, s1], axis=1)], axis=0)


def _pair_views(n_pairs, *refs):
    return [[ref[:, _pair_lanes(p)] for p in range(n_pairs)] for ref in refs]


def _wkv_prompt_kernel(r_ref, k_ref, v_ref, lw_ref, a_ref, kk_ref, ka_ref, rk_ref, gg_ref, gb_ref,
                       y_ref, sfin_ref, s_scr, *, n_pairs, c):
    ci = pl.program_id(1)

    @pl.when(ci == 0)
    def _():
        s_scr[...] = jnp.zeros_like(s_scr)

    args = _pair_views(n_pairs, r_ref, k_ref, v_ref, lw_ref, a_ref, kk_ref, ka_ref, rk_ref, gg_ref, gb_ref)
    yn, new = _wkv_chunk(*args, [[s_scr[p]] for p in range(n_pairs)], 1, c)
    for p in range(n_pairs):
        y_ref[:, _pair_lanes(p)] = yn[p]
        s_scr[p] = new[p][0]

    @pl.when(ci == pl.num_programs(1) - 1)
    def _():
        for p in range(n_pairs):
            s = s_scr[p]
            sfin_ref[0, 2 * p] = s[:RWKV_HEAD, :RWKV_HEAD]
            sfin_ref[0, 2 * p + 1] = s[RWKV_HEAD:, RWKV_HEAD:]


def _wkv_sample_kernel(r_ref, k_ref, v_ref, lw_ref, a_ref, kk_ref, ka_ref, rk_ref, gg_ref, gb_ref,
                       s0_ref, y_ref, sfin_ref, *, n_pairs, nb, c):
    args = _pair_views(n_pairs, r_ref, k_ref, v_ref, lw_ref, a_ref, kk_ref, ka_ref, rk_ref, gg_ref, gb_ref)
    states = [[_state_to_blockdiag(s0_ref[b, 2 * p], s0_ref[b, 2 * p + 1]) for b in range(nb)]
              for p in range(n_pairs)]
    yn, new = _wkv_chunk(*args, states, nb, c)
    for p in range(n_pairs):
        y_ref[:, _pair_lanes(p)] = yn[p]
        for b in range(nb):
            sfin_ref[b, 2 * p] = new[p][b][:RWKV_HEAD, :RWKV_HEAD]
            sfin_ref[b, 2 * p + 1] = new[p][b][RWKV_HEAD:, RWKV_HEAD:]


def _wkv_prompt(r, k, v, lw, a, k_k, k_a, r_k, gn_g, gn_b, batch, seq):
    rows, d = r.shape
    heads = d // RWKV_HEAD
    c = PROMPT_CHUNK
    nc = seq // c
    row_spec = pl.BlockSpec((c, d), lambda b, i: (b * nc + i, 0))
    vec = _const_spec((1, d))
    kern = functools.partial(_wkv_prompt_kernel, n_pairs=heads // HEADS_PER_VREG, c=c)
    return pl.pallas_call(
        kern,
        grid=(batch, nc),
        in_specs=[row_spec] * 5 + [vec] * 5,
        out_specs=[row_spec, pl.BlockSpec((1, heads, RWKV_HEAD, RWKV_HEAD), lambda b, i: (b, 0, 0, 0))],
        out_shape=[jax.ShapeDtypeStruct((rows, d), F32),
                   jax.ShapeDtypeStruct((batch, heads, RWKV_HEAD, RWKV_HEAD), F32)],
        scratch_shapes=[pltpu.VMEM((heads // HEADS_PER_VREG, V7X_LANES, V7X_LANES), F32)],
        compiler_params=_cparams("parallel", "arbitrary"),
        name="wkv_prompt",
    )(r, k, v, lw, a, k_k, k_a, r_k, gn_g, gn_b)


def _wkv_sample(r, k, v, lw, a, k_k, k_a, r_k, gn_g, gn_b, state, batch, seq):
    rows, d = r.shape
    heads = d // RWKV_HEAD
    nb = WKV_ROWS // seq
    row_spec = pl.BlockSpec((WKV_ROWS, d), lambda i: (i, 0))
    st_spec = pl.BlockSpec((nb, heads, RWKV_HEAD, RWKV_HEAD), lambda i: (i, 0, 0, 0))
    vec = _const_spec((1, d))
    kern = functools.partial(_wkv_sample_kernel, n_pairs=heads // HEADS_PER_VREG, nb=nb, c=seq)
    return pl.pallas_call(
        kern,
        grid=(batch // nb,),
        in_specs=[row_spec] * 5 + [vec] * 5 + [st_spec],
        out_specs=[row_spec, st_spec],
        out_shape=[jax.ShapeDtypeStruct((rows, d), F32),
                   jax.ShapeDtypeStruct((batch, heads, RWKV_HEAD, RWKV_HEAD), F32)],
        compiler_params=_cparams("parallel"),
        name="wkv_sample",
    )(r, k, v, lw, a, k_k, k_a, r_k, gn_g, gn_b, state)


def _post_kernel(y_ref, g_ref, x_ref, w_ref, lng_ref, lnb_ref, o_ref):
    g = g_ref[...]
    h = (y_ref[...] * (g / (1.0 + jnp.exp(-g)))).astype(BF16)
    z = ALPHA * x_ref[...] + jnp.dot(h, w_ref[...], preferred_element_type=F32)
    mu = jnp.mean(z, axis=-1, keepdims=True)
    zc = z - mu
    var = jnp.mean(zc * zc, axis=-1, keepdims=True)
    o_ref[...] = zc * lax.rsqrt(var + LN_EPS) * lng_ref[...] + lnb_ref[...]


def _post(y, g, x2, w_out, ln_g, ln_b, tm):
    rows, d = x2.shape
    din = y.shape[1]
    in_spec = pl.BlockSpec((tm, din), lambda i: (i, 0))
    row_spec = pl.BlockSpec((tm, d), lambda i: (i, 0))
    return pl.pallas_call(
        _post_kernel,
        grid=(rows // tm,),
        in_specs=[in_spec, in_spec, row_spec, _const_spec((din, d)), _const_spec((1, d)), _const_spec((1, d))],
        out_specs=row_spec,
        out_shape=jax.ShapeDtypeStruct((rows, d), F32),
        compiler_params=_cparams("parallel"),
        name="branch_post",
    )(y, g, x2, w_out.astype(BF16), ln_g.reshape(1, d), ln_b.reshape(1, d))


def _att_pre_kernel(x_ref, w_ref, q_ref, k_ref, v_ref, g_ref, *, hq, hkv):
    z = jnp.dot(x_ref[...].astype(BF16), w_ref[...], preferred_element_type=F32)
    q_ref[...] = z[:, :hq]
    k_ref[...] = z[:, hq:hq + hkv]
    v_ref[...] = z[:, hq + hkv:hq + 2 * hkv]
    g_ref[...] = z[:, hq + 2 * hkv:]


def _att_pre(x2, w_in, tm):
    rows, d = x2.shape
    hq = ATT_KV_HEADS * ATT_GROUP * ATT_HEAD_DIM
    hkv = ATT_KV_HEADS * ATT_HEAD_DIM
    wide = pl.BlockSpec((tm, hq), lambda i: (i, 0))
    narrow = pl.BlockSpec((tm, hkv), lambda i: (i, 0))
    return pl.pallas_call(
        functools.partial(_att_pre_kernel, hq=hq, hkv=hkv),
        grid=(rows // tm,),
        in_specs=[pl.BlockSpec((tm, d), lambda i: (i, 0)), _const_spec(w_in.shape)],
        out_specs=[wide, narrow, narrow, wide],
        out_shape=[jax.ShapeDtypeStruct((rows, hq), F32), jax.ShapeDtypeStruct((rows, hkv), F32),
                   jax.ShapeDtypeStruct((rows, hkv), F32), jax.ShapeDtypeStruct((rows, hq), F32)],
        compiler_params=_cparams("parallel"),
        name="att_pre",
    )(x2, w_in.astype(BF16))


def _alibi_slope(h):
    n_heads = ATT_KV_HEADS * ATT_GROUP
    return 2.0 ** (-8.0 * (h + 1) / n_heads)


def _attend(q, kcat, vcat, slope, sink, dist, live):
    s = lax.dot_general(q, kcat, (((1,), (1,)), ((), ())), preferred_element_type=F32) * ATT_SCALE
    s = jnp.where(live, s - slope * dist, NEG)
    m = jnp.maximum(jnp.max(s, axis=-1, keepdims=True), sink)
    p = jnp.exp(s - m)
    den = jnp.sum(p, axis=-1, keepdims=True) + jnp.exp(sink - m)
    return jnp.dot(p.astype(BF16), vcat, preferred_element_type=F32) / den


def _att_prompt_kernel(sink_ref, q_ref, kc_ref, vc_ref, kp_ref, vp_ref, o_ref):
    blk = pl.program_id(1)
    qi = lax.broadcasted_iota(jnp.int32, (WINDOW, 2 * WINDOW), 0)
    kj = lax.broadcasted_iota(jnp.int32, (WINDOW, 2 * WINDOW), 1)
    dist_i = qi + WINDOW - kj
    live = (dist_i >= 0) & (dist_i < WINDOW) & ((kj >= WINDOW) | (blk > 0))
    dist = dist_i.astype(F32)
    hd = ATT_HEAD_DIM
    for kvh in range(ATT_KV_HEADS):
        ks = slice(kvh * hd, (kvh + 1) * hd)
        kcat = jnp.concatenate([kp_ref[:, ks], kc_ref[:, ks]], axis=0).astype(BF16)
        vcat = jnp.concatenate([vp_ref[:, ks], vc_ref[:, ks]], axis=0).astype(BF16)
        for g in range(ATT_GROUP):
            h = kvh * ATT_GROUP + g
            hs = slice(h * hd, (h + 1) * hd)
            o_ref[:, hs] = _attend(q_ref[:, hs].astype(BF16), kcat, vcat, _alibi_slope(h), sink_ref[h],
                                   dist, live)


def _att_prompt(q, k, v, sinks, batch, seq):
    rows, hq = q.shape
    hkv = k.shape[1]
    nb = seq // WINDOW
    cur = lambda b, n: (b * nb + n, 0)
    prev = lambda b, n: (b * nb + jnp.maximum(n - 1, 0), 0)
    return pl.pallas_call(
        _att_prompt_kernel,
        grid=(batch, nb),
        in_specs=[pl.BlockSpec(memory_space=pltpu.SMEM),
                  pl.BlockSpec((WINDOW, hq), cur),
                  pl.BlockSpec((WINDOW, hkv), cur), pl.BlockSpec((WINDOW, hkv), cur),
                  pl.BlockSpec((WINDOW, hkv), prev), pl.BlockSpec((WINDOW, hkv), prev)],
        out_specs=pl.BlockSpec((WINDOW, hq), cur),
        out_shape=jax.ShapeDtypeStruct((rows, hq), F32),
        compiler_params=_cparams("parallel", "parallel"),
        name="att_prompt",
    )(sinks, q, k, v, k, v)


def _att_sample_kernel(sink_ref, q_ref, kn_ref, vn_ref, ck_ref, cv_ref, o_ref, *, nb, t):
    m = ATT_GROUP * t
    ri = lax.broadcasted_iota(jnp.int32, (m, 1), 0)
    grp = ri // t
    qi = lax.broadcasted_iota(jnp.int32, (m, 2 * WINDOW), 0) % t
    kj = lax.broadcasted_iota(jnp.int32, (m, 2 * WINDOW), 1)
    dist_i = qi + WINDOW - kj
    live = (dist_i >= 0) & (dist_i < WINDOW)
    dist = dist_i.astype(F32)
    hd = ATT_HEAD_DIM
    pad = jnp.zeros((WINDOW - t, hd), F32)
    for kvh in range(ATT_KV_HEADS):
        ks = slice(kvh * hd, (kvh + 1) * hd)
        slope = jnp.zeros((m, 1), F32)
        sink = jnp.zeros((m, 1), F32)
        for g in range(ATT_GROUP):
            h = kvh * ATT_GROUP + g
            slope = jnp.where(grp == g, _alibi_slope(h), slope)
            sink = jnp.where(grp == g, sink_ref[h], sink)
        for b in range(nb):
            rs = slice(b * t, (b + 1) * t)
            kcat = jnp.concatenate([ck_ref[b, :, ks], kn_ref[rs, ks], pad], axis=0).astype(BF16)
            vcat = jnp.concatenate([cv_ref[b, :, ks], vn_ref[rs, ks], pad], axis=0).astype(BF16)
            qs = jnp.concatenate(
                [q_ref[rs, (kvh * ATT_GROUP + g) * hd:(kvh * ATT_GROUP + g + 1) * hd] for g in range(ATT_GROUP)],
                axis=0).astype(BF16)
            o = _attend(qs, kcat, vcat, slope, sink, dist, live)
            for g in range(ATT_GROUP):
                h = kvh * ATT_GROUP + g
                o_ref[rs, h * hd:(h + 1) * hd] = o[g * t:(g + 1) * t]


def _att_sample(q, k, v, cache_k, cache_v, sinks, batch, t, nb):
    rows, hq = q.shape
    hkv = k.shape[1]
    row = lambda i: (i, 0)
    cache_spec = pl.BlockSpec((nb, WINDOW, hkv), lambda i: (i, 0, 0))
    return pl.pallas_call(
        functools.partial(_att_sample_kernel, nb=nb, t=t),
        grid=(batch // nb,),
        in_specs=[pl.BlockSpec(memory_space=pltpu.SMEM),
                  pl.BlockSpec((nb * t, hq), row), pl.BlockSpec((nb * t, hkv), row),
                  pl.BlockSpec((nb * t, hkv), row), cache_spec, cache_spec],
        out_specs=pl.BlockSpec((nb * t, hq), row),
        out_shape=jax.ShapeDtypeStruct((rows, hq), F32),
        compiler_params=_cparams("parallel"),
        name="att_sample",
    )(sinks, q, k, v, cache_k, cache_v)


def _row_tile(rows):
    return 256 if rows % 256 == 0 else rows


def _shifted(x, first):
    return jnp.concatenate([first[:, None, :], x[:, :-1]], axis=1)


def kernel(x_prompt, x_sample, state_wkv, state_shift, cache_win_k, cache_win_v, ln_g, ln_b, rwkv_mu, rwkv_w_in, rwkv_w0, rwkv_w1, rwkv_w2, rwkv_a0, rwkv_a1, rwkv_a2, rwkv_k_k, rwkv_k_a, rwkv_r_k, rwkv_gn_g, rwkv_gn_b, rwkv_w_out, att_w_in, att_sinks, att_w_out):
    bp, tp, d = x_prompt.shape
    bs, ts, _ = x_sample.shape
    assert tp % PROMPT_CHUNK == 0 and tp % WINDOW == 0 and WKV_ROWS % ts == 0 and bs % (WKV_ROWS // ts) == 0
    assert cache_win_k.shape[2] == WINDOW
    heads = d // RWKV_HEAD
    hkv = ATT_KV_HEADS * ATT_HEAD_DIM

    j = 0
    vec = lambda t: t.reshape(1, d)
    pre_w = (rwkv_mu[j], rwkv_w_in[j], rwkv_w0[j], rwkv_w1[j], rwkv_w2[j], rwkv_a0[j], rwkv_a1[j], rwkv_a2[j])
    head_w = (vec(rwkv_k_k[j]), vec(rwkv_k_a[j]), vec(rwkv_r_k[j]), vec(rwkv_gn_g[j]), vec(rwkv_gn_b[j]))

    xp2 = x_prompt.reshape(bp * tp, d)
    xs2 = x_sample.reshape(bs * ts, d)
    xp_prev = _shifted(x_prompt, jnp.zeros((bp, d), F32)).reshape(bp * tp, d)
    xs_prev = _shifted(x_sample, state_shift[j]).reshape(bs * ts, d)

    r, k, v, g, lw, a = _rwkv_pre(xp2, xp_prev, *pre_w, tm=_row_tile(bp * tp))
    yp, wkv_p = _wkv_prompt(r, k, v, lw, a, *head_w, batch=bp, seq=tp)
    xp1 = _post(yp, g, xp2, rwkv_w_out[j], ln_g[0], ln_b[0], tm=_row_tile(bp * tp))

    r, k, v, g, lw, a = _rwkv_pre(xs2, xs_prev, *pre_w, tm=_row_tile(bs * ts))
    ys, wkv_s = _wkv_sample(r, k, v, lw, a, *head_w, state_wkv[j], batch=bs, seq=ts)
    xs1 = _post(ys, g, xs2, rwkv_w_out[j], ln_g[0], ln_b[0], tm=_row_tile(bs * ts))

    q, k, v, gate = _att_pre(xp1, att_w_in[j], tm=_row_tile(bp * tp))
    o = _att_prompt(q, k, v, att_sinks[j], batch=bp, seq=tp)
    y_prompt = _post(o, gate, xp1, att_w_out[j], ln_g[1], ln_b[1], tm=_row_tile(bp * tp)).reshape(bp, tp, d)
    win_k_p = k.reshape(bp, tp, ATT_KV_HEADS, ATT_HEAD_DIM)[:, tp - WINDOW:]
    win_v_p = v.reshape(bp, tp, ATT_KV_HEADS, ATT_HEAD_DIM)[:, tp - WINDOW:]

    q, k, v, gate = _att_pre(xs1, att_w_in[j], tm=_row_tile(bs * ts))
    ck = cache_win_k[j].reshape(bs, WINDOW, hkv)
    cv = cache_win_v[j].reshape(bs, WINDOW, hkv)
    o = _att_sample(q, k, v, ck, cv, att_sinks[j], batch=bs, t=ts, nb=WKV_ROWS // ts)
    y_sample = _post(o, gate, xs1, att_w_out[j], ln_g[1], ln_b[1], tm=_row_tile(bs * ts)).reshape(bs, ts, d)
    win_k_s = jnp.concatenate([ck[:, ts:], k.reshape(bs, ts, hkv)], axis=1).reshape(bs, WINDOW, ATT_KV_HEADS, ATT_HEAD_DIM)
    win_v_s = jnp.concatenate([cv[:, ts:], v.reshape(bs, ts, hkv)], axis=1).reshape(bs, WINDOW, ATT_KV_HEADS, ATT_HEAD_DIM)

    return (y_prompt, y_sample,
            wkv_p[None], x_prompt[:, -1][None], win_k_p[None], win_v_p[None],
            wkv_s[None], x_sample[:, -1][None], win_k_s[None], win_v_s[None])
```

```python
import functools

import jax
import jax.numpy as jnp
from jax import lax
from jax.experimental import pallas as pl
from jax.experimental.pallas import tpu as pltpu

F32 = jnp.float32
BF16 = jnp.bfloat16

RWKV_HEAD = 64
ATT_HEAD_DIM = 64
ATT_KV_HEADS = 4
ATT_GROUP = 4
WINDOW = 128
ATT_SCALE = ATT_HEAD_DIM ** -0.5
NEG = -1e30
DEPTH = 2
ALPHA = (2 * DEPTH) ** 0.25
LN_EPS = 1e-5
GN_EPS = 64e-5

V7X_LANES = 128
V7X_SUBLANES = 8
V7X_VMEM_LIMIT_BYTES = 56 * 1024 * 1024

HEADS_PER_VREG = V7X_LANES // RWKV_HEAD
STACK_ROWS = 128
WKV_ROWS = STACK_ROWS // HEADS_PER_VREG
PROMPT_CHUNK = WKV_ROWS


def _cparams(*sem):
    return pltpu.CompilerParams(dimension_semantics=sem, vmem_limit_bytes=V7X_VMEM_LIMIT_BYTES)


def _const_spec(shape):
    nd = len(shape)
    return pl.BlockSpec(shape, lambda *_: (0,) * nd)


def _rwkv_pre_kernel(x_ref, first_ref, halo_ref, mu_ref, win_ref, w0_ref, w1_ref, w2_ref, a0_ref, a1_ref, a2_ref,
                     r_ref, k_ref, v_ref, g_ref, lw_ref, a_ref, *, tiles_per_seq):
    x3 = x_ref[...]
    nseq, rows, d = x3.shape
    first = first_ref[...]
    if tiles_per_seq > 1:
        first = jnp.where(pl.program_id(0) % tiles_per_seq == 0, first, halo_ref[:, -1:, :])
    t_idx = lax.broadcasted_iota(jnp.int32, (1, rows, 1), 1)
    x_prev = jnp.where(t_idx == 0, first, pltpu.roll(x3, 1, axis=1))
    x = x3.reshape(nseq * rows, d)
    xx = x_prev.reshape(nseq * rows, d) - x

    def mix(p):
        return (x + xx * mu_ref[p:p + 1, :]).astype(BF16)

    def proj(p):
        return jnp.dot(mix(p), win_ref[p], preferred_element_type=F32)

    r_ref[...] = proj(0)
    k_ref[...] = proj(1)
    v_ref[...] = proj(2)
    g_ref[...] = proj(3)
    hw = jnp.tanh(jnp.dot(mix(4), w1_ref[...], preferred_element_type=F32))
    wl = w0_ref[...] + jnp.dot(hw.astype(BF16), w2_ref[...], preferred_element_type=F32)
    z = -wl
    softplus = jnp.maximum(z, 0.0) + jnp.log(1.0 + jnp.exp(-jnp.abs(z)))
    lw_ref[...] = -jnp.exp(-softplus - 0.5)
    ha = jnp.dot(mix(5), a1_ref[...], preferred_element_type=F32)
    al = a0_ref[...] + jnp.dot(ha.astype(BF16), a2_ref[...], preferred_element_type=F32)
    a_ref[...] = 1.0 / (1.0 + jnp.exp(-al))


def _rwkv_pre(x3, first, mu, w_in, w0, w1, w2, a0, a1, a2, tm):
    nseq, seq, d = x3.shape
    rows = nseq * seq
    lora = w1.shape[1]
    if seq >= tm:
        tps = seq // tm
        x_spec = pl.BlockSpec((1, tm, d), lambda i: (i // tps, i % tps, 0))
        first_spec = pl.BlockSpec((1, 1, d), lambda i: (i // tps, 0, 0))
        halo_spec = pl.BlockSpec(
            (1, V7X_SUBLANES, d),
            lambda i: (i // tps, jnp.maximum((i % tps) * (tm // V7X_SUBLANES) - 1, 0), 0))
    else:
        tps = 1
        x_spec = pl.BlockSpec((tm // seq, seq, d), lambda i: (i, 0, 0))
        first_spec = pl.BlockSpec((tm // seq, 1, d), lambda i: (i, 0, 0))
        halo_spec = pl.BlockSpec((1, V7X_SUBLANES, d), lambda i: (0, 0, 0))
    row_spec = pl.BlockSpec((tm, d), lambda i: (i, 0))
    out = jax.ShapeDtypeStruct((rows, d), F32)
    return pl.pallas_call(
        functools.partial(_rwkv_pre_kernel, tiles_per_seq=tps),
        grid=(rows // tm,),
        in_specs=[x_spec, first_spec, halo_spec, _const_spec((6, d)), _const_spec((4, d, d)),
                  _const_spec((1, d)), _const_spec((d, lora)), _const_spec((lora, d)), _const_spec((1, d)),
                  _const_spec((d, lora)), _const_spec((lora, d))],
        out_specs=[row_spec] * 6,
        out_shape=[out] * 6,
        compiler_params=_cparams("parallel"),
        name="rwkv_pre",
    )(x3, first.reshape(nseq, 1, d), x3, mu, w_in.astype(BF16), w0.reshape(1, d), w1.astype(BF16),
      w2.astype(BF16), a0.reshape(1, d), a1.astype(BF16), a2.astype(BF16))


def _mm(a, b, dims):
    return lax.dot_general(a.astype(BF16), b.astype(BF16), (dims, ((), ())), preferred_element_type=F32)


def _mm_nn(a, b):
    return _mm(a, b, ((1,), (0,)))


def _mm_nt(a, b):
    return _mm(a, b, ((1,), (1,)))


def _mm_tn(a, b):
    return _mm(a, b, ((0,), (0,)))


def _wkv_chunk(r, k, v, lw, a, k_k, k_a, r_k, gn_g, gn_b, states, nb, c):
    pairs = range(len(r))
    rows = nb * c
    n = STACK_ROWS
    hb = 2 * c
    lane = lax.broadcasted_iota(jnp.int32, (1, V7X_LANES), 1)
    head0 = lane < RWKV_HEAD
    mk = (jnp.where(head0, 1.0, 0.0).astype(F32), jnp.where(head0, 0.0, 1.0).astype(F32))

    def head_sum(t):
        s0 = jnp.sum(t * mk[0], axis=-1, keepdims=True)
        s1 = jnp.sum(t * mk[1], axis=-1, keepdims=True)
        return jnp.where(head0, s0, s1)

    def stack(t):
        parts = []
        for b in range(nb):
            tb = t[b * c:(b + 1) * c]
            parts += [tb * mk[0], tb * mk[1]]
        return jnp.concatenate(parts, axis=0).astype(BF16)

    ri = lax.broadcasted_iota(jnp.int32, (rows, rows), 0)
    ci = lax.broadcasted_iota(jnp.int32, (rows, rows), 1)
    tri = jnp.where((ri // c == ci // c) & (ri >= ci), 1.0, 0.0).astype(BF16)
    lw_hi = [lw[p].astype(BF16) for p in pairs]
    cum = [_mm_nn(tri, lw_hi[p]) + _mm_nn(tri, lw[p] - lw_hi[p].astype(F32)) for p in pairs]

    kkr = [k[p] * k_k[p] for p in pairs]
    kk = [kkr[p] / jnp.maximum(jnp.sqrt(head_sum(kkr[p] * kkr[p])), 1e-12) for p in pairs]
    kf = [k[p] * (1.0 + (a[p] - 1.0) * k_a[p]) for p in pairs]
    g_incl = [jnp.exp(cum[p]) for p in pairs]
    g_inv = [jnp.exp(-cum[p]) for p in pairs]
    at_s = [stack(-kk[p] * jnp.exp(cum[p] - lw[p])) for p in pairs]
    rt_s = [stack(r[p] * g_incl[p]) for p in pairs]
    bt_s = [stack(kk[p] * a[p] * g_inv[p]) for p in pairs]
    kt_s = [stack(kf[p] * g_inv[p]) for p in pairs]
    v_s = [stack(v[p]) for p in pairs]
    xa = [jnp.concatenate([at_s[p], rt_s[p]], axis=0) for p in pairs]
    yb = [jnp.concatenate([bt_s[p], kt_s[p]], axis=0) for p in pairs]

    si = lax.broadcasted_iota(jnp.int32, (n, n), 0)
    sj = lax.broadcasted_iota(jnp.int32, (n, n), 1)
    same = si // c == sj // c
    strict = same & (si > sj)
    incl = same & (si >= sj)
    eye = jnp.where(si == sj, 1.0, 0.0).astype(F32)

    gm = [_mm_nt(xa[p], yb[p]) for p in pairs]
    a_ab = [jnp.where(strict, gm[p][:n, :n], 0.0) for p in pairs]
    a_kr = [jnp.concatenate([jnp.where(strict, gm[p][:n, n:], 0.0),
                             jnp.where(incl, gm[p][n:, n:], 0.0)], axis=0).astype(BF16) for p in pairs]
    a_rb = [jnp.where(incl, gm[p][n:, :n], 0.0).astype(BF16) for p in pairs]

    tinv = [eye + a_ab[p] for p in pairs]
    pw = [a_ab[p].astype(BF16) for p in pairs]
    span = 2
    while span < c:
        pw = [_mm_nn(pw[p], pw[p]).astype(BF16) for p in pairs]
        tinv = [tinv[p] + _mm_nn(tinv[p], pw[p]) for p in pairs]
        span *= 2

    av = [_mm_nn(a_kr[p], v_s[p]) for p in pairs]
    if nb == 1:
        ah = [_mm_nt(xa[p], states[p][0]) for p in pairs]
        ah_a = [ah[p][:n] for p in pairs]
        ah_r = [ah[p][n:] for p in pairs]
    else:
        ah_a, ah_r = [], []
        for p in pairs:
            parts = []
            for b in range(nb):
                sl = slice(b * hb, (b + 1) * hb)
                parts.append(_mm_nt(jnp.concatenate([at_s[p][sl], rt_s[p][sl]], axis=0), states[p][b]))
            ah_a.append(jnp.concatenate([t[:hb] for t in parts], axis=0))
            ah_r.append(jnp.concatenate([t[hb:] for t in parts], axis=0))

    u = [_mm_nn(tinv[p], ah_a[p] + av[p][:n]).astype(BF16) for p in pairs]
    y_s = [ah_r[p] + av[p][n:] + _mm_nn(a_rb[p], u[p]) for p in pairs]

    new_states = []
    for p in pairs:
        per_batch = []
        for b in range(nb):
            sl = slice(b * hb, (b + 1) * hb)
            upd = _mm_tn(jnp.concatenate([u[p][sl], v_s[p][sl]], axis=0),
                         jnp.concatenate([bt_s[p][sl], kt_s[p][sl]], axis=0))
            g_last = g_incl[p][(b + 1) * c - 1:(b + 1) * c]
            per_batch.append((states[p][b] + upd) * g_last)
        new_states.append(per_batch)

    inv_n = 1.0 / RWKV_HEAD
    out = []
    for p in pairs:
        y = jnp.concatenate([y_s[p][b * hb:b * hb + c] + y_s[p][b * hb + c:(b + 1) * hb] for b in range(nb)],
                            axis=0)
        mean = head_sum(y) * inv_n
        yc = y - mean
        var = head_sum(yc * yc) * inv_n
        yn = yc * lax.rsqrt(var + GN_EPS) * gn_g[p] + gn_b[p]
        out.append(yn + head_sum(r[p] * kf[p] * r_k[p]) * v[p])
    return out, new_states


def _pair_lanes(p):
    return slice(p * V7X_LANES, (p + 1) * V7X_LANES)


def _state_to_blockdiag(s0, s1):
    z = jnp.zeros_like(s0)
    return jnp.concatenate([jnp.concatenate([s0, z], axis=1), jnp.concatenate([z, s1], axis=1)], axis=0)


def _pair_views(n_pairs, *refs):
    return [[ref[:, _pair_lanes(p)] for p in range(n_pairs)] for ref in refs]


def _wkv_prompt_kernel(r_ref, k_ref, v_ref, lw_ref, a_ref, kk_ref, ka_ref, rk_ref, gg_ref, gb_ref,
                       y_ref, sfin_ref, s_scr, *, n_pairs, c):
    ci = pl.program_id(1)

    @pl.when(ci == 0)
    def _():
        s_scr[...] = jnp.zeros_like(s_scr)

    args = _pair_views(n_pairs, r_ref, k_ref, v_ref, lw_ref, a_ref, kk_ref, ka_ref, rk_ref, gg_ref, gb_ref)
    yn, new = _wkv_chunk(*args, [[s_scr[p]] for p in range(n_pairs)], 1, c)
    for p in range(n_pairs):
        y_ref[:, _pair_lanes(p)] = yn[p]
        s_scr[p] = new[p][0]

    @pl.when(ci == pl.num_programs(1) - 1)
    def _():
        for p in range(n_pairs):
            s = s_scr[p]
            sfin_ref[0, 2 * p] = s[:RWKV_HEAD, :RWKV_HEAD]
            sfin_ref[0, 2 * p + 1] = s[RWKV_HEAD:, RWKV_HEAD:]


def _wkv_sample_kernel(r_ref, k_ref, v_ref, lw_ref, a_ref, kk_ref, ka_ref, rk_ref, gg_ref, gb_ref,
                       s0_ref, y_ref, sfin_ref, *, n_pairs, nb, c):
    args = _pair_views(n_pairs, r_ref, k_ref, v_ref, lw_ref, a_ref, kk_ref, ka_ref, rk_ref, gg_ref, gb_ref)
    states = [[_state_to_blockdiag(s0_ref[b, 2 * p], s0_ref[b, 2 * p + 1]) for b in range(nb)]
              for p in range(n_pairs)]
    yn, new = _wkv_chunk(*args, states, nb, c)
    for p in range(n_pairs):
        y_ref[:, _pair_lanes(p)] = yn[p]
        for b in range(nb):
            sfin_ref[b, 2 * p] = new[p][b][:RWKV_HEAD, :RWKV_HEAD]
            sfin_ref[b, 2 * p + 1] = new[p][b][RWKV_HEAD:, RWKV_HEAD:]


def _wkv_prompt(r, k, v, lw, a, k_k, k_a, r_k, gn_g, gn_b, batch, seq):
    rows, d = r.shape
    heads = d // RWKV_HEAD
    c = PROMPT_CHUNK
    nc = seq // c
    row_spec = pl.BlockSpec((c, d), lambda b, i: (b * nc + i, 0))
    vec = _const_spec((1, d))
    kern = functools.partial(_wkv_prompt_kernel, n_pairs=heads // HEADS_PER_VREG, c=c)
    return pl.pallas_call(
        kern,
        grid=(batch, nc),
        in_specs=[row_spec] * 5 + [vec] * 5,
        out_specs=[row_spec, pl.BlockSpec((1, heads, RWKV_HEAD, RWKV_HEAD), lambda b, i: (b, 0, 0, 0))],
        out_shape=[jax.ShapeDtypeStruct((rows, d), F32),
                   jax.ShapeDtypeStruct((batch, heads, RWKV_HEAD, RWKV_HEAD), F32)],
        scratch_shapes=[pltpu.VMEM((heads // HEADS_PER_VREG, V7X_LANES, V7X_LANES), F32)],
        compiler_params=_cparams("parallel", "arbitrary"),
        name="wkv_prompt",
    )(r, k, v, lw, a, k_k, k_a, r_k, gn_g, gn_b)


def _wkv_sample(r, k, v, lw, a, k_k, k_a, r_k, gn_g, gn_b, state, batch, seq):
    rows, d = r.shape
    heads = d // RWKV_HEAD
    nb = WKV_ROWS // seq
    row_spec = pl.BlockSpec((WKV_ROWS, d), lambda i: (i, 0))
    st_spec = pl.BlockSpec((nb, heads, RWKV_HEAD, RWKV_HEAD), lambda i: (i, 0, 0, 0))
    vec = _const_spec((1, d))
    kern = functools.partial(_wkv_sample_kernel, n_pairs=heads // HEADS_PER_VREG, nb=nb, c=seq)
    return pl.pallas_call(
        kern,
        grid=(batch // nb,),
        in_specs=[row_spec] * 5 + [vec] * 5 + [st_spec],
        out_specs=[row_spec, st_spec],
        out_shape=[jax.ShapeDtypeStruct((rows, d), F32),
                   jax.ShapeDtypeStruct((batch, heads, RWKV_HEAD, RWKV_HEAD), F32)],
        compiler_params=_cparams("parallel"),
        name="wkv_sample",
    )(r, k, v, lw, a, k_k, k_a, r_k, gn_g, gn_b, state)


def _post_kernel(y_ref, g_ref, x_ref, w_ref, lng_ref, lnb_ref, o_ref):
    g = g_ref[...]
    h = (y_ref[...] * (g / (1.0 + jnp.exp(-g)))).astype(BF16)
    z = ALPHA * x_ref[...] + jnp.dot(h, w_ref[...], preferred_element_type=F32)
    mu = jnp.mean(z, axis=-1, keepdims=True)
    zc = z - mu
    var = jnp.mean(zc * zc, axis=-1, keepdims=True)
    o_ref[...] = zc * lax.rsqrt(var + LN_EPS) * lng_ref[...] + lnb_ref[...]


def _post(y, g, x2, w_out, ln_g, ln_b, tm):
    rows, d = x2.shape
    din = y.shape[1]
    in_spec = pl.BlockSpec((tm, din), lambda i: (i, 0))
    row_spec = pl.BlockSpec((tm, d), lambda i: (i, 0))
    return pl.pallas_call(
        _post_kernel,
        grid=(rows // tm,),
        in_specs=[in_spec, in_spec, row_spec, _const_spec((din, d)), _const_spec((1, d)), _const_spec((1, d))],
        out_specs=row_spec,
        out_shape=jax.ShapeDtypeStruct((rows, d), F32),
        compiler_params=_cparams("parallel"),
        name="branch_post",
    )(y, g, x2, w_out.astype(BF16), ln_g.reshape(1, d), ln_b.reshape(1, d))


def _att_pre_kernel(x_ref, w_ref, q_ref, k_ref, v_ref, g_ref, *, hq, hkv):
    z = jnp.dot(x_ref[...].astype(BF16), w_ref[...], preferred_element_type=F32)
    q_ref[...] = z[:, :hq]
    k_ref[...] = z[:, hq:hq + hkv]
    v_ref[...] = z[:, hq + hkv:hq + 2 * hkv]
    g_ref[...] = z[:, hq + 2 * hkv:]


def _att_pre(x2, w_in, tm):
    rows, d = x2.shape
    hq = ATT_KV_HEADS * ATT_GROUP * ATT_HEAD_DIM
    hkv = ATT_KV_HEADS * ATT_HEAD_DIM
    wide = pl.BlockSpec((tm, hq), lambda i: (i, 0))
    narrow = pl.BlockSpec((tm, hkv), lambda i: (i, 0))
    return pl.pallas_call(
        functools.partial(_att_pre_kernel, hq=hq, hkv=hkv),
        grid=(rows // tm,),
        in_specs=[pl.BlockSpec((tm, d), lambda i: (i, 0)), _const_spec(w_in.shape)],
        out_specs=[wide, narrow, narrow, wide],
        out_shape=[jax.ShapeDtypeStruct((rows, hq), F32), jax.ShapeDtypeStruct((rows, hkv), F32),
                   jax.ShapeDtypeStruct((rows, hkv), F32), jax.ShapeDtypeStruct((rows, hq), F32)],
        compiler_params=_cparams("parallel"),
        name="att_pre",
    )(x2, w_in.astype(BF16))


def _alibi_slope(h):
    n_heads = ATT_KV_HEADS * ATT_GROUP
    return 2.0 ** (-8.0 * (h + 1) / n_heads)


def _attend_all(qs, kcats, vcats, slopes, sinks, dist, live):
    idx = range(len(qs))
    s = [lax.dot_general(qs[i], kcats[i], (((1,), (1,)), ((), ())), preferred_element_type=F32) * ATT_SCALE
         for i in idx]
    s = [jnp.where(live, s[i] - slopes[i] * dist, NEG) for i in idx]
    m = [jnp.maximum(jnp.max(s[i], axis=-1, keepdims=True), sinks[i]) for i in idx]
    p = [jnp.exp(s[i] - m[i]) for i in idx]
    den = [jnp.sum(p[i], axis=-1, keepdims=True) + jnp.exp(sinks[i] - m[i]) for i in idx]
    return [jnp.dot(p[i].astype(BF16), vcats[i], preferred_element_type=F32) / den[i] for i in idx]


def _att_prompt_kernel(sink_ref, q_ref, kc_ref, vc_ref, kp_ref, vp_ref, o_ref):
    blk = pl.program_id(1)
    qi = lax.broadcasted_iota(jnp.int32, (WINDOW, 2 * WINDOW), 0)
    kj = lax.broadcasted_iota(jnp.int32, (WINDOW, 2 * WINDOW), 1)
    dist_i = qi + WINDOW - kj
    live = (dist_i >= 0) & (dist_i < WINDOW) & ((kj >= WINDOW) | (blk > 0))
    dist = dist_i.astype(F32)
    hd = ATT_HEAD_DIM
    n_heads = ATT_KV_HEADS * ATT_GROUP
    kcat, vcat = [], []
    for kvh in range(ATT_KV_HEADS):
        ks = slice(kvh * hd, (kvh + 1) * hd)
        kcat.append(jnp.concatenate([kp_ref[:, ks], kc_ref[:, ks]], axis=0).astype(BF16))
        vcat.append(jnp.concatenate([vp_ref[:, ks], vc_ref[:, ks]], axis=0).astype(BF16))
    o = _attend_all([q_ref[:, h * hd:(h + 1) * hd].astype(BF16) for h in range(n_heads)],
                    [kcat[h // ATT_GROUP] for h in range(n_heads)],
                    [vcat[h // ATT_GROUP] for h in range(n_heads)],
                    [_alibi_slope(h) for h in range(n_heads)],
                    [sink_ref[h] for h in range(n_heads)], dist, live)
    for h in range(n_heads):
        o_ref[:, h * hd:(h + 1) * hd] = o[h]


def _att_prompt(q, k, v, sinks, batch, seq):
    rows, hq = q.shape
    hkv = k.shape[1]
    nb = seq // WINDOW
    cur = lambda b, n: (b * nb + n, 0)
    prev = lambda b, n: (b * nb + jnp.maximum(n - 1, 0), 0)
    return pl.pallas_call(
        _att_prompt_kernel,
        grid=(batch, nb),
        in_specs=[pl.BlockSpec(memory_space=pltpu.SMEM),
                  pl.BlockSpec((WINDOW, hq), cur),
                  pl.BlockSpec((WINDOW, hkv), cur), pl.BlockSpec((WINDOW, hkv), cur),
                  pl.BlockSpec((WINDOW, hkv), prev), pl.BlockSpec((WINDOW, hkv), prev)],
        out_specs=pl.BlockSpec((WINDOW, hq), cur),
        out_shape=jax.ShapeDtypeStruct((rows, hq), F32),
        compiler_params=_cparams("parallel", "parallel"),
        name="att_prompt",
    )(sinks, q, k, v, k, v)


def _att_sample_kernel(sink_ref, q_ref, kn_ref, vn_ref, ck_ref, cv_ref, o_ref, *, nb, t):
    m = ATT_GROUP * t
    ri = lax.broadcasted_iota(jnp.int32, (m, 1), 0)
    grp = ri // t
    qi = lax.broadcasted_iota(jnp.int32, (m, 2 * WINDOW), 0) % t
    kj = lax.broadcasted_iota(jnp.int32, (m, 2 * WINDOW), 1)
    dist_i = qi + WINDOW - kj
    live = (dist_i >= 0) & (dist_i < WINDOW)
    dist = dist_i.astype(F32)
    hd = ATT_HEAD_DIM
    pad = jnp.zeros((WINDOW - t, hd), F32)
    qs, kcats, vcats, slopes, sinks = [], [], [], [], []
    for kvh in range(ATT_KV_HEADS):
        ks = slice(kvh * hd, (kvh + 1) * hd)
        slope = jnp.zeros((m, 1), F32)
        sink = jnp.zeros((m, 1), F32)
        for g in range(ATT_GROUP):
            h = kvh * ATT_GROUP + g
            slope = jnp.where(grp == g, _alibi_slope(h), slope)
            sink = jnp.where(grp == g, sink_ref[h], sink)
        for b in range(nb):
            rs = slice(b * t, (b + 1) * t)
            kcats.append(jnp.concatenate([ck_ref[b, :, ks], kn_ref[rs, ks], pad], axis=0).astype(BF16))
            vcats.append(jnp.concatenate([cv_ref[b, :, ks], vn_ref[rs, ks], pad], axis=0).astype(BF16))
            qs.append(jnp.concatenate(
                [q_ref[rs, (kvh * ATT_GROUP + g) * hd:(kvh * ATT_GROUP + g + 1) * hd] for g in range(ATT_GROUP)],
                axis=0).astype(BF16))
            slopes.append(slope)
            sinks.append(sink)
    o = _attend_all(qs, kcats, vcats, slopes, sinks, dist, live)
    for kvh in range(ATT_KV_HEADS):
        for b in range(nb):
            for g in range(ATT_GROUP):
                h = kvh * ATT_GROUP + g
                o_ref[b * t:(b + 1) * t, h * hd:(h + 1) * hd] = o[kvh * nb + b][g * t:(g + 1) * t]


def _att_sample(q, k, v, cache_k, cache_v, sinks, batch, t, nb):
    rows, hq = q.shape
    hkv = k.shape[1]
    row = lambda i: (i, 0)
    cache_spec = pl.BlockSpec((nb, WINDOW, hkv), lambda i: (i, 0, 0))
    return pl.pallas_call(
        functools.partial(_att_sample_kernel, nb=nb, t=t),
        grid=(batch // nb,),
        in_specs=[pl.BlockSpec(memory_space=pltpu.SMEM),
                  pl.BlockSpec((nb * t, hq), row), pl.BlockSpec((nb * t, hkv), row),
                  pl.BlockSpec((nb * t, hkv), row), cache_spec, cache_spec],
        out_specs=pl.BlockSpec((nb * t, hq), row),
        out_shape=jax.ShapeDtypeStruct((rows, hq), F32),
        compiler_params=_cparams("parallel"),
        name="att_sample",
    )(sinks, q, k, v, cache_k, cache_v)


def _row_tile(rows):
    return 256 if rows % 256 == 0 else rows


def kernel(x_prompt, x_sample, state_wkv, state_shift, cache_win_k, cache_win_v, ln_g, ln_b, rwkv_mu, rwkv_w_in, rwkv_w0, rwkv_w1, rwkv_w2, rwkv_a0, rwkv_a1, rwkv_a2, rwkv_k_k, rwkv_k_a, rwkv_r_k, rwkv_gn_g, rwkv_gn_b, rwkv_w_out, att_w_in, att_sinks, att_w_out):
    bp, tp, d = x_prompt.shape
    bs, ts, _ = x_sample.shape
    assert tp % PROMPT_CHUNK == 0 and tp % WINDOW == 0 and WKV_ROWS % ts == 0 and bs % (WKV_ROWS // ts) == 0
    assert cache_win_k.shape[2] == WINDOW
    heads = d // RWKV_HEAD
    hkv = ATT_KV_HEADS * ATT_HEAD_DIM

    j = 0
    vec = lambda t: t.reshape(1, d)
    pre_w = (rwkv_mu[j], rwkv_w_in[j], rwkv_w0[j], rwkv_w1[j], rwkv_w2[j], rwkv_a0[j], rwkv_a1[j], rwkv_a2[j])
    head_w = (vec(rwkv_k_k[j]), vec(rwkv_k_a[j]), vec(rwkv_r_k[j]), vec(rwkv_gn_g[j]), vec(rwkv_gn_b[j]))

    xp2 = x_prompt.reshape(bp * tp, d)
    xs2 = x_sample.reshape(bs * ts, d)

    r, k, v, g, lw, a = _rwkv_pre(x_prompt, jnp.zeros((bp, d), F32), *pre_w, tm=_row_tile(bp * tp))
    yp, wkv_p = _wkv_prompt(r, k, v, lw, a, *head_w, batch=bp, seq=tp)
    xp1 = _post(yp, g, xp2, rwkv_w_out[j], ln_g[0], ln_b[0], tm=_row_tile(bp * tp))

    r, k, v, g, lw, a = _rwkv_pre(x_sample, state_shift[j], *pre_w, tm=_row_tile(bs * ts))
    ys, wkv_s = _wkv_sample(r, k, v, lw, a, *head_w, state_wkv[j], batch=bs, seq=ts)
    xs1 = _post(ys, g, xs2, rwkv_w_out[j], ln_g[0], ln_b[0], tm=_row_tile(bs * ts))

    q, k, v, gate = _att_pre(xp1, att_w_in[j], tm=_row_tile(bp * tp))
    o = _att_prompt(q, k, v, att_sinks[j], batch=bp, seq=tp)
    y_prompt = _post(o, gate, xp1, att_w_out[j], ln_g[1], ln_b[1], tm=_row_tile(bp * tp)).reshape(bp, tp, d)
    win_shape = (bp, WINDOW, ATT_KV_HEADS, ATT_HEAD_DIM)
    win_k_p = k.reshape(bp, tp, hkv)[:, tp - WINDOW:].reshape(win_shape)
    win_v_p = v.reshape(bp, tp, hkv)[:, tp - WINDOW:].reshape(win_shape)

    q, k, v, gate = _att_pre(xs1, att_w_in[j], tm=_row_tile(bs * ts))
    ck = cache_win_k[j].reshape(bs, WINDOW, hkv)
    cv = cache_win_v[j].reshape(bs, WINDOW, hkv)
    o = _att_sample(q, k, v, ck, cv, att_sinks[j], batch=bs, t=ts, nb=WKV_ROWS // ts)
    y_sample = _post(o, gate, xs1, att_w_out[j], ln_g[1], ln_b[1], tm=_row_tile(bs * ts)).reshape(bs, ts, d)
    win_k_s = jnp.concatenate([ck[:, ts:], k.reshape(bs, ts, hkv)], axis=1).reshape(bs, WINDOW, ATT_KV_HEADS, ATT_HEAD_DIM)
    win_v_s = jnp.concatenate([cv[:, ts:], v.reshape(bs, ts, hkv)], axis=1).reshape(bs, WINDOW, ATT_KV_HEADS, ATT_HEAD_DIM)

    return (y_prompt, y_sample,
            wkv_p[None], x_prompt[:, -1][None], win_k_p[None], win_v_p[None],
            wkv_s[None], x_sample[:, -1][None], win_k_s[None], win_v_s[None])
```

```python
import functools

import jax
import jax.numpy as jnp
from jax import lax
from jax.experimental import pallas as pl
from jax.experimental.pallas import tpu as pltpu

F32 = jnp.float32
BF16 = jnp.bfloat16

RWKV_HEAD = 64
ATT_HEAD_DIM = 64
ATT_KV_HEADS = 4
ATT_GROUP = 4
WINDOW = 128
ATT_SCALE = ATT_HEAD_DIM ** -0.5
NEG = -1e30
DEPTH = 2
ALPHA = (2 * DEPTH) ** 0.25
LN_EPS = 1e-5
GN_EPS = 64e-5

V7X_LANES = 128
V7X_SUBLANES = 8
V7X_VMEM_LIMIT_BYTES = 56 * 1024 * 1024

HEADS_PER_VREG = V7X_LANES // RWKV_HEAD
STACK_ROWS = 128
WKV_ROWS = STACK_ROWS // HEADS_PER_VREG
PROMPT_CHUNK = WKV_ROWS


def _cparams(*sem):
    return pltpu.CompilerParams(dimension_semantics=sem, vmem_limit_bytes=V7X_VMEM_LIMIT_BYTES)


def _const_spec(shape):
    nd = len(shape)
    return pl.BlockSpec(shape, lambda *_: (0,) * nd, pipeline_mode=pl.Buffered(1))


def _rwkv_pre_kernel(x_ref, first_ref, halo_ref, mu_ref, win_ref, w0_ref, w1_ref, w2_ref, a0_ref, a1_ref, a2_ref,
                     r_ref, k_ref, v_ref, g_ref, lw_ref, a_ref, *, tiles_per_seq):
    x3 = x_ref[...]
    nseq, rows, d = x3.shape
    first = first_ref[...]
    if tiles_per_seq > 1:
        first = jnp.where(pl.program_id(0) % tiles_per_seq == 0, first, halo_ref[:, -1:, :])
    t_idx = lax.broadcasted_iota(jnp.int32, (1, rows, 1), 1)
    x_prev = jnp.where(t_idx == 0, first, pltpu.roll(x3, 1, axis=1))
    x = x3.reshape(nseq * rows, d)
    xx = x_prev.reshape(nseq * rows, d) - x

    def mix(p):
        return (x + xx * mu_ref[p:p + 1, :]).astype(BF16)

    def proj(p):
        return jnp.dot(mix(p), win_ref[p], preferred_element_type=F32)

    r_ref[...] = proj(0)
    k_ref[...] = proj(1)
    v_ref[...] = proj(2)
    g_ref[...] = proj(3)
    hw = jnp.tanh(jnp.dot(mix(4), w1_ref[...], preferred_element_type=F32))
    wl = w0_ref[...] + jnp.dot(hw.astype(BF16), w2_ref[...], preferred_element_type=F32)
    z = -wl
    softplus = jnp.maximum(z, 0.0) + jnp.log(1.0 + jnp.exp(-jnp.abs(z)))
    lw_ref[...] = -jnp.exp(-softplus - 0.5)
    ha = jnp.dot(mix(5), a1_ref[...], preferred_element_type=F32)
    al = a0_ref[...] + jnp.dot(ha.astype(BF16), a2_ref[...], preferred_element_type=F32)
    a_ref[...] = 1.0 / (1.0 + jnp.exp(-al))


def _rwkv_pre(x3, first, mu, w_in, w0, w1, w2, a0, a1, a2, tm):
    nseq, seq, d = x3.shape
    rows = nseq * seq
    lora = w1.shape[1]
    if seq >= tm:
        tps = seq // tm
        x_spec = pl.BlockSpec((1, tm, d), lambda i: (i // tps, i % tps, 0))
        first_spec = pl.BlockSpec((1, 1, d), lambda i: (i // tps, 0, 0))
        halo_spec = pl.BlockSpec(
            (1, V7X_SUBLANES, d),
            lambda i: (i // tps, jnp.maximum((i % tps) * (tm // V7X_SUBLANES) - 1, 0), 0))
    else:
        tps = 1
        x_spec = pl.BlockSpec((tm // seq, seq, d), lambda i: (i, 0, 0))
        first_spec = pl.BlockSpec((tm // seq, 1, d), lambda i: (i, 0, 0))
        halo_spec = pl.BlockSpec((1, V7X_SUBLANES, d), lambda i: (0, 0, 0))
    row_spec = pl.BlockSpec((tm, d), lambda i: (i, 0))
    out = jax.ShapeDtypeStruct((rows, d), F32)
    return pl.pallas_call(
        functools.partial(_rwkv_pre_kernel, tiles_per_seq=tps),
        grid=(rows // tm,),
        in_specs=[x_spec, first_spec, halo_spec, _const_spec((6, d)), _const_spec((4, d, d)),
                  _const_spec((1, d)), _const_spec((d, lora)), _const_spec((lora, d)), _const_spec((1, d)),
                  _const_spec((d, lora)), _const_spec((lora, d))],
        out_specs=[row_spec] * 6,
        out_shape=[out] * 6,
        compiler_params=_cparams("parallel"),
        name="rwkv_pre",
    )(x3, first.reshape(nseq, 1, d), x3, mu, w_in.astype(BF16), w0.reshape(1, d), w1.astype(BF16),
      w2.astype(BF16), a0.reshape(1, d), a1.astype(BF16), a2.astype(BF16))


def _mm(a, b, dims):
    return lax.dot_general(a.astype(BF16), b.astype(BF16), (dims, ((), ())), preferred_element_type=F32)


def _mm_nn(a, b):
    return _mm(a, b, ((1,), (0,)))


def _mm_nt(a, b):
    return _mm(a, b, ((1,), (1,)))


def _mm_tn(a, b):
    return _mm(a, b, ((0,), (0,)))


def _wkv_chunk(r, k, v, lw, a, k_k, k_a, r_k, gn_g, gn_b, states, nb, c):
    pairs = range(len(r))
    rows = nb * c
    n = STACK_ROWS
    hb = 2 * c
    lane = lax.broadcasted_iota(jnp.int32, (1, V7X_LANES), 1)
    head0 = lane < RWKV_HEAD
    mk = (jnp.where(head0, 1.0, 0.0).astype(F32), jnp.where(head0, 0.0, 1.0).astype(F32))

    def head_sum(t):
        s0 = jnp.sum(t * mk[0], axis=-1, keepdims=True)
        s1 = jnp.sum(t * mk[1], axis=-1, keepdims=True)
        return jnp.where(head0, s0, s1)

    def stack(t):
        parts = []
        for b in range(nb):
            tb = t[b * c:(b + 1) * c]
            parts += [tb * mk[0], tb * mk[1]]
        return jnp.concatenate(parts, axis=0).astype(BF16)

    ri = lax.broadcasted_iota(jnp.int32, (rows, rows), 0)
    ci = lax.broadcasted_iota(jnp.int32, (rows, rows), 1)
    tri = jnp.where((ri // c == ci // c) & (ri >= ci), 1.0, 0.0).astype(BF16)
    lw_hi = [lw[p].astype(BF16) for p in pairs]
    cum = [_mm_nn(tri, lw_hi[p]) + _mm_nn(tri, lw[p] - lw_hi[p].astype(F32)) for p in pairs]

    kkr = [k[p] * k_k[p] for p in pairs]
    kk = [kkr[p] / jnp.maximum(jnp.sqrt(head_sum(kkr[p] * kkr[p])), 1e-12) for p in pairs]
    kf = [k[p] * (1.0 + (a[p] - 1.0) * k_a[p]) for p in pairs]
    g_incl = [jnp.exp(cum[p]) for p in pairs]
    g_inv = [jnp.exp(-cum[p]) for p in pairs]
    at_s = [stack(-kk[p] * jnp.exp(cum[p] - lw[p])) for p in pairs]
    rt_s = [stack(r[p] * g_incl[p]) for p in pairs]
    bt_s = [stack(kk[p] * a[p] * g_inv[p]) for p in pairs]
    kt_s = [stack(kf[p] * g_inv[p]) for p in pairs]
    v_s = [stack(v[p]) for p in pairs]
    xa = [jnp.concatenate([at_s[p], rt_s[p]], axis=0) for p in pairs]
    yb = [jnp.concatenate([bt_s[p], kt_s[p]], axis=0) for p in pairs]

    si = lax.broadcasted_iota(jnp.int32, (n, n), 0)
    sj = lax.broadcasted_iota(jnp.int32, (n, n), 1)
    same = si // c == sj // c
    strict = same & (si > sj)
    incl = same & (si >= sj)
    eye = jnp.where(si == sj, 1.0, 0.0).astype(F32)

    gm = [_mm_nt(xa[p], yb[p]) for p in pairs]
    a_ab = [jnp.where(strict, gm[p][:n, :n], 0.0) for p in pairs]
    a_kr = [jnp.concatenate([jnp.where(strict, gm[p][:n, n:], 0.0),
                             jnp.where(incl, gm[p][n:, n:], 0.0)], axis=0).astype(BF16) for p in pairs]
    a_rb = [jnp.where(incl, gm[p][n:, :n], 0.0).astype(BF16) for p in pairs]

    tinv = [eye + a_ab[p] for p in pairs]
    pw = [a_ab[p].astype(BF16) for p in pairs]
    if c > 2:
        pw = [_mm_nn(pw[p], pw[p]).astype(BF16) for p in pairs]
        span = 4
        while span < c:
            sq = [_mm_nn(pw[p], jnp.concatenate([pw[p], tinv[p].astype(BF16)], axis=1)) for p in pairs]
            tinv = [tinv[p] + sq[p][:, n:] for p in pairs]
            pw = [sq[p][:, :n].astype(BF16) for p in pairs]
            span *= 2
        tinv = [tinv[p] + _mm_nn(pw[p], tinv[p]) for p in pairs]

    av = [_mm_nn(a_kr[p], v_s[p]) for p in pairs]
    if nb == 1:
        ah = [_mm_nt(xa[p], states[p][0]) for p in pairs]
        ah_a = [ah[p][:n] for p in pairs]
        ah_r = [ah[p][n:] for p in pairs]
    else:
        ah_a, ah_r = [], []
        for p in pairs:
            parts = []
            for b in range(nb):
                sl = slice(b * hb, (b + 1) * hb)
                parts.append(_mm_nt(jnp.concatenate([at_s[p][sl], rt_s[p][sl]], axis=0), states[p][b]))
            ah_a.append(jnp.concatenate([t[:hb] for t in parts], axis=0))
            ah_r.append(jnp.concatenate([t[hb:] for t in parts], axis=0))

    u = [_mm_nn(tinv[p], ah_a[p] + av[p][:n]).astype(BF16) for p in pairs]
    y_s = [ah_r[p] + av[p][n:] + _mm_nn(a_rb[p], u[p]) for p in pairs]

    new_states = []
    for p in pairs:
        per_batch = []
        for b in range(nb):
            sl = slice(b * hb, (b + 1) * hb)
            upd = _mm_tn(jnp.concatenate([u[p][sl], v_s[p][sl]], axis=0),
                         jnp.concatenate([bt_s[p][sl], kt_s[p][sl]], axis=0))
            g_last = g_incl[p][(b + 1) * c - 1:(b + 1) * c]
            per_batch.append((states[p][b] + upd) * g_last)
        new_states.append(per_batch)

    inv_n = 1.0 / RWKV_HEAD
    out = []
    for p in pairs:
        y = jnp.concatenate([y_s[p][b * hb:b * hb + c] + y_s[p][b * hb + c:(b + 1) * hb] for b in range(nb)],
                            axis=0)
        mean = head_sum(y) * inv_n
        yc = y - mean
        var = head_sum(yc * yc) * inv_n
        yn = yc * lax.rsqrt(var + GN_EPS) * gn_g[p] + gn_b[p]
        out.append(yn + head_sum(r[p] * kf[p] * r_k[p]) * v[p])
    return out, new_states


def _pair_lanes(p):
    return slice(p * V7X_LANES, (p + 1) * V7X_LANES)


def _state_to_blockdiag(s0, s1):
    z = jnp.zeros_like(s0)
    return jnp.concatenate([jnp.concatenate([s0, z], axis=1), jnp.concatenate([z, s1], axis=1)], axis=0)


def _pair_views(n_pairs, *refs):
    return [[ref[:, _pair_lanes(p)] for p in range(n_pairs)] for ref in refs]


def _wkv_prompt_kernel(r_ref, k_ref, v_ref, lw_ref, a_ref, kk_ref, ka_ref, rk_ref, gg_ref, gb_ref,
                       y_ref, sfin_ref, s_scr, *, n_batch, n_pairs, c):
    ci = pl.program_id(0)

    @pl.when(ci == 0)
    def _():
        s_scr[...] = jnp.zeros_like(s_scr)

    chains = [(b, p) for b in range(n_batch) for p in range(n_pairs)]
    data = [[ref[b, :, _pair_lanes(p)] for b, p in chains] for ref in (r_ref, k_ref, v_ref, lw_ref, a_ref)]
    prm = [[ref[:, _pair_lanes(p)] for _, p in chains] for ref in (kk_ref, ka_ref, rk_ref, gg_ref, gb_ref)]
    yn, new = _wkv_chunk(*data, *prm, [[s_scr[i]] for i in range(len(chains))], 1, c)
    for i, (b, p) in enumerate(chains):
        y_ref[b, :, _pair_lanes(p)] = yn[i]
        s_scr[i] = new[i][0]

    @pl.when(ci == pl.num_programs(0) - 1)
    def _():
        for i, (b, p) in enumerate(chains):
            s = s_scr[i]
            sfin_ref[b, 2 * p] = s[:RWKV_HEAD, :RWKV_HEAD]
            sfin_ref[b, 2 * p + 1] = s[RWKV_HEAD:, RWKV_HEAD:]


def _wkv_sample_kernel(r_ref, k_ref, v_ref, lw_ref, a_ref, kk_ref, ka_ref, rk_ref, gg_ref, gb_ref,
                       s0_ref, y_ref, sfin_ref, *, n_pairs, nb, c):
    args = _pair_views(n_pairs, r_ref, k_ref, v_ref, lw_ref, a_ref, kk_ref, ka_ref, rk_ref, gg_ref, gb_ref)
    states = [[_state_to_blockdiag(s0_ref[b, 2 * p], s0_ref[b, 2 * p + 1]) for b in range(nb)]
              for p in range(n_pairs)]
    yn, new = _wkv_chunk(*args, states, nb, c)
    for p in range(n_pairs):
        y_ref[:, _pair_lanes(p)] = yn[p]
        for b in range(nb):
            sfin_ref[b, 2 * p] = new[p][b][:RWKV_HEAD, :RWKV_HEAD]
            sfin_ref[b, 2 * p + 1] = new[p][b][RWKV_HEAD:, RWKV_HEAD:]


def _wkv_prompt(r, k, v, lw, a, k_k, k_a, r_k, gn_g, gn_b, batch, seq):
    rows, d = r.shape
    heads = d // RWKV_HEAD
    n_pairs = heads // HEADS_PER_VREG
    c = PROMPT_CHUNK
    row_spec = pl.BlockSpec((batch, c, d), lambda i: (0, i, 0))
    vec = _const_spec((1, d))
    kern = functools.partial(_wkv_prompt_kernel, n_batch=batch, n_pairs=n_pairs, c=c)
    seq3 = lambda t: t.reshape(batch, seq, d)
    y, s_fin = pl.pallas_call(
        kern,
        grid=(seq // c,),
        in_specs=[row_spec] * 5 + [vec] * 5,
        out_specs=[row_spec, pl.BlockSpec((batch, heads, RWKV_HEAD, RWKV_HEAD), lambda i: (0, 0, 0, 0))],
        out_shape=[jax.ShapeDtypeStruct((batch, seq, d), F32),
                   jax.ShapeDtypeStruct((batch, heads, RWKV_HEAD, RWKV_HEAD), F32)],
        scratch_shapes=[pltpu.VMEM((batch * n_pairs, V7X_LANES, V7X_LANES), F32)],
        compiler_params=_cparams("arbitrary"),
        name="wkv_prompt",
    )(seq3(r), seq3(k), seq3(v), seq3(lw), seq3(a), k_k, k_a, r_k, gn_g, gn_b)
    return y.reshape(rows, d), s_fin


def _wkv_sample(r, k, v, lw, a, k_k, k_a, r_k, gn_g, gn_b, state, batch, seq):
    rows, d = r.shape
    heads = d // RWKV_HEAD
    nb = WKV_ROWS // seq
    row_spec = pl.BlockSpec((WKV_ROWS, d), lambda i: (i, 0))
    st_spec = pl.BlockSpec((nb, heads, RWKV_HEAD, RWKV_HEAD), lambda i: (i, 0, 0, 0))
    vec = _const_spec((1, d))
    kern = functools.partial(_wkv_sample_kernel, n_pairs=heads // HEADS_PER_VREG, nb=nb, c=seq)
    return pl.pallas_call(
        kern,
        grid=(batch // nb,),
        in_specs=[row_spec] * 5 + [vec] * 5 + [st_spec],
        out_specs=[row_spec, st_spec],
        out_shape=[jax.ShapeDtypeStruct((rows, d), F32),
                   jax.ShapeDtypeStruct((batch, heads, RWKV_HEAD, RWKV_HEAD), F32)],
        compiler_params=_cparams("parallel"),
        name="wkv_sample",
    )(r, k, v, lw, a, k_k, k_a, r_k, gn_g, gn_b, state)


def _post_kernel(y_ref, g_ref, x_ref, w_ref, lng_ref, lnb_ref, o_ref):
    g = g_ref[...]
    h = (y_ref[...] * (g / (1.0 + jnp.exp(-g)))).astype(BF16)
    z = ALPHA * x_ref[...] + jnp.dot(h, w_ref[...], preferred_element_type=F32)
    mu = jnp.mean(z, axis=-1, keepdims=True)
    zc = z - mu
    var = jnp.mean(zc * zc, axis=-1, keepdims=True)
    o_ref[...] = zc * lax.rsqrt(var + LN_EPS) * lng_ref[...] + lnb_ref[...]


def _post(y, g, x2, w_out, ln_g, ln_b, tm):
    rows, d = x2.shape
    din = y.shape[1]
    in_spec = pl.BlockSpec((tm, din), lambda i: (i, 0))
    row_spec = pl.BlockSpec((tm, d), lambda i: (i, 0))
    return pl.pallas_call(
        _post_kernel,
        grid=(rows // tm,),
        in_specs=[in_spec, in_spec, row_spec, _const_spec((din, d)), _const_spec((1, d)), _const_spec((1, d))],
        out_specs=row_spec,
        out_shape=jax.ShapeDtypeStruct((rows, d), F32),
        compiler_params=_cparams("parallel"),
        name="branch_post",
    )(y, g, x2, w_out.astype(BF16), ln_g.reshape(1, d), ln_b.reshape(1, d))


def _att_pre_kernel(x_ref, w_ref, q_ref, k_ref, v_ref, g_ref, *, hq, hkv):
    z = jnp.dot(x_ref[...].astype(BF16), w_ref[...], preferred_element_type=F32)
    q_ref[...] = z[:, :hq]
    k_ref[...] = z[:, hq:hq + hkv]
    v_ref[...] = z[:, hq + hkv:hq + 2 * hkv]
    g_ref[...] = z[:, hq + 2 * hkv:]


def _att_pre(x2, w_in, tm):
    rows, d = x2.shape
    hq = ATT_KV_HEADS * ATT_GROUP * ATT_HEAD_DIM
    hkv = ATT_KV_HEADS * ATT_HEAD_DIM
    wide = pl.BlockSpec((tm, hq), lambda i: (i, 0))
    narrow = pl.BlockSpec((tm, hkv), lambda i: (i, 0))
    return pl.pallas_call(
        functools.partial(_att_pre_kernel, hq=hq, hkv=hkv),
        grid=(rows // tm,),
        in_specs=[pl.BlockSpec((tm, d), lambda i: (i, 0)), _const_spec(w_in.shape)],
        out_specs=[wide, narrow, narrow, wide],
        out_shape=[jax.ShapeDtypeStruct((rows, hq), F32), jax.ShapeDtypeStruct((rows, hkv), F32),
                   jax.ShapeDtypeStruct((rows, hkv), F32), jax.ShapeDtypeStruct((rows, hq), F32)],
        compiler_params=_cparams("parallel"),
        name="att_pre",
    )(x2, w_in.astype(BF16))


def _alibi_slope(h):
    n_heads = ATT_KV_HEADS * ATT_GROUP
    return 2.0 ** (-8.0 * (h + 1) / n_heads)


def _attend_all(qs, kcats, vcats, slopes, sinks, dist, live):
    idx = range(len(qs))
    s = [lax.dot_general(qs[i], kcats[i], (((1,), (1,)), ((), ())), preferred_element_type=F32) * ATT_SCALE
         for i in idx]
    s = [jnp.where(live, s[i] - slopes[i] * dist, NEG) for i in idx]
    m = [jnp.maximum(jnp.max(s[i], axis=-1, keepdims=True), sinks[i]) for i in idx]
    p = [jnp.exp(s[i] - m[i]) for i in idx]
    den = [jnp.sum(p[i], axis=-1, keepdims=True) + jnp.exp(sinks[i] - m[i]) for i in idx]
    return [jnp.dot(p[i].astype(BF16), vcats[i], preferred_element_type=F32) / den[i] for i in idx]


def _att_prompt_kernel(sink_ref, q_ref, kc_ref, vc_ref, kp_ref, vp_ref, o_ref):
    blk = pl.program_id(1)
    qi = lax.broadcasted_iota(jnp.int32, (WINDOW, 2 * WINDOW), 0)
    kj = lax.broadcasted_iota(jnp.int32, (WINDOW, 2 * WINDOW), 1)
    dist_i = qi + WINDOW - kj
    live = (dist_i >= 0) & (dist_i < WINDOW) & ((kj >= WINDOW) | (blk > 0))
    dist = dist_i.astype(F32)
    hd = ATT_HEAD_DIM
    n_heads = ATT_KV_HEADS * ATT_GROUP
    kcat, vcat = [], []
    for kvh in range(ATT_KV_HEADS):
        ks = slice(kvh * hd, (kvh + 1) * hd)
        kcat.append(jnp.concatenate([kp_ref[:, ks], kc_ref[:, ks]], axis=0).astype(BF16))
        vcat.append(jnp.concatenate([vp_ref[:, ks], vc_ref[:, ks]], axis=0).astype(BF16))
    o = _attend_all([q_ref[:, h * hd:(h + 1) * hd].astype(BF16) for h in range(n_heads)],
                    [kcat[h // ATT_GROUP] for h in range(n_heads)],
                    [vcat[h // ATT_GROUP] for h in range(n_heads)],
                    [_alibi_slope(h) for h in range(n_heads)],
                    [sink_ref[h] for h in range(n_heads)], dist, live)
    for h in range(n_heads):
        o_ref[:, h * hd:(h + 1) * hd] = o[h]


def _att_prompt(q, k, v, sinks, batch, seq):
    rows, hq = q.shape
    hkv = k.shape[1]
    nb = seq // WINDOW
    cur = lambda b, n: (b * nb + n, 0)
    prev = lambda b, n: (b * nb + jnp.maximum(n - 1, 0), 0)
    return pl.pallas_call(
        _att_prompt_kernel,
        grid=(batch, nb),
        in_specs=[pl.BlockSpec(memory_space=pltpu.SMEM),
                  pl.BlockSpec((WINDOW, hq), cur),
                  pl.BlockSpec((WINDOW, hkv), cur), pl.BlockSpec((WINDOW, hkv), cur),
                  pl.BlockSpec((WINDOW, hkv), prev), pl.BlockSpec((WINDOW, hkv), prev)],
        out_specs=pl.BlockSpec((WINDOW, hq), cur),
        out_shape=jax.ShapeDtypeStruct((rows, hq), F32),
        compiler_params=_cparams("parallel", "parallel"),
        name="att_prompt",
    )(sinks, q, k, v, k, v)


def _att_sample_kernel(sink_ref, q_ref, kn_ref, vn_ref, ck_ref, cv_ref, o_ref, *, nb, t):
    m = ATT_GROUP * t
    ri = lax.broadcasted_iota(jnp.int32, (m, 1), 0)
    grp = ri // t
    qi = lax.broadcasted_iota(jnp.int32, (m, 2 * WINDOW), 0) % t
    kj = lax.broadcasted_iota(jnp.int32, (m, 2 * WINDOW), 1)
    dist_i = qi + WINDOW - kj
    live = (dist_i >= 0) & (dist_i < WINDOW)
    dist = dist_i.astype(F32)
    hd = ATT_HEAD_DIM
    pad = jnp.zeros((WINDOW - t, hd), F32)
    qs, kcats, vcats, slopes, sinks = [], [], [], [], []
    for kvh in range(ATT_KV_HEADS):
        ks = slice(kvh * hd, (kvh + 1) * hd)
        slope = jnp.zeros((m, 1), F32)
        sink = jnp.zeros((m, 1), F32)
        for g in range(ATT_GROUP):
            h = kvh * ATT_GROUP + g
            slope = jnp.where(grp == g, _alibi_slope(h), slope)
            sink = jnp.where(grp == g, sink_ref[h], sink)
        for b in range(nb):
            rs = slice(b * t, (b + 1) * t)
            kcats.append(jnp.concatenate([ck_ref[b, :, ks], kn_ref[rs, ks], pad], axis=0).astype(BF16))
            vcats.append(jnp.concatenate([cv_ref[b, :, ks], vn_ref[rs, ks], pad], axis=0).astype(BF16))
            qs.append(jnp.concatenate(
                [q_ref[rs, (kvh * ATT_GROUP + g) * hd:(kvh * ATT_GROUP + g + 1) * hd] for g in range(ATT_GROUP)],
                axis=0).astype(BF16))
            slopes.append(slope)
            sinks.append(sink)
    o = _attend_all(qs, kcats, vcats, slopes, sinks, dist, live)
    for kvh in range(ATT_KV_HEADS):
        for b in range(nb):
            for g in range(ATT_GROUP):
                h = kvh * ATT_GROUP + g
                o_ref[b * t:(b + 1) * t, h * hd:(h + 1) * hd] = o[kvh * nb + b][g * t:(g + 1) * t]


def _att_sample(q, k, v, cache_k, cache_v, sinks, batch, t, nb):
    rows, hq = q.shape
    hkv = k.shape[1]
    row = lambda i: (i, 0)
    cache_spec = pl.BlockSpec((nb, WINDOW, hkv), lambda i: (i, 0, 0))
    return pl.pallas_call(
        functools.partial(_att_sample_kernel, nb=nb, t=t),
        grid=(batch // nb,),
        in_specs=[pl.BlockSpec(memory_space=pltpu.SMEM),
                  pl.BlockSpec((nb * t, hq), row), pl.BlockSpec((nb * t, hkv), row),
                  pl.BlockSpec((nb * t, hkv), row), cache_spec, cache_spec],
        out_specs=pl.BlockSpec((nb * t, hq), row),
        out_shape=jax.ShapeDtypeStruct((rows, hq), F32),
        compiler_params=_cparams("parallel"),
        name="att_sample",
    )(sinks, q, k, v, cache_k, cache_v)


ROW_TILE = 512


def _row_tile(rows):
    return ROW_TILE if rows % ROW_TILE == 0 else rows


def kernel(x_prompt, x_sample, state_wkv, state_shift, cache_win_k, cache_win_v, ln_g, ln_b, rwkv_mu, rwkv_w_in, rwkv_w0, rwkv_w1, rwkv_w2, rwkv_a0, rwkv_a1, rwkv_a2, rwkv_k_k, rwkv_k_a, rwkv_r_k, rwkv_gn_g, rwkv_gn_b, rwkv_w_out, att_w_in, att_sinks, att_w_out):
    bp, tp, d = x_prompt.shape
    bs, ts, _ = x_sample.shape
    assert tp % PROMPT_CHUNK == 0 and tp % WINDOW == 0 and WKV_ROWS % ts == 0 and bs % (WKV_ROWS // ts) == 0
    assert cache_win_k.shape[2] == WINDOW
    heads = d // RWKV_HEAD
    hkv = ATT_KV_HEADS * ATT_HEAD_DIM

    j = 0
    vec = lambda t: t.reshape(1, d)
    pre_w = (rwkv_mu[j], rwkv_w_in[j], rwkv_w0[j], rwkv_w1[j], rwkv_w2[j], rwkv_a0[j], rwkv_a1[j], rwkv_a2[j])
    head_w = (vec(rwkv_k_k[j]), vec(rwkv_k_a[j]), vec(rwkv_r_k[j]), vec(rwkv_gn_g[j]), vec(rwkv_gn_b[j]))

    xp2 = x_prompt.reshape(bp * tp, d)
    xs2 = x_sample.reshape(bs * ts, d)

    r, k, v, g, lw, a = _rwkv_pre(x_prompt, jnp.zeros((bp, d), F32), *pre_w, tm=_row_tile(bp * tp))
    yp, wkv_p = _wkv_prompt(r, k, v, lw, a, *head_w, batch=bp, seq=tp)
    xp1 = _post(yp, g, xp2, rwkv_w_out[j], ln_g[0], ln_b[0], tm=_row_tile(bp * tp))

    r, k, v, g, lw, a = _rwkv_pre(x_sample, state_shift[j], *pre_w, tm=_row_tile(bs * ts))
    ys, wkv_s = _wkv_sample(r, k, v, lw, a, *head_w, state_wkv[j], batch=bs, seq=ts)
    xs1 = _post(ys, g, xs2, rwkv_w_out[j], ln_g[0], ln_b[0], tm=_row_tile(bs * ts))

    q, k, v, gate = _att_pre(xp1, att_w_in[j], tm=_row_tile(bp * tp))
    o = _att_prompt(q, k, v, att_sinks[j], batch=bp, seq=tp)
    y_prompt = _post(o, gate, xp1, att_w_out[j], ln_g[1], ln_b[1], tm=_row_tile(bp * tp)).reshape(bp, tp, d)
    win_shape = (bp, WINDOW, ATT_KV_HEADS, ATT_HEAD_DIM)
    win_k_p = k.reshape(bp, tp, hkv)[:, tp - WINDOW:].reshape(win_shape)
    win_v_p = v.reshape(bp, tp, hkv)[:, tp - WINDOW:].reshape(win_shape)

    q, k, v, gate = _att_pre(xs1, att_w_in[j], tm=_row_tile(bs * ts))
    ck = cache_win_k[j].reshape(bs, WINDOW, hkv)
    cv = cache_win_v[j].reshape(bs, WINDOW, hkv)
    o = _att_sample(q, k, v, ck, cv, att_sinks[j], batch=bs, t=ts, nb=WKV_ROWS // ts)
    y_sample = _post(o, gate, xs1, att_w_out[j], ln_g[1], ln_b[1], tm=_row_tile(bs * ts)).reshape(bs, ts, d)
    win_k_s = jnp.concatenate([ck[:, ts:], k.reshape(bs, ts, hkv)], axis=1).reshape(bs, WINDOW, ATT_KV_HEADS, ATT_HEAD_DIM)
    win_v_s = jnp.concatenate([cv[:, ts:], v.reshape(bs, ts, hkv)], axis=1).reshape(bs, WINDOW, ATT_KV_HEADS, ATT_HEAD_DIM)

    return (y_prompt, y_sample,
            wkv_p[None], x_prompt[:, -1][None], win_k_p[None], win_v_p[None],
            wkv_s[None], x_sample[:, -1][None], win_k_s[None], win_v_s[None])
```

```python
import functools

import jax
import jax.numpy as jnp
from jax import lax
from jax.experimental import pallas as pl
from jax.experimental.pallas import tpu as pltpu

F32 = jnp.float32
BF16 = jnp.bfloat16

RWKV_HEAD = 64
ATT_HEAD_DIM = 64
ATT_KV_HEADS = 4
ATT_GROUP = 4
WINDOW = 128
ATT_SCALE = ATT_HEAD_DIM ** -0.5
NEG = -1e30
DEPTH = 2
ALPHA = (2 * DEPTH) ** 0.25
LN_EPS = 1e-5
GN_EPS = 64e-5

V7X_LANES = 128
V7X_SUBLANES = 8
V7X_VMEM_LIMIT_BYTES = 56 * 1024 * 1024

HEADS_PER_VREG = V7X_LANES // RWKV_HEAD
STACK_ROWS = 128
WKV_ROWS = STACK_ROWS // HEADS_PER_VREG
PROMPT_CHUNK = WKV_ROWS


def _cparams(*sem):
    return pltpu.CompilerParams(dimension_semantics=sem, vmem_limit_bytes=V7X_VMEM_LIMIT_BYTES)


def _const_spec(shape):
    nd = len(shape)
    return pl.BlockSpec(shape, lambda *_: (0,) * nd, pipeline_mode=pl.Buffered(1))


def _rwkv_pre_kernel(x_ref, first_ref, halo_ref, mu_ref, win_ref, w0_ref, w1_ref, w2_ref, a0_ref, a1_ref, a2_ref,
                     r_ref, k_ref, v_ref, g_ref, lw_ref, a_ref, *, tiles_per_seq):
    x3 = x_ref[...]
    nseq, rows, d = x3.shape
    first = first_ref[...]
    if tiles_per_seq > 1:
        first = jnp.where(pl.program_id(0) % tiles_per_seq == 0, first, halo_ref[:, -1:, :])
    t_idx = lax.broadcasted_iota(jnp.int32, (1, rows, 1), 1)
    x_prev = jnp.where(t_idx == 0, first, pltpu.roll(x3, 1, axis=1))
    x = x3.reshape(nseq * rows, d)
    xx = x_prev.reshape(nseq * rows, d) - x

    def mix(p):
        return (x + xx * mu_ref[p:p + 1, :]).astype(BF16)

    def proj(p):
        return jnp.dot(mix(p), win_ref[p], preferred_element_type=F32)

    r_ref[...] = proj(0)
    k_ref[...] = proj(1)
    v_ref[...] = proj(2)
    g_ref[...] = proj(3)
    hw = jnp.tanh(jnp.dot(mix(4), w1_ref[...], preferred_element_type=F32))
    wl = w0_ref[...] + jnp.dot(hw.astype(BF16), w2_ref[...], preferred_element_type=F32)
    z = -wl
    softplus = jnp.maximum(z, 0.0) + jnp.log(1.0 + jnp.exp(-jnp.abs(z)))
    lw_ref[...] = -jnp.exp(-softplus - 0.5)
    ha = jnp.dot(mix(5), a1_ref[...], preferred_element_type=F32)
    al = a0_ref[...] + jnp.dot(ha.astype(BF16), a2_ref[...], preferred_element_type=F32)
    a_ref[...] = 1.0 / (1.0 + jnp.exp(-al))


def _rwkv_pre(x3, first, mu, w_in, w0, w1, w2, a0, a1, a2, tm):
    nseq, seq, d = x3.shape
    rows = nseq * seq
    lora = w1.shape[1]
    if seq >= tm:
        tps = seq // tm
        x_spec = pl.BlockSpec((1, tm, d), lambda i: (i // tps, i % tps, 0))
        first_spec = pl.BlockSpec((1, 1, d), lambda i: (i // tps, 0, 0))
        halo_spec = pl.BlockSpec(
            (1, V7X_SUBLANES, d),
            lambda i: (i // tps, jnp.maximum((i % tps) * (tm // V7X_SUBLANES) - 1, 0), 0))
    else:
        tps = 1
        x_spec = pl.BlockSpec((tm // seq, seq, d), lambda i: (i, 0, 0))
        first_spec = pl.BlockSpec((tm // seq, 1, d), lambda i: (i, 0, 0))
        halo_spec = pl.BlockSpec((1, V7X_SUBLANES, d), lambda i: (0, 0, 0))
    row_spec = pl.BlockSpec((tm, d), lambda i: (i, 0))
    out = jax.ShapeDtypeStruct((rows, d), F32)
    return pl.pallas_call(
        functools.partial(_rwkv_pre_kernel, tiles_per_seq=tps),
        grid=(rows // tm,),
        in_specs=[x_spec, first_spec, halo_spec, _const_spec((6, d)), _const_spec((4, d, d)),
                  _const_spec((1, d)), _const_spec((d, lora)), _const_spec((lora, d)), _const_spec((1, d)),
                  _const_spec((d, lora)), _const_spec((lora, d))],
        out_specs=[row_spec] * 6,
        out_shape=[out] * 6,
        compiler_params=_cparams("parallel"),
        name="rwkv_pre",
    )(x3, first.reshape(nseq, 1, d), x3, mu, w_in.astype(BF16), w0.reshape(1, d), w1.astype(BF16),
      w2.astype(BF16), a0.reshape(1, d), a1.astype(BF16), a2.astype(BF16))


def _mm(a, b, dims):
    return lax.dot_general(a.astype(BF16), b.astype(BF16), (dims, ((), ())), preferred_element_type=F32)


def _mm_nn(a, b):
    return _mm(a, b, ((1,), (0,)))


def _mm_nt(a, b):
    return _mm(a, b, ((1,), (1,)))


def _mm_tn(a, b):
    return _mm(a, b, ((0,), (0,)))


def _wkv_chunk(r, k, v, lw, a, k_k, k_a, r_k, gn_g, gn_b, states, nb, c):
    pairs = range(len(r))
    rows = nb * c
    n = STACK_ROWS
    hb = 2 * c
    lane = lax.broadcasted_iota(jnp.int32, (1, V7X_LANES), 1)
    head0 = lane < RWKV_HEAD
    mk = (jnp.where(head0, 1.0, 0.0).astype(F32), jnp.where(head0, 0.0, 1.0).astype(F32))

    def head_sum(t):
        s0 = jnp.sum(t * mk[0], axis=-1, keepdims=True)
        s1 = jnp.sum(t * mk[1], axis=-1, keepdims=True)
        return jnp.where(head0, s0, s1)

    def stack(t):
        parts = []
        for b in range(nb):
            tb = t[b * c:(b + 1) * c]
            parts += [tb * mk[0], tb * mk[1]]
        return jnp.concatenate(parts, axis=0).astype(BF16)

    ri = lax.broadcasted_iota(jnp.int32, (rows, rows), 0)
    ci = lax.broadcasted_iota(jnp.int32, (rows, rows), 1)
    tri = jnp.where((ri // c == ci // c) & (ri >= ci), 1.0, 0.0).astype(BF16)
    lw_hi = [lw[p].astype(BF16) for p in pairs]
    cum = [_mm_nn(tri, lw_hi[p]) + _mm_nn(tri, lw[p] - lw_hi[p].astype(F32)) for p in pairs]

    kkr = [k[p] * k_k[p] for p in pairs]
    kk = [kkr[p] / jnp.maximum(jnp.sqrt(head_sum(kkr[p] * kkr[p])), 1e-12) for p in pairs]
    kf = [k[p] * (1.0 + (a[p] - 1.0) * k_a[p]) for p in pairs]
    g_incl = [jnp.exp(cum[p]) for p in pairs]
    g_inv = [jnp.exp(-cum[p]) for p in pairs]
    at_s = [stack(-kk[p] * jnp.exp(cum[p] - lw[p])) for p in pairs]
    rt_s = [stack(r[p] * g_incl[p]) for p in pairs]
    bt_s = [stack(kk[p] * a[p] * g_inv[p]) for p in pairs]
    kt_s = [stack(kf[p] * g_inv[p]) for p in pairs]
    v_s = [stack(v[p]) for p in pairs]
    xa = [jnp.concatenate([at_s[p], rt_s[p]], axis=0) for p in pairs]
    yb = [jnp.concatenate([bt_s[p], kt_s[p]], axis=0) for p in pairs]

    si = lax.broadcasted_iota(jnp.int32, (n, n), 0)
    sj = lax.broadcasted_iota(jnp.int32, (n, n), 1)
    same = si // c == sj // c
    strict = same & (si > sj)
    incl = same & (si >= sj)
    eye = jnp.where(si == sj, 1.0, 0.0).astype(F32)

    gm = [_mm_nt(xa[p], yb[p]) for p in pairs]
    a_ab = [jnp.where(strict, gm[p][:n, :n], 0.0) for p in pairs]
    a_kr = [jnp.concatenate([jnp.where(strict, gm[p][:n, n:], 0.0),
                             jnp.where(incl, gm[p][n:, n:], 0.0)], axis=0).astype(BF16) for p in pairs]
    a_rb = [jnp.where(incl, gm[p][n:, :n], 0.0).astype(BF16) for p in pairs]

    tinv = [eye + a_ab[p] for p in pairs]
    pw = [a_ab[p].astype(BF16) for p in pairs]
    if c > 2:
        pw = [_mm_nn(pw[p], pw[p]).astype(BF16) for p in pairs]
        span = 4
        while span < c:
            sq = [_mm_nn(pw[p], jnp.concatenate([pw[p], tinv[p].astype(BF16)], axis=1)) for p in pairs]
            tinv = [tinv[p] + sq[p][:, n:] for p in pairs]
            pw = [sq[p][:, :n].astype(BF16) for p in pairs]
            span *= 2
        tinv = [tinv[p] + _mm_nn(pw[p], tinv[p]) for p in pairs]

    av = [_mm_nn(a_kr[p], v_s[p]) for p in pairs]
    if nb == 1:
        ah = [_mm_nt(xa[p], states[p][0]) for p in pairs]
        ah_a = [ah[p][:n] for p in pairs]
        ah_r = [ah[p][n:] for p in pairs]
    else:
        ah_a, ah_r = [], []
        for p in pairs:
            parts = []
            for b in range(nb):
                sl = slice(b * hb, (b + 1) * hb)
                parts.append(_mm_nt(jnp.concatenate([at_s[p][sl], rt_s[p][sl]], axis=0), states[p][b]))
            ah_a.append(jnp.concatenate([t[:hb] for t in parts], axis=0))
            ah_r.append(jnp.concatenate([t[hb:] for t in parts], axis=0))

    u = [_mm_nn(tinv[p], ah_a[p] + av[p][:n]).astype(BF16) for p in pairs]
    y_s = [ah_r[p] + av[p][n:] + _mm_nn(a_rb[p], u[p]) for p in pairs]

    new_states = []
    for p in pairs:
        per_batch = []
        for b in range(nb):
            sl = slice(b * hb, (b + 1) * hb)
            upd = _mm_tn(jnp.concatenate([u[p][sl], v_s[p][sl]], axis=0),
                         jnp.concatenate([bt_s[p][sl], kt_s[p][sl]], axis=0))
            g_last = g_incl[p][(b + 1) * c - 1:(b + 1) * c]
            per_batch.append((states[p][b] + upd) * g_last)
        new_states.append(per_batch)

    inv_n = 1.0 / RWKV_HEAD
    out = []
    for p in pairs:
        y = jnp.concatenate([y_s[p][b * hb:b * hb + c] + y_s[p][b * hb + c:(b + 1) * hb] for b in range(nb)],
                            axis=0)
        mean = head_sum(y) * inv_n
        yc = y - mean
        var = head_sum(yc * yc) * inv_n
        yn = yc * lax.rsqrt(var + GN_EPS) * gn_g[p] + gn_b[p]
        out.append(yn + head_sum(r[p] * kf[p] * r_k[p]) * v[p])
    return out, new_states


def _pair_lanes(p):
    return slice(p * V7X_LANES, (p + 1) * V7X_LANES)


def _state_to_blockdiag(s0, s1):
    z = jnp.zeros_like(s0)
    return jnp.concatenate([jnp.concatenate([s0, z], axis=1), jnp.concatenate([z, s1], axis=1)], axis=0)


def _pair_views(n_pairs, *refs):
    return [[ref[:, _pair_lanes(p)] for p in range(n_pairs)] for ref in refs]


def _wkv_prompt_kernel(r_ref, k_ref, v_ref, lw_ref, a_ref, kk_ref, ka_ref, rk_ref, gg_ref, gb_ref,
                       y_ref, sfin_ref, s_scr, *, n_batch, n_pairs, c):
    ci = pl.program_id(0)

    @pl.when(ci == 0)
    def _():
        s_scr[...] = jnp.zeros_like(s_scr)

    chains = [(b, p) for b in range(n_batch) for p in range(n_pairs)]
    data = [[ref[b, :, _pair_lanes(p)] for b, p in chains] for ref in (r_ref, k_ref, v_ref, lw_ref, a_ref)]
    prm = [[ref[:, _pair_lanes(p)] for _, p in chains] for ref in (kk_ref, ka_ref, rk_ref, gg_ref, gb_ref)]
    yn, new = _wkv_chunk(*data, *prm, [[s_scr[i]] for i in range(len(chains))], 1, c)
    for i, (b, p) in enumerate(chains):
        y_ref[b, :, _pair_lanes(p)] = yn[i]
        s_scr[i] = new[i][0]

    @pl.when(ci == pl.num_programs(0) - 1)
    def _():
        for i, (b, p) in enumerate(chains):
            s = s_scr[i]
            sfin_ref[b, 2 * p] = s[:RWKV_HEAD, :RWKV_HEAD]
            sfin_ref[b, 2 * p + 1] = s[RWKV_HEAD:, RWKV_HEAD:]


def _wkv_sample_kernel(r_ref, k_ref, v_ref, lw_ref, a_ref, kk_ref, ka_ref, rk_ref, gg_ref, gb_ref,
                       s0_ref, y_ref, sfin_ref, *, n_pairs, nb, c):
    args = _pair_views(n_pairs, r_ref, k_ref, v_ref, lw_ref, a_ref, kk_ref, ka_ref, rk_ref, gg_ref, gb_ref)
    states = [[_state_to_blockdiag(s0_ref[b, 2 * p], s0_ref[b, 2 * p + 1]) for b in range(nb)]
              for p in range(n_pairs)]
    yn, new = _wkv_chunk(*args, states, nb, c)
    for p in range(n_pairs):
        y_ref[:, _pair_lanes(p)] = yn[p]
        for b in range(nb):
            sfin_ref[b, 2 * p] = new[p][b][:RWKV_HEAD, :RWKV_HEAD]
            sfin_ref[b, 2 * p + 1] = new[p][b][RWKV_HEAD:, RWKV_HEAD:]


def _wkv_prompt(r, k, v, lw, a, k_k, k_a, r_k, gn_g, gn_b, batch, seq):
    rows, d = r.shape
    heads = d // RWKV_HEAD
    n_pairs = heads // HEADS_PER_VREG
    c = PROMPT_CHUNK
    row_spec = pl.BlockSpec((batch, c, d), lambda i: (0, i, 0))
    vec = _const_spec((1, d))
    kern = functools.partial(_wkv_prompt_kernel, n_batch=batch, n_pairs=n_pairs, c=c)
    seq3 = lambda t: t.reshape(batch, seq, d)
    y, s_fin = pl.pallas_call(
        kern,
        grid=(seq // c,),
        in_specs=[row_spec] * 5 + [vec] * 5,
        out_specs=[row_spec, pl.BlockSpec((batch, heads, RWKV_HEAD, RWKV_HEAD), lambda i: (0, 0, 0, 0))],
        out_shape=[jax.ShapeDtypeStruct((batch, seq, d), F32),
                   jax.ShapeDtypeStruct((batch, heads, RWKV_HEAD, RWKV_HEAD), F32)],
        scratch_shapes=[pltpu.VMEM((batch * n_pairs, V7X_LANES, V7X_LANES), F32)],
        compiler_params=_cparams("arbitrary"),
        name="wkv_prompt",
    )(seq3(r), seq3(k), seq3(v), seq3(lw), seq3(a), k_k, k_a, r_k, gn_g, gn_b)
    return y.reshape(rows, d), s_fin


def _wkv_sample(r, k, v, lw, a, k_k, k_a, r_k, gn_g, gn_b, state, batch, seq):
    rows, d = r.shape
    heads = d // RWKV_HEAD
    nb = WKV_ROWS // seq
    row_spec = pl.BlockSpec((WKV_ROWS, d), lambda i: (i, 0))
    st_spec = pl.BlockSpec((nb, heads, RWKV_HEAD, RWKV_HEAD), lambda i: (i, 0, 0, 0))
    vec = _const_spec((1, d))
    kern = functools.partial(_wkv_sample_kernel, n_pairs=heads // HEADS_PER_VREG, nb=nb, c=seq)
    return pl.pallas_call(
        kern,
        grid=(batch // nb,),
        in_specs=[row_spec] * 5 + [vec] * 5 + [st_spec],
        out_specs=[row_spec, st_spec],
        out_shape=[jax.ShapeDtypeStruct((rows, d), F32),
                   jax.ShapeDtypeStruct((batch, heads, RWKV_HEAD, RWKV_HEAD), F32)],
        compiler_params=_cparams("parallel"),
        name="wkv_sample",
    )(r, k, v, lw, a, k_k, k_a, r_k, gn_g, gn_b, state)


def _post_kernel(y_ref, g_ref, x_ref, w_ref, lng_ref, lnb_ref, o_ref):
    g = g_ref[...]
    h = (y_ref[...] * (g / (1.0 + jnp.exp(-g)))).astype(BF16)
    z = ALPHA * x_ref[...] + jnp.dot(h, w_ref[...], preferred_element_type=F32)
    mu = jnp.mean(z, axis=-1, keepdims=True)
    zc = z - mu
    var = jnp.mean(zc * zc, axis=-1, keepdims=True)
    o_ref[...] = zc * lax.rsqrt(var + LN_EPS) * lng_ref[...] + lnb_ref[...]


def _post(y, g, x2, w_out, ln_g, ln_b, tm):
    rows, d = x2.shape
    din = y.shape[1]
    in_spec = pl.BlockSpec((tm, din), lambda i: (i, 0))
    row_spec = pl.BlockSpec((tm, d), lambda i: (i, 0))
    return pl.pallas_call(
        _post_kernel,
        grid=(rows // tm,),
        in_specs=[in_spec, in_spec, row_spec, _const_spec((din, d)), _const_spec((1, d)), _const_spec((1, d))],
        out_specs=row_spec,
        out_shape=jax.ShapeDtypeStruct((rows, d), F32),
        compiler_params=_cparams("parallel"),
        name="branch_post",
    )(y, g, x2, w_out.astype(BF16), ln_g.reshape(1, d), ln_b.reshape(1, d))


def _att_pre_kernel(x_ref, w_ref, q_ref, k_ref, v_ref, g_ref, *, hq, hkv):
    z = jnp.dot(x_ref[...].astype(BF16), w_ref[...], preferred_element_type=F32)
    q_ref[...] = z[:, :hq]
    k_ref[...] = z[:, hq:hq + hkv]
    v_ref[...] = z[:, hq + hkv:hq + 2 * hkv]
    g_ref[...] = z[:, hq + 2 * hkv:]


def _att_pre(x2, w_in, tm):
    rows, d = x2.shape
    hq = ATT_KV_HEADS * ATT_GROUP * ATT_HEAD_DIM
    hkv = ATT_KV_HEADS * ATT_HEAD_DIM
    wide = pl.BlockSpec((tm, hq), lambda i: (i, 0))
    narrow = pl.BlockSpec((tm, hkv), lambda i: (i, 0))
    return pl.pallas_call(
        functools.partial(_att_pre_kernel, hq=hq, hkv=hkv),
        grid=(rows // tm,),
        in_specs=[pl.BlockSpec((tm, d), lambda i: (i, 0)), _const_spec(w_in.shape)],
        out_specs=[wide, narrow, narrow, wide],
        out_shape=[jax.ShapeDtypeStruct((rows, hq), F32), jax.ShapeDtypeStruct((rows, hkv), F32),
                   jax.ShapeDtypeStruct((rows, hkv), F32), jax.ShapeDtypeStruct((rows, hq), F32)],
        compiler_params=_cparams("parallel"),
        name="att_pre",
    )(x2, w_in.astype(BF16))


def _alibi_slope(h):
    n_heads = ATT_KV_HEADS * ATT_GROUP
    return 2.0 ** (-8.0 * (h + 1) / n_heads)


def _fold_masks(q_time):
    col = lax.broadcasted_iota(jnp.int32, q_time.shape, 1)
    from_prev = col > q_time
    dist = jnp.where(from_prev, q_time + WINDOW - col, q_time - col).astype(F32)
    return from_prev, dist


def _softmax_folded(sp, sc, slopes, sinks, from_prev, dist, dead):
    idx = range(len(sp))
    s = [jnp.where(from_prev, sp[i], sc[i]) * ATT_SCALE - slopes[i] * dist for i in idx]
    if dead is not None:
        s = [jnp.where(dead, NEG, s[i]) for i in idx]
    m = [jnp.maximum(jnp.max(s[i], axis=-1, keepdims=True), sinks[i]) for i in idx]
    p = [jnp.exp(s[i] - m[i]) for i in idx]
    den = [jnp.sum(p[i], axis=-1, keepdims=True) + jnp.exp(sinks[i] - m[i]) for i in idx]
    pp = [jnp.where(from_prev, p[i], 0.0).astype(BF16) for i in idx]
    pc = [jnp.where(from_prev, 0.0, p[i]).astype(BF16) for i in idx]
    return pp, pc, den


_NT = (((1,), (1,)), ((), ()))


def _att_prompt_kernel(sink_ref, q_ref, kc_ref, vc_ref, kp_ref, vp_ref, o_ref):
    blk = pl.program_id(1)
    from_prev, dist = _fold_masks(lax.broadcasted_iota(jnp.int32, (WINDOW, WINDOW), 0))
    dead = from_prev & (blk == 0)
    hd = ATT_HEAD_DIM
    heads = range(ATT_KV_HEADS * ATT_GROUP)
    kcat, vcat = [[jnp.concatenate([prev[:, kvh * hd:(kvh + 1) * hd], cur[:, kvh * hd:(kvh + 1) * hd]],
                                   axis=0).astype(BF16) for kvh in range(ATT_KV_HEADS)]
                  for prev, cur in ((kp_ref, kc_ref), (vp_ref, vc_ref))]
    s2 = [lax.dot_general(q_ref[:, h * hd:(h + 1) * hd].astype(BF16), kcat[h // ATT_GROUP], _NT,
                          preferred_element_type=F32) for h in heads]
    pp, pc, den = _softmax_folded([s2[h][:, :WINDOW] for h in heads], [s2[h][:, WINDOW:] for h in heads],
                                  [_alibi_slope(h) for h in heads], [sink_ref[h] for h in heads],
                                  from_prev, dist, dead)
    for h in heads:
        o = jnp.dot(jnp.concatenate([pp[h], pc[h]], axis=1), vcat[h // ATT_GROUP], preferred_element_type=F32)
        o_ref[:, h * hd:(h + 1) * hd] = o / den[h]


def _att_prompt(q, k, v, sinks, batch, seq):
    rows, hq = q.shape
    hkv = k.shape[1]
    nb = seq // WINDOW
    cur = lambda b, n: (b * nb + n, 0)
    prev = lambda b, n: (b * nb + jnp.maximum(n - 1, 0), 0)
    return pl.pallas_call(
        _att_prompt_kernel,
        grid=(batch, nb),
        in_specs=[pl.BlockSpec(memory_space=pltpu.SMEM),
                  pl.BlockSpec((WINDOW, hq), cur),
                  pl.BlockSpec((WINDOW, hkv), cur), pl.BlockSpec((WINDOW, hkv), cur),
                  pl.BlockSpec((WINDOW, hkv), prev), pl.BlockSpec((WINDOW, hkv), prev)],
        out_specs=pl.BlockSpec((WINDOW, hq), cur),
        out_shape=jax.ShapeDtypeStruct((rows, hq), F32),
        compiler_params=_cparams("parallel", "parallel"),
        name="att_prompt",
    )(sinks, q, k, v, k, v)


def _att_sample_kernel(sink_ref, q_ref, kn_ref, vn_ref, ckt_ref, cvt_ref, o_ref, kwt_ref, vwt_ref, *, nb, t):
    m = ATT_GROUP * t
    grp = lax.broadcasted_iota(jnp.int32, (m, 1), 0) // t
    from_prev, dist = _fold_masks(lax.broadcasted_iota(jnp.int32, (m, WINDOW), 0) % t)
    hd = ATT_HEAD_DIM
    pad = jnp.zeros((WINDOW - t, hd), F32)
    items = [(kvh, b) for kvh in range(ATT_KV_HEADS) for b in range(nb)]
    idx = range(len(items))
    slopes, sinks = [], []
    for kvh in range(ATT_KV_HEADS):
        slope = jnp.zeros((m, 1), F32)
        sink = jnp.zeros((m, 1), F32)
        for g in range(ATT_GROUP):
            h = kvh * ATT_GROUP + g
            slope = jnp.where(grp == g, _alibi_slope(h), slope)
            sink = jnp.where(grp == g, sink_ref[h], sink)
        slopes += [slope] * nb
        sinks += [sink] * nb
    qs = [jnp.concatenate([q_ref[b * t:(b + 1) * t, (kvh * ATT_GROUP + g) * hd:(kvh * ATT_GROUP + g + 1) * hd]
                           for g in range(ATT_GROUP)], axis=0).astype(BF16) for kvh, b in items]
    k_new = [kn_ref[b * t:(b + 1) * t, kvh * hd:(kvh + 1) * hd] for kvh, b in items]
    v_new = [vn_ref[b * t:(b + 1) * t, kvh * hd:(kvh + 1) * hd] for kvh, b in items]
    kt_old = [ckt_ref[b, kvh] for kvh, b in items]
    vt_old = [cvt_ref[b, kvh] for kvh, b in items]

    sp = [jnp.dot(qs[i], kt_old[i].astype(BF16), preferred_element_type=F32) for i in idx]
    sc = [lax.dot_general(qs[i], jnp.concatenate([k_new[i], pad], axis=0).astype(BF16), _NT,
                          preferred_element_type=F32) for i in idx]
    pp, pc, den = _softmax_folded(sp, sc, slopes, sinks, from_prev, dist, None)
    o = [(lax.dot_general(pp[i], vt_old[i].astype(BF16), _NT, preferred_element_type=F32)
          + jnp.dot(pc[i], jnp.concatenate([v_new[i], pad], axis=0).astype(BF16), preferred_element_type=F32))
         / den[i] for i in idx]
    for i, (kvh, b) in enumerate(items):
        for g in range(ATT_GROUP):
            h = kvh * ATT_GROUP + g
            o_ref[b * t:(b + 1) * t, h * hd:(h + 1) * hd] = o[i][g * t:(g + 1) * t]

    lane = lax.broadcasted_iota(jnp.int32, (hd, WINDOW), 1)
    for new, old, out_ref in ((k_new, kt_old, kwt_ref), (v_new, vt_old, vwt_ref)):
        new_t = [jnp.concatenate([pad, new[i]], axis=0).T for i in idx]
        for i, (kvh, b) in enumerate(items):
            out_ref[b, kvh] = jnp.where(lane >= WINDOW - t, new_t[i], pltpu.roll(old[i], WINDOW - t, axis=1))


def _att_sample(q, k, v, cache_kt, cache_vt, sinks, batch, t, nb):
    rows, hq = q.shape
    hkv = k.shape[1]
    row = lambda i: (i, 0)
    cache_spec = pl.BlockSpec((nb,) + cache_kt.shape[1:], lambda i: (i, 0, 0, 0))
    cache_shape = jax.ShapeDtypeStruct(cache_kt.shape, F32)
    return pl.pallas_call(
        functools.partial(_att_sample_kernel, nb=nb, t=t),
        grid=(batch // nb,),
        in_specs=[pl.BlockSpec(memory_space=pltpu.SMEM),
                  pl.BlockSpec((nb * t, hq), row), pl.BlockSpec((nb * t, hkv), row),
                  pl.BlockSpec((nb * t, hkv), row), cache_spec, cache_spec],
        out_specs=[pl.BlockSpec((nb * t, hq), row), cache_spec, cache_spec],
        out_shape=[jax.ShapeDtypeStruct((rows, hq), F32), cache_shape, cache_shape],
        compiler_params=_cparams("parallel"),
        name="att_sample",
    )(sinks, q, k, v, cache_kt, cache_vt)


ROW_TILE = 512


def _row_tile(rows):
    return ROW_TILE if rows % ROW_TILE == 0 else rows


def kernel(x_prompt, x_sample, state_wkv, state_shift, cache_win_k, cache_win_v, ln_g, ln_b, rwkv_mu, rwkv_w_in, rwkv_w0, rwkv_w1, rwkv_w2, rwkv_a0, rwkv_a1, rwkv_a2, rwkv_k_k, rwkv_k_a, rwkv_r_k, rwkv_gn_g, rwkv_gn_b, rwkv_w_out, att_w_in, att_sinks, att_w_out):
    bp, tp, d = x_prompt.shape
    bs, ts, _ = x_sample.shape
    assert tp % PROMPT_CHUNK == 0 and tp % WINDOW == 0 and WKV_ROWS % ts == 0 and bs % (WKV_ROWS // ts) == 0
    assert cache_win_k.shape[2] == WINDOW
    heads = d // RWKV_HEAD
    hkv = ATT_KV_HEADS * ATT_HEAD_DIM

    j = 0
    vec = lambda t: t.reshape(1, d)
    pre_w = (rwkv_mu[j], rwkv_w_in[j], rwkv_w0[j], rwkv_w1[j], rwkv_w2[j], rwkv_a0[j], rwkv_a1[j], rwkv_a2[j])
    head_w = (vec(rwkv_k_k[j]), vec(rwkv_k_a[j]), vec(rwkv_r_k[j]), vec(rwkv_gn_g[j]), vec(rwkv_gn_b[j]))

    xp2 = x_prompt.reshape(bp * tp, d)
    xs2 = x_sample.reshape(bs * ts, d)

    r, k, v, g, lw, a = _rwkv_pre(x_prompt, jnp.zeros((bp, d), F32), *pre_w, tm=_row_tile(bp * tp))
    yp, wkv_p = _wkv_prompt(r, k, v, lw, a, *head_w, batch=bp, seq=tp)
    xp1 = _post(yp, g, xp2, rwkv_w_out[j], ln_g[0], ln_b[0], tm=_row_tile(bp * tp))

    r, k, v, g, lw, a = _rwkv_pre(x_sample, state_shift[j], *pre_w, tm=_row_tile(bs * ts))
    ys, wkv_s = _wkv_sample(r, k, v, lw, a, *head_w, state_wkv[j], batch=bs, seq=ts)
    xs1 = _post(ys, g, xs2, rwkv_w_out[j], ln_g[0], ln_b[0], tm=_row_tile(bs * ts))

    q, k, v, gate = _att_pre(xp1, att_w_in[j], tm=_row_tile(bp * tp))
    o = _att_prompt(q, k, v, att_sinks[j], batch=bp, seq=tp)
    y_prompt = _post(o, gate, xp1, att_w_out[j], ln_g[1], ln_b[1], tm=_row_tile(bp * tp)).reshape(bp, tp, d)
    win_shape = (bp, WINDOW, ATT_KV_HEADS, ATT_HEAD_DIM)
    win_k_p = k.reshape(bp, tp, hkv)[:, tp - WINDOW:].reshape(win_shape)
    win_v_p = v.reshape(bp, tp, hkv)[:, tp - WINDOW:].reshape(win_shape)

    q, k, v, gate = _att_pre(xs1, att_w_in[j], tm=_row_tile(bs * ts))
    to_t = lambda c: jnp.transpose(c, (0, 2, 3, 1))
    o, win_k_s, win_v_s = _att_sample(q, k, v, to_t(cache_win_k[j]), to_t(cache_win_v[j]), att_sinks[j],
                                      batch=bs, t=ts, nb=WKV_ROWS // ts)
    y_sample = _post(o, gate, xs1, att_w_out[j], ln_g[1], ln_b[1], tm=_row_tile(bs * ts)).reshape(bs, ts, d)
    win_k_s = jnp.transpose(win_k_s, (0, 3, 1, 2))
    win_v_s = jnp.transpose(win_v_s, (0, 3, 1, 2))

    return (y_prompt, y_sample,
            wkv_p[None], x_prompt[:, -1][None], win_k_p[None], win_v_p[None],
            wkv_s[None], x_sample[:, -1][None], win_k_s[None], win_v_s[None])
```

```python
import functools

import jax
import jax.numpy as jnp
from jax import lax
from jax.experimental import pallas as pl
from jax.experimental.pallas import tpu as pltpu

F32 = jnp.float32
BF16 = jnp.bfloat16

RWKV_HEAD = 64
ATT_HEAD_DIM = 64
ATT_KV_HEADS = 4
ATT_GROUP = 4
WINDOW = 128
ATT_SCALE = ATT_HEAD_DIM ** -0.5
LOG2E = 1.4426950408889634
NEG = -1e30
DEPTH = 2
ALPHA = (2 * DEPTH) ** 0.25
LN_EPS = 1e-5
GN_EPS = 64e-5

V7X_LANES = 128
V7X_SUBLANES = 8
V7X_VMEM_LIMIT_BYTES = 56 * 1024 * 1024

HEADS_PER_VREG = V7X_LANES // RWKV_HEAD
STACK_ROWS = 128
WKV_ROWS = STACK_ROWS // HEADS_PER_VREG
PROMPT_CHUNK = WKV_ROWS
PROMPT_CHUNKS_PER_STEP = 2


def _cparams(*sem):
    return pltpu.CompilerParams(dimension_semantics=sem, vmem_limit_bytes=V7X_VMEM_LIMIT_BYTES)


def _const_spec(shape):
    nd = len(shape)
    return pl.BlockSpec(shape, lambda *_: (0,) * nd, pipeline_mode=pl.Buffered(1))


def _rwkv_pre_kernel(x_ref, first_ref, halo_ref, mu_ref, win_ref, w0_ref, w1_ref, w2_ref, a0_ref, a1_ref, a2_ref,
                     r_ref, k_ref, v_ref, g_ref, lw_ref, a_ref, *, tiles_per_seq):
    x3 = x_ref[...]
    nseq, rows, d = x3.shape
    first = first_ref[...]
    if tiles_per_seq > 1:
        first = jnp.where(pl.program_id(0) % tiles_per_seq == 0, first, halo_ref[:, -1:, :])
    t_idx = lax.broadcasted_iota(jnp.int32, (1, rows, 1), 1)
    x_prev = jnp.where(t_idx == 0, first, pltpu.roll(x3, 1, axis=1))
    x = x3.reshape(nseq * rows, d)
    xx = x_prev.reshape(nseq * rows, d) - x

    def mix(p):
        return (x + xx * mu_ref[p:p + 1, :]).astype(BF16)

    def proj(p):
        return jnp.dot(mix(p), win_ref[p], preferred_element_type=F32)

    r_ref[...] = proj(0)
    k_ref[...] = proj(1)
    v_ref[...] = proj(2)
    g_ref[...] = proj(3)
    hw = jnp.tanh(jnp.dot(mix(4), w1_ref[...], preferred_element_type=F32))
    wl = w0_ref[...] + jnp.dot(hw.astype(BF16), w2_ref[...], preferred_element_type=F32)
    z = -wl
    softplus = jnp.maximum(z, 0.0) + jnp.log(1.0 + jnp.exp(-jnp.abs(z)))
    lw_ref[...] = -jnp.exp(-softplus - 0.5)
    ha = jnp.dot(mix(5), a1_ref[...], preferred_element_type=F32)
    al = a0_ref[...] + jnp.dot(ha.astype(BF16), a2_ref[...], preferred_element_type=F32)
    a_ref[...] = 1.0 / (1.0 + jnp.exp(-al))


def _rwkv_pre(x3, first, mu, w_in, w0, w1, w2, a0, a1, a2, tm):
    nseq, seq, d = x3.shape
    rows = nseq * seq
    lora = w1.shape[1]
    if seq >= tm:
        tps = seq // tm
        x_spec = pl.BlockSpec((1, tm, d), lambda i: (i // tps, i % tps, 0))
        first_spec = pl.BlockSpec((1, 1, d), lambda i: (i // tps, 0, 0))
        halo_spec = pl.BlockSpec(
            (1, V7X_SUBLANES, d),
            lambda i: (i // tps, jnp.maximum((i % tps) * (tm // V7X_SUBLANES) - 1, 0), 0))
    else:
        tps = 1
        x_spec = pl.BlockSpec((tm // seq, seq, d), lambda i: (i, 0, 0))
        first_spec = pl.BlockSpec((tm // seq, 1, d), lambda i: (i, 0, 0))
        halo_spec = pl.BlockSpec((1, V7X_SUBLANES, d), lambda i: (0, 0, 0))
    row_spec = pl.BlockSpec((tm, d), lambda i: (i, 0))
    out = jax.ShapeDtypeStruct((rows, d), F32)
    return pl.pallas_call(
        functools.partial(_rwkv_pre_kernel, tiles_per_seq=tps),
        grid=(rows // tm,),
        in_specs=[x_spec, first_spec, halo_spec, _const_spec((6, d)), _const_spec((4, d, d)),
                  _const_spec((1, d)), _const_spec((d, lora)), _const_spec((lora, d)), _const_spec((1, d)),
                  _const_spec((d, lora)), _const_spec((lora, d))],
        out_specs=[row_spec] * 6,
        out_shape=[out] * 6,
        compiler_params=_cparams("parallel"),
        name="rwkv_pre",
    )(x3, first.reshape(nseq, 1, d), x3, mu, w_in.astype(BF16), w0.reshape(1, d), w1.astype(BF16),
      w2.astype(BF16), a0.reshape(1, d), a1.astype(BF16), a2.astype(BF16))


def _mm(a, b, dims):
    return lax.dot_general(a.astype(BF16), b.astype(BF16), (dims, ((), ())), preferred_element_type=F32)


def _mm_nn(a, b):
    return _mm(a, b, ((1,), (0,)))


def _mm_nt(a, b):
    return _mm(a, b, ((1,), (1,)))


def _mm_tn(a, b):
    return _mm(a, b, ((0,), (0,)))


def _wkv_prepare(r, k, v, lw, a, k_k, k_a, r_k, nb, c):
    chains = range(len(r))
    rows = nb * c
    n = STACK_ROWS
    lane = lax.broadcasted_iota(jnp.int32, (1, V7X_LANES), 1)
    head0 = lane < RWKV_HEAD
    mk = (jnp.where(head0, 1.0, 0.0).astype(F32), jnp.where(head0, 0.0, 1.0).astype(F32))

    def head_sum(t):
        s0 = jnp.sum(t * mk[0], axis=-1, keepdims=True)
        s1 = jnp.sum(t * mk[1], axis=-1, keepdims=True)
        return jnp.where(head0, s0, s1)

    def stack(t):
        parts = []
        for b in range(nb):
            tb = t[b * c:(b + 1) * c]
            parts += [tb * mk[0], tb * mk[1]]
        return jnp.concatenate(parts, axis=0).astype(BF16)

    ri = lax.broadcasted_iota(jnp.int32, (rows, rows), 0)
    ci = lax.broadcasted_iota(jnp.int32, (rows, rows), 1)
    tri = jnp.where((ri // c == ci // c) & (ri >= ci), 1.0, 0.0).astype(BF16)
    lw_hi = [lw[p].astype(BF16) for p in chains]
    cum = [_mm_nn(tri, lw_hi[p]) + _mm_nn(tri, lw[p] - lw_hi[p].astype(F32)) for p in chains]

    kkr = [k[p] * k_k[p] for p in chains]
    kk = [kkr[p] / jnp.maximum(jnp.sqrt(head_sum(kkr[p] * kkr[p])), 1e-12) for p in chains]
    kf = [k[p] * (1.0 + (a[p] - 1.0) * k_a[p]) for p in chains]
    g_incl = [jnp.exp(cum[p]) for p in chains]
    g_inv = [jnp.exp(-cum[p]) for p in chains]
    at_s = [stack(-kk[p] * jnp.exp(cum[p] - lw[p])) for p in chains]
    rt_s = [stack(r[p] * g_incl[p]) for p in chains]
    bt_s = [stack(kk[p] * a[p] * g_inv[p]) for p in chains]
    kt_s = [stack(kf[p] * g_inv[p]) for p in chains]
    v_s = [stack(v[p]) for p in chains]
    xa = [jnp.concatenate([at_s[p], rt_s[p]], axis=0) for p in chains]
    yb = [jnp.concatenate([bt_s[p], kt_s[p]], axis=0) for p in chains]

    si = lax.broadcasted_iota(jnp.int32, (n, n), 0)
    sj = lax.broadcasted_iota(jnp.int32, (n, n), 1)
    same = si // c == sj // c
    strict = same & (si > sj)
    incl = same & (si >= sj)
    eye = jnp.where(si == sj, 1.0, 0.0).astype(F32)

    gm = [_mm_nt(xa[p], yb[p]) for p in chains]
    a_ab = [jnp.where(strict, gm[p][:n, :n], 0.0) for p in chains]
    a_kr = [jnp.concatenate([jnp.where(strict, gm[p][:n, n:], 0.0),
                             jnp.where(incl, gm[p][n:, n:], 0.0)], axis=0).astype(BF16) for p in chains]
    a_rb = [jnp.where(incl, gm[p][n:, :n], 0.0).astype(BF16) for p in chains]

    tinv = [eye + a_ab[p] for p in chains]
    pw = [a_ab[p].astype(BF16) for p in chains]
    if c > 2:
        pw = [_mm_nn(pw[p], pw[p]).astype(BF16) for p in chains]
        span = 4
        while span < c:
            sq = [_mm_nn(pw[p], jnp.concatenate([pw[p], tinv[p].astype(BF16)], axis=1)) for p in chains]
            tinv = [tinv[p] + sq[p][:, n:] for p in chains]
            pw = [sq[p][:, :n].astype(BF16) for p in chains]
            span *= 2
        tinv = [tinv[p] + _mm_nn(pw[p], tinv[p]) for p in chains]

    av = [_mm_nn(a_kr[p], v_s[p]) for p in chains]
    bonus = [head_sum(r[p] * kf[p] * r_k[p]) * v[p] for p in chains]
    g_last = [[g_incl[p][(b + 1) * c - 1:(b + 1) * c] for b in range(nb)] for p in chains]
    return dict(xa=xa, yb=yb, at_s=at_s, rt_s=rt_s, v_s=v_s, tinv=[t.astype(BF16) for t in tinv], av=av,
                a_rb=a_rb, bonus=bonus, g_last=g_last, head_sum=head_sum)


def _wkv_apply(prep, sel, states, gn_g, gn_b, nb, c):
    n = STACK_ROWS
    hb = 2 * c
    idx = range(len(sel))
    xa, yb, at_s, rt_s, v_s = (prep[key] for key in ("xa", "yb", "at_s", "rt_s", "v_s"))
    if nb == 1:
        ah = [_mm_nt(xa[p], states[i][0]) for i, p in enumerate(sel)]
        ah_a = [ah[i][:n] for i in idx]
        ah_r = [ah[i][n:] for i in idx]
    else:
        ah_a, ah_r = [], []
        for i, p in enumerate(sel):
            parts = []
            for b in range(nb):
                sl = slice(b * hb, (b + 1) * hb)
                parts.append(_mm_nt(jnp.concatenate([at_s[p][sl], rt_s[p][sl]], axis=0), states[i][b]))
            ah_a.append(jnp.concatenate([t[:hb] for t in parts], axis=0))
            ah_r.append(jnp.concatenate([t[hb:] for t in parts], axis=0))

    u = [_mm_nn(prep["tinv"][p], ah_a[i] + prep["av"][p][:n]).astype(BF16) for i, p in enumerate(sel)]
    y_s = [ah_r[i] + prep["av"][p][n:] + _mm_nn(prep["a_rb"][p], u[i]) for i, p in enumerate(sel)]

    new_states = []
    for i, p in enumerate(sel):
        per_batch = []
        for b in range(nb):
            sl = slice(b * hb, (b + 1) * hb)
            bk = yb[p] if nb == 1 else jnp.concatenate([yb[p][sl], yb[p][n + b * hb:n + (b + 1) * hb]], axis=0)
            upd = _mm_tn(jnp.concatenate([u[i][sl], v_s[p][sl]], axis=0), bk)
            per_batch.append((states[i][b] + upd) * prep["g_last"][p][b])
        new_states.append(per_batch)

    head_sum = prep["head_sum"]
    inv_n = 1.0 / RWKV_HEAD
    out = []
    for i, p in enumerate(sel):
        y = jnp.concatenate([y_s[i][b * hb:b * hb + c] + y_s[i][b * hb + c:(b + 1) * hb] for b in range(nb)],
                            axis=0)
        mean = head_sum(y) * inv_n
        yc = y - mean
        var = head_sum(yc * yc) * inv_n
        out.append(yc * lax.rsqrt(var + GN_EPS) * gn_g[i] + gn_b[i] + prep["bonus"][p])
    return out, new_states


def _pair_lanes(p):
    return slice(p * V7X_LANES, (p + 1) * V7X_LANES)


def _state_to_blockdiag(s0, s1):
    z = jnp.zeros_like(s0)
    return jnp.concatenate([jnp.concatenate([s0, z], axis=1), jnp.concatenate([z, s1], axis=1)], axis=0)


def _wkv_prompt_kernel(r_ref, k_ref, v_ref, lw_ref, a_ref, kk_ref, ka_ref, rk_ref, gg_ref, gb_ref,
                       y_ref, sfin_ref, s_scr, *, n_batch, n_pairs, c, n_sub):
    ci = pl.program_id(0)

    @pl.when(ci == 0)
    def _():
        s_scr[...] = jnp.zeros_like(s_scr)

    seq_pairs = [(b, p) for b in range(n_batch) for p in range(n_pairs)]
    chains = [(j, b, p) for j in range(n_sub) for b, p in seq_pairs]
    data = [[ref[b, j * c:(j + 1) * c, _pair_lanes(p)] for j, b, p in chains]
            for ref in (r_ref, k_ref, v_ref, lw_ref, a_ref)]
    prm = [[ref[:, _pair_lanes(p)] for _, _, p in chains] for ref in (kk_ref, ka_ref, rk_ref)]
    prep = _wkv_prepare(*data, *prm, 1, c)
    gn = [[ref[:, _pair_lanes(p)] for _, p in seq_pairs] for ref in (gg_ref, gb_ref)]
    states = [[s_scr[i]] for i in range(len(seq_pairs))]
    for j in range(n_sub):
        sel = [j * len(seq_pairs) + i for i in range(len(seq_pairs))]
        yn, states = _wkv_apply(prep, sel, states, *gn, 1, c)
        for i, (b, p) in enumerate(seq_pairs):
            y_ref[b, j * c:(j + 1) * c, _pair_lanes(p)] = yn[i]
    for i in range(len(seq_pairs)):
        s_scr[i] = states[i][0]

    @pl.when(ci == pl.num_programs(0) - 1)
    def _():
        for i, (b, p) in enumerate(seq_pairs):
            s = s_scr[i]
            sfin_ref[b, 2 * p] = s[:RWKV_HEAD, :RWKV_HEAD]
            sfin_ref[b, 2 * p + 1] = s[RWKV_HEAD:, RWKV_HEAD:]


def _wkv_sample_kernel(r_ref, k_ref, v_ref, lw_ref, a_ref, kk_ref, ka_ref, rk_ref, gg_ref, gb_ref,
                       s0_ref, y_ref, sfin_ref, *, n_pairs, nb, c):
    pairs = range(n_pairs)
    args = [[ref[:, _pair_lanes(p)] for p in pairs]
            for ref in (r_ref, k_ref, v_ref, lw_ref, a_ref, kk_ref, ka_ref, rk_ref)]
    prep = _wkv_prepare(*args, nb, c)
    states = [[_state_to_blockdiag(s0_ref[b, 2 * p], s0_ref[b, 2 * p + 1]) for b in range(nb)] for p in pairs]
    gn = [[ref[:, _pair_lanes(p)] for p in pairs] for ref in (gg_ref, gb_ref)]
    yn, new = _wkv_apply(prep, list(pairs), states, *gn, nb, c)
    for p in pairs:
        y_ref[:, _pair_lanes(p)] = yn[p]
        for b in range(nb):
            sfin_ref[b, 2 * p] = new[p][b][:RWKV_HEAD, :RWKV_HEAD]
            sfin_ref[b, 2 * p + 1] = new[p][b][RWKV_HEAD:, RWKV_HEAD:]


def _wkv_prompt(r, k, v, lw, a, k_k, k_a, r_k, gn_g, gn_b, batch, seq):
    rows, d = r.shape
    heads = d // RWKV_HEAD
    n_pairs = heads // HEADS_PER_VREG
    c = PROMPT_CHUNK
    n_sub = PROMPT_CHUNKS_PER_STEP
    row_spec = pl.BlockSpec((batch, n_sub * c, d), lambda i: (0, i, 0))
    vec = _const_spec((1, d))
    kern = functools.partial(_wkv_prompt_kernel, n_batch=batch, n_pairs=n_pairs, c=c, n_sub=n_sub)
    seq3 = lambda t: t.reshape(batch, seq, d)
    y, s_fin = pl.pallas_call(
        kern,
        grid=(seq // (n_sub * c),),
        in_specs=[row_spec] * 5 + [vec] * 5,
        out_specs=[row_spec, pl.BlockSpec((batch, heads, RWKV_HEAD, RWKV_HEAD), lambda i: (0, 0, 0, 0))],
        out_shape=[jax.ShapeDtypeStruct((batch, seq, d), F32),
                   jax.ShapeDtypeStruct((batch, heads, RWKV_HEAD, RWKV_HEAD), F32)],
        scratch_shapes=[pltpu.VMEM((batch * n_pairs, V7X_LANES, V7X_LANES), F32)],
        compiler_params=_cparams("arbitrary"),
        name="wkv_prompt",
    )(seq3(r), seq3(k), seq3(v), seq3(lw), seq3(a), k_k, k_a, r_k, gn_g, gn_b)
    return y.reshape(rows, d), s_fin


def _wkv_sample(r, k, v, lw, a, k_k, k_a, r_k, gn_g, gn_b, state, batch, seq):
    rows, d = r.shape
    heads = d // RWKV_HEAD
    nb = WKV_ROWS // seq
    row_spec = pl.BlockSpec((WKV_ROWS, d), lambda i: (i, 0))
    st_spec = pl.BlockSpec((nb, heads, RWKV_HEAD, RWKV_HEAD), lambda i: (i, 0, 0, 0))
    vec = _const_spec((1, d))
    kern = functools.partial(_wkv_sample_kernel, n_pairs=heads // HEADS_PER_VREG, nb=nb, c=seq)
    return pl.pallas_call(
        kern,
        grid=(batch // nb,),
        in_specs=[row_spec] * 5 + [vec] * 5 + [st_spec],
        out_specs=[row_spec, st_spec],
        out_shape=[jax.ShapeDtypeStruct((rows, d), F32),
                   jax.ShapeDtypeStruct((batch, heads, RWKV_HEAD, RWKV_HEAD), F32)],
        compiler_params=_cparams("parallel"),
        name="wkv_sample",
    )(r, k, v, lw, a, k_k, k_a, r_k, gn_g, gn_b, state)


def _post_kernel(y_ref, g_ref, x_ref, w_ref, lng_ref, lnb_ref, o_ref):
    g = g_ref[...]
    h = (y_ref[...] * (g / (1.0 + jnp.exp(-g)))).astype(BF16)
    z = ALPHA * x_ref[...] + jnp.dot(h, w_ref[...], preferred_element_type=F32)
    mu = jnp.mean(z, axis=-1, keepdims=True)
    zc = z - mu
    var = jnp.mean(zc * zc, axis=-1, keepdims=True)
    o_ref[...] = zc * lax.rsqrt(var + LN_EPS) * lng_ref[...] + lnb_ref[...]


def _post(y, g, x2, w_out, ln_g, ln_b, tm):
    rows, d = x2.shape
    din = y.shape[1]
    in_spec = pl.BlockSpec((tm, din), lambda i: (i, 0))
    row_spec = pl.BlockSpec((tm, d), lambda i: (i, 0))
    return pl.pallas_call(
        _post_kernel,
        grid=(rows // tm,),
        in_specs=[in_spec, in_spec, row_spec, _const_spec((din, d)), _const_spec((1, d)), _const_spec((1, d))],
        out_specs=row_spec,
        out_shape=jax.ShapeDtypeStruct((rows, d), F32),
        compiler_params=_cparams("parallel"),
        name="branch_post",
    )(y, g, x2, w_out.astype(BF16), ln_g.reshape(1, d), ln_b.reshape(1, d))


def _att_pre_kernel(x_ref, w_ref, q_ref, k_ref, v_ref, g_ref, *, hq, hkv):
    z = jnp.dot(x_ref[...].astype(BF16), w_ref[...], preferred_element_type=F32)
    q_ref[...] = z[:, :hq] * (ATT_SCALE * LOG2E)
    k_ref[...] = z[:, hq:hq + hkv]
    v_ref[...] = z[:, hq + hkv:hq + 2 * hkv]
    g_ref[...] = z[:, hq + 2 * hkv:]


def _att_pre(x2, w_in, tm):
    rows, d = x2.shape
    hq = ATT_KV_HEADS * ATT_GROUP * ATT_HEAD_DIM
    hkv = ATT_KV_HEADS * ATT_HEAD_DIM
    wide = pl.BlockSpec((tm, hq), lambda i: (i, 0))
    narrow = pl.BlockSpec((tm, hkv), lambda i: (i, 0))
    return pl.pallas_call(
        functools.partial(_att_pre_kernel, hq=hq, hkv=hkv),
        grid=(rows // tm,),
        in_specs=[pl.BlockSpec((tm, d), lambda i: (i, 0)), _const_spec(w_in.shape)],
        out_specs=[wide, narrow, narrow, wide],
        out_shape=[jax.ShapeDtypeStruct((rows, hq), F32), jax.ShapeDtypeStruct((rows, hkv), F32),
                   jax.ShapeDtypeStruct((rows, hkv), F32), jax.ShapeDtypeStruct((rows, hq), F32)],
        compiler_params=_cparams("parallel"),
        name="att_pre",
    )(x2, w_in.astype(BF16))


def _alibi_slope(h):
    n_heads = ATT_KV_HEADS * ATT_GROUP
    return LOG2E * 2.0 ** (-8.0 * (h + 1) / n_heads)


def _fold_masks(q_time):
    col = lax.broadcasted_iota(jnp.int32, q_time.shape, 1)
    from_prev = col > q_time
    dist = jnp.where(from_prev, q_time + WINDOW - col, q_time - col).astype(F32)
    return from_prev, dist


def _softmax_folded(sp, sc, slopes, sinks, from_prev, dist, dead):
    idx = range(len(sp))
    s = [jnp.where(from_prev, sp[i], sc[i]) - slopes[i] * dist for i in idx]
    if dead is not None:
        s = [jnp.where(dead, NEG, s[i]) for i in idx]
    m = [jnp.maximum(jnp.max(s[i], axis=-1, keepdims=True), sinks[i]) for i in idx]
    p = [jnp.exp2(s[i] - m[i]) for i in idx]
    den = [jnp.sum(p[i], axis=-1, keepdims=True) + jnp.exp2(sinks[i] - m[i]) for i in idx]
    pp = [jnp.where(from_prev, p[i], 0.0).astype(BF16) for i in idx]
    pc = [jnp.where(from_prev, 0.0, p[i]).astype(BF16) for i in idx]
    return pp, pc, den


_NT = (((1,), (1,)), ((), ()))


def _att_prompt_kernel(sink_ref, q_ref, kc_ref, vc_ref, kp_ref, vp_ref, o_ref):
    blk = pl.program_id(1)
    from_prev, dist = _fold_masks(lax.broadcasted_iota(jnp.int32, (WINDOW, WINDOW), 0))
    dead = from_prev & (blk == 0)
    hd = ATT_HEAD_DIM
    heads = range(ATT_KV_HEADS * ATT_GROUP)
    kcat, vcat = [[jnp.concatenate([prev[:, kvh * hd:(kvh + 1) * hd], cur[:, kvh * hd:(kvh + 1) * hd]],
                                   axis=0).astype(BF16) for kvh in range(ATT_KV_HEADS)]
                  for prev, cur in ((kp_ref, kc_ref), (vp_ref, vc_ref))]
    s2 = [lax.dot_general(q_ref[:, h * hd:(h + 1) * hd].astype(BF16), kcat[h // ATT_GROUP], _NT,
                          preferred_element_type=F32) for h in heads]
    pp, pc, den = _softmax_folded([s2[h][:, :WINDOW] for h in heads], [s2[h][:, WINDOW:] for h in heads],
                                  [_alibi_slope(h) for h in heads], [sink_ref[h] * LOG2E for h in heads],
                                  from_prev, dist, dead)
    for h in heads:
        o = jnp.dot(jnp.concatenate([pp[h], pc[h]], axis=1), vcat[h // ATT_GROUP], preferred_element_type=F32)
        o_ref[:, h * hd:(h + 1) * hd] = o / den[h]


def _att_prompt(q, k, v, sinks, batch, seq):
    rows, hq = q.shape
    hkv = k.shape[1]
    nb = seq // WINDOW
    cur = lambda b, n: (b * nb + n, 0)
    prev = lambda b, n: (b * nb + jnp.maximum(n - 1, 0), 0)
    return pl.pallas_call(
        _att_prompt_kernel,
        grid=(batch, nb),
        in_specs=[pl.BlockSpec(memory_space=pltpu.SMEM),
                  pl.BlockSpec((WINDOW, hq), cur),
                  pl.BlockSpec((WINDOW, hkv), cur), pl.BlockSpec((WINDOW, hkv), cur),
                  pl.BlockSpec((WINDOW, hkv), prev), pl.BlockSpec((WINDOW, hkv), prev)],
        out_specs=pl.BlockSpec((WINDOW, hq), cur),
        out_shape=jax.ShapeDtypeStruct((rows, hq), F32),
        compiler_params=_cparams("parallel", "parallel"),
        name="att_prompt",
    )(sinks, q, k, v, k, v)


def _att_sample_kernel(sink_ref, q_ref, kn_ref, vn_ref, ckt_ref, cvt_ref, o_ref, kwt_ref, vwt_ref, *, nb, t):
    m = ATT_GROUP * t
    grp = lax.broadcasted_iota(jnp.int32, (m, 1), 0) // t
    from_prev, dist = _fold_masks(lax.broadcasted_iota(jnp.int32, (m, WINDOW), 0) % t)
    hd = ATT_HEAD_DIM
    pad = jnp.zeros((WINDOW - t, hd), F32)
    items = [(kvh, b) for kvh in range(ATT_KV_HEADS) for b in range(nb)]
    idx = range(len(items))
    slopes, sinks = [], []
    for kvh in range(ATT_KV_HEADS):
        slope = jnp.zeros((m, 1), F32)
        sink = jnp.zeros((m, 1), F32)
        for g in range(ATT_GROUP):
            h = kvh * ATT_GROUP + g
            slope = jnp.where(grp == g, _alibi_slope(h), slope)
            sink = jnp.where(grp == g, sink_ref[h] * LOG2E, sink)
        slopes += [slope] * nb
        sinks += [sink] * nb
    qs = [jnp.concatenate([q_ref[b * t:(b + 1) * t, (kvh * ATT_GROUP + g) * hd:(kvh * ATT_GROUP + g + 1) * hd]
                           for g in range(ATT_GROUP)], axis=0).astype(BF16) for kvh, b in items]
    k_new = [kn_ref[b * t:(b + 1) * t, kvh * hd:(kvh + 1) * hd] for kvh, b in items]
    v_new = [vn_ref[b * t:(b + 1) * t, kvh * hd:(kvh + 1) * hd] for kvh, b in items]
    kt_old = [ckt_ref[b, kvh] for kvh, b in items]
    vt_old = [cvt_ref[b, kvh] for kvh, b in items]

    sp = [jnp.dot(qs[i], kt_old[i].astype(BF16), preferred_element_type=F32) for i in idx]
    sc = [lax.dot_general(qs[i], jnp.concatenate([k_new[i], pad], axis=0).astype(BF16), _NT,
                          preferred_element_type=F32) for i in idx]
    pp, pc, den = _softmax_folded(sp, sc, slopes, sinks, from_prev, dist, None)
    o = [(lax.dot_general(pp[i], vt_old[i].astype(BF16), _NT, preferred_element_type=F32)
          + jnp.dot(pc[i], jnp.concatenate([v_new[i], pad], axis=0).astype(BF16), preferred_element_type=F32))
         / den[i] for i in idx]
    for i, (kvh, b) in enumerate(items):
        for g in range(ATT_GROUP):
            h = kvh * ATT_GROUP + g
            o_ref[b * t:(b + 1) * t, h * hd:(h + 1) * hd] = o[i][g * t:(g + 1) * t]

    lane = lax.broadcasted_iota(jnp.int32, (hd, WINDOW), 1)
    for new, old, out_ref in ((k_new, kt_old, kwt_ref), (v_new, vt_old, vwt_ref)):
        new_t = [jnp.concatenate([pad, new[i]], axis=0).T for i in idx]
        for i, (kvh, b) in enumerate(items):
            out_ref[b, kvh] = jnp.where(lane >= WINDOW - t, new_t[i], pltpu.roll(old[i], WINDOW - t, axis=1))


def _att_sample(q, k, v, cache_kt, cache_vt, sinks, batch, t, nb):
    rows, hq = q.shape
    hkv = k.shape[1]
    row = lambda i: (i, 0)
    cache_spec = pl.BlockSpec((nb,) + cache_kt.shape[1:], lambda i: (i, 0, 0, 0))
    cache_shape = jax.ShapeDtypeStruct(cache_kt.shape, F32)
    return pl.pallas_call(
        functools.partial(_att_sample_kernel, nb=nb, t=t),
        grid=(batch // nb,),
        in_specs=[pl.BlockSpec(memory_space=pltpu.SMEM),
                  pl.BlockSpec((nb * t, hq), row), pl.BlockSpec((nb * t, hkv), row),
                  pl.BlockSpec((nb * t, hkv), row), cache_spec, cache_spec],
        out_specs=[pl.BlockSpec((nb * t, hq), row), cache_spec, cache_spec],
        out_shape=[jax.ShapeDtypeStruct((rows, hq), F32), cache_shape, cache_shape],
        compiler_params=_cparams("parallel"),
        name="att_sample",
    )(sinks, q, k, v, cache_kt, cache_vt)


ROW_TILE = 512


def _row_tile(rows):
    return ROW_TILE if rows % ROW_TILE == 0 else rows


def kernel(x_prompt, x_sample, state_wkv, state_shift, cache_win_k, cache_win_v, ln_g, ln_b, rwkv_mu, rwkv_w_in, rwkv_w0, rwkv_w1, rwkv_w2, rwkv_a0, rwkv_a1, rwkv_a2, rwkv_k_k, rwkv_k_a, rwkv_r_k, rwkv_gn_g, rwkv_gn_b, rwkv_w_out, att_w_in, att_sinks, att_w_out):
    bp, tp, d = x_prompt.shape
    bs, ts, _ = x_sample.shape
    assert tp % (PROMPT_CHUNK * PROMPT_CHUNKS_PER_STEP) == 0 and tp % WINDOW == 0 and WKV_ROWS % ts == 0 and bs % (WKV_ROWS // ts) == 0
    assert cache_win_k.shape[2] == WINDOW
    heads = d // RWKV_HEAD
    hkv = ATT_KV_HEADS * ATT_HEAD_DIM

    j = 0
    vec = lambda t: t.reshape(1, d)
    pre_w = (rwkv_mu[j], rwkv_w_in[j], rwkv_w0[j], rwkv_w1[j], rwkv_w2[j], rwkv_a0[j], rwkv_a1[j], rwkv_a2[j])
    head_w = (vec(rwkv_k_k[j]), vec(rwkv_k_a[j]), vec(rwkv_r_k[j]), vec(rwkv_gn_g[j]), vec(rwkv_gn_b[j]))

    xp2 = x_prompt.reshape(bp * tp, d)
    xs2 = x_sample.reshape(bs * ts, d)

    r, k, v, g, lw, a = _rwkv_pre(x_prompt, jnp.zeros((bp, d), F32), *pre_w, tm=_row_tile(bp * tp))
    yp, wkv_p = _wkv_prompt(r, k, v, lw, a, *head_w, batch=bp, seq=tp)
    xp1 = _post(yp, g, xp2, rwkv_w_out[j], ln_g[0], ln_b[0], tm=_row_tile(bp * tp))

    r, k, v, g, lw, a = _rwkv_pre(x_sample, state_shift[j], *pre_w, tm=_row_tile(bs * ts))
    ys, wkv_s = _wkv_sample(r, k, v, lw, a, *head_w, state_wkv[j], batch=bs, seq=ts)
    xs1 = _post(ys, g, xs2, rwkv_w_out[j], ln_g[0], ln_b[0], tm=_row_tile(bs * ts))

    q, k, v, gate = _att_pre(xp1, att_w_in[j], tm=_row_tile(bp * tp))
    o = _att_prompt(q, k, v, att_sinks[j], batch=bp, seq=tp)
    y_prompt = _post(o, gate, xp1, att_w_out[j], ln_g[1], ln_b[1], tm=_row_tile(bp * tp)).reshape(bp, tp, d)
    win_shape = (bp, WINDOW, ATT_KV_HEADS, ATT_HEAD_DIM)
    win_k_p = k.reshape(bp, tp, hkv)[:, tp - WINDOW:].reshape(win_shape)
    win_v_p = v.reshape(bp, tp, hkv)[:, tp - WINDOW:].reshape(win_shape)

    q, k, v, gate = _att_pre(xs1, att_w_in[j], tm=_row_tile(bs * ts))
    to_t = lambda c: jnp.transpose(c, (0, 2, 3, 1))
    o, win_k_s, win_v_s = _att_sample(q, k, v, to_t(cache_win_k[j]), to_t(cache_win_v[j]), att_sinks[j],
                                      batch=bs, t=ts, nb=WKV_ROWS // ts)
    y_sample = _post(o, gate, xs1, att_w_out[j], ln_g[1], ln_b[1], tm=_row_tile(bs * ts)).reshape(bs, ts, d)
    win_k_s = jnp.transpose(win_k_s, (0, 3, 1, 2))
    win_v_s = jnp.transpose(win_v_s, (0, 3, 1, 2))

    return (y_prompt, y_sample,
            wkv_p[None], x_prompt[:, -1][None], win_k_p[None], win_v_p[None],
            wkv_s[None], x_sample[:, -1][None], win_k_s[None], win_v_s[None])
```

```python
import functools

import jax
import jax.numpy as jnp
from jax import lax
from jax.experimental import pallas as pl
from jax.experimental.pallas import tpu as pltpu

F32 = jnp.float32
BF16 = jnp.bfloat16

RWKV_HEAD = 64
ATT_HEAD_DIM = 64
ATT_KV_HEADS = 4
ATT_GROUP = 4
WINDOW = 128
ATT_SCALE = ATT_HEAD_DIM ** -0.5
LOG2E = 1.4426950408889634
NEG = -1e30
DEPTH = 2
ALPHA = (2 * DEPTH) ** 0.25
LN_EPS = 1e-5
GN_EPS = 64e-5

V7X_LANES = 128
V7X_SUBLANES = 8
V7X_VMEM_LIMIT_BYTES = 56 * 1024 * 1024

HEADS_PER_VREG = V7X_LANES // RWKV_HEAD
STACK_ROWS = 128
WKV_ROWS = STACK_ROWS // HEADS_PER_VREG
PROMPT_CHUNK = WKV_ROWS
PROMPT_CHUNKS_PER_STEP = 2


def _cparams(*sem):
    return pltpu.CompilerParams(dimension_semantics=sem, vmem_limit_bytes=V7X_VMEM_LIMIT_BYTES)


def _const_spec(shape):
    nd = len(shape)
    return pl.BlockSpec(shape, lambda *_: (0,) * nd, pipeline_mode=pl.Buffered(1))


def _rwkv_pre_kernel(x_ref, first_ref, halo_ref, mu_ref, win_ref, w0_ref, w1_ref, w2_ref, a0_ref, a1_ref, a2_ref,
                     r_ref, k_ref, v_ref, g_ref, lw_ref, a_ref, *, tiles_per_seq):
    x3 = x_ref[...]
    nseq, rows, d = x3.shape
    first = first_ref[...]
    if tiles_per_seq > 1:
        first = jnp.where(pl.program_id(0) % tiles_per_seq == 0, first, halo_ref[:, -1:, :])
    t_idx = lax.broadcasted_iota(jnp.int32, (1, rows, 1), 1)
    x_prev = jnp.where(t_idx == 0, first, pltpu.roll(x3, 1, axis=1))
    x = x3.reshape(nseq * rows, d)
    xx = x_prev.reshape(nseq * rows, d) - x

    def mix(p):
        return (x + xx * mu_ref[p:p + 1, :]).astype(BF16)

    def proj(p):
        return jnp.dot(mix(p), win_ref[p], preferred_element_type=F32)

    r_ref[...] = proj(0)
    k_ref[...] = proj(1)
    v_ref[...] = proj(2)
    g_ref[...] = proj(3)
    hw = jnp.tanh(jnp.dot(mix(4), w1_ref[...], preferred_element_type=F32))
    wl = w0_ref[...] + jnp.dot(hw.astype(BF16), w2_ref[...], preferred_element_type=F32)
    z = -wl
    softplus = jnp.maximum(z, 0.0) + jnp.log(1.0 + jnp.exp(-jnp.abs(z)))
    lw_ref[...] = -jnp.exp(-softplus - 0.5)
    ha = jnp.dot(mix(5), a1_ref[...], preferred_element_type=F32)
    al = a0_ref[...] + jnp.dot(ha.astype(BF16), a2_ref[...], preferred_element_type=F32)
    a_ref[...] = 1.0 / (1.0 + jnp.exp(-al))


def _rwkv_pre(x3, first, mu, w_in, w0, w1, w2, a0, a1, a2, tm):
    nseq, seq, d = x3.shape
    rows = nseq * seq
    lora = w1.shape[1]
    if seq >= tm:
        tps = seq // tm
        x_spec = pl.BlockSpec((1, tm, d), lambda i: (i // tps, i % tps, 0))
        first_spec = pl.BlockSpec((1, 1, d), lambda i: (i // tps, 0, 0))
        halo_spec = pl.BlockSpec(
            (1, V7X_SUBLANES, d),
            lambda i: (i // tps, jnp.maximum((i % tps) * (tm // V7X_SUBLANES) - 1, 0), 0))
    else:
        tps = 1
        x_spec = pl.BlockSpec((tm // seq, seq, d), lambda i: (i, 0, 0))
        first_spec = pl.BlockSpec((tm // seq, 1, d), lambda i: (i, 0, 0))
        halo_spec = pl.BlockSpec((1, V7X_SUBLANES, d), lambda i: (0, 0, 0))
    row_spec = pl.BlockSpec((tm, d), lambda i: (i, 0))
    out = jax.ShapeDtypeStruct((rows, d), F32)
    return pl.pallas_call(
        functools.partial(_rwkv_pre_kernel, tiles_per_seq=tps),
        grid=(rows // tm,),
        in_specs=[x_spec, first_spec, halo_spec, _const_spec((6, d)), _const_spec((4, d, d)),
                  _const_spec((1, d)), _const_spec((d, lora)), _const_spec((lora, d)), _const_spec((1, d)),
                  _const_spec((d, lora)), _const_spec((lora, d))],
        out_specs=[row_spec] * 6,
        out_shape=[out] * 6,
        compiler_params=_cparams("parallel"),
        name="rwkv_pre",
    )(x3, first.reshape(nseq, 1, d), x3, mu, w_in.astype(BF16), w0.reshape(1, d), w1.astype(BF16),
      w2.astype(BF16), a0.reshape(1, d), a1.astype(BF16), a2.astype(BF16))


def _mm(a, b, dims):
    return lax.dot_general(a.astype(BF16), b.astype(BF16), (dims, ((), ())), preferred_element_type=F32)


def _mm_nn(a, b):
    return _mm(a, b, ((1,), (0,)))


def _mm_nt(a, b):
    return _mm(a, b, ((1,), (1,)))


def _mm_tn(a, b):
    return _mm(a, b, ((0,), (0,)))


def _wkv_prepare(r, k, v, lw, a, k_k, k_a, r_k, nb, c):
    chains = range(len(r))
    rows = nb * c
    n = STACK_ROWS
    lane = lax.broadcasted_iota(jnp.int32, (1, V7X_LANES), 1)
    head0 = lane < RWKV_HEAD
    mk = (jnp.where(head0, 1.0, 0.0).astype(F32), jnp.where(head0, 0.0, 1.0).astype(F32))

    def head_sum(t):
        s0 = jnp.sum(t * mk[0], axis=-1, keepdims=True)
        s1 = jnp.sum(t * mk[1], axis=-1, keepdims=True)
        return jnp.where(head0, s0, s1)

    def stack(t):
        parts = []
        for b in range(nb):
            tb = t[b * c:(b + 1) * c]
            parts += [tb * mk[0], tb * mk[1]]
        return jnp.concatenate(parts, axis=0).astype(BF16)

    ri = lax.broadcasted_iota(jnp.int32, (rows, rows), 0)
    ci = lax.broadcasted_iota(jnp.int32, (rows, rows), 1)
    tri = jnp.where((ri // c == ci // c) & (ri >= ci), 1.0, 0.0).astype(BF16)
    lw_hi = [lw[p].astype(BF16) for p in chains]
    cum = [_mm_nn(tri, lw_hi[p]) + _mm_nn(tri, lw[p] - lw_hi[p].astype(F32)) for p in chains]

    kkr = [k[p] * k_k[p] for p in chains]
    kk = [kkr[p] / jnp.maximum(jnp.sqrt(head_sum(kkr[p] * kkr[p])), 1e-12) for p in chains]
    kf = [k[p] * (1.0 + (a[p] - 1.0) * k_a[p]) for p in chains]
    g_incl = [jnp.exp(cum[p]) for p in chains]
    g_inv = [jnp.exp(-cum[p]) for p in chains]
    at_s = [stack(-kk[p] * jnp.exp(cum[p] - lw[p])) for p in chains]
    rt_s = [stack(r[p] * g_incl[p]) for p in chains]
    bt_s = [stack(kk[p] * a[p] * g_inv[p]) for p in chains]
    kt_s = [stack(kf[p] * g_inv[p]) for p in chains]
    v_s = [stack(v[p]) for p in chains]
    xa = [jnp.concatenate([at_s[p], rt_s[p]], axis=0) for p in chains]
    yb = [jnp.concatenate([bt_s[p], kt_s[p]], axis=0) for p in chains]

    si = lax.broadcasted_iota(jnp.int32, (n, n), 0)
    sj = lax.broadcasted_iota(jnp.int32, (n, n), 1)
    same = si // c == sj // c
    strict = same & (si > sj)
    incl = same & (si >= sj)
    eye = jnp.where(si == sj, 1.0, 0.0).astype(F32)

    gm = [_mm_nt(xa[p], yb[p]) for p in chains]
    a_ab = [jnp.where(strict, gm[p][:n, :n], 0.0) for p in chains]
    a_kr = [jnp.concatenate([jnp.where(strict, gm[p][:n, n:], 0.0),
                             jnp.where(incl, gm[p][n:, n:], 0.0)], axis=0).astype(BF16) for p in chains]
    a_rb = [jnp.where(incl, gm[p][n:, :n], 0.0).astype(BF16) for p in chains]

    tinv = [eye + a_ab[p] for p in chains]
    pw = [a_ab[p].astype(BF16) for p in chains]
    if c > 2:
        pw = [_mm_nn(pw[p], pw[p]).astype(BF16) for p in chains]
        span = 4
        while span < c:
            sq = [_mm_nn(pw[p], jnp.concatenate([pw[p], tinv[p].astype(BF16)], axis=1)) for p in chains]
            tinv = [tinv[p] + sq[p][:, n:] for p in chains]
            pw = [sq[p][:, :n].astype(BF16) for p in chains]
            span *= 2
        tinv = [tinv[p] + _mm_nn(pw[p], tinv[p]) for p in chains]

    av = [_mm_nn(a_kr[p], v_s[p]) for p in chains]
    bonus = [head_sum(r[p] * kf[p] * r_k[p]) * v[p] for p in chains]
    g_last = [[g_incl[p][(b + 1) * c - 1:(b + 1) * c] for b in range(nb)] for p in chains]
    return dict(xa=xa, yb=yb, at_s=at_s, rt_s=rt_s, v_s=v_s, tinv=[t.astype(BF16) for t in tinv], av=av,
                a_rb=a_rb, bonus=bonus, g_last=g_last, head_sum=head_sum)


def _wkv_apply(prep, sel, states, gn_g, gn_b, nb, c):
    n = STACK_ROWS
    hb = 2 * c
    idx = range(len(sel))
    xa, yb, at_s, rt_s, v_s = (prep[key] for key in ("xa", "yb", "at_s", "rt_s", "v_s"))
    if nb == 1:
        ah = [_mm_nt(xa[p], states[i][0]) for i, p in enumerate(sel)]
        ah_a = [ah[i][:n] for i in idx]
        ah_r = [ah[i][n:] for i in idx]
    else:
        ah_a, ah_r = [], []
        for i, p in enumerate(sel):
            parts = []
            for b in range(nb):
                sl = slice(b * hb, (b + 1) * hb)
                parts.append(_mm_nt(jnp.concatenate([at_s[p][sl], rt_s[p][sl]], axis=0), states[i][b]))
            ah_a.append(jnp.concatenate([t[:hb] for t in parts], axis=0))
            ah_r.append(jnp.concatenate([t[hb:] for t in parts], axis=0))

    u = [_mm_nn(prep["tinv"][p], ah_a[i] + prep["av"][p][:n]).astype(BF16) for i, p in enumerate(sel)]
    y_s = [ah_r[i] + prep["av"][p][n:] + _mm_nn(prep["a_rb"][p], u[i]) for i, p in enumerate(sel)]

    new_states = []
    for i, p in enumerate(sel):
        per_batch = []
        for b in range(nb):
            sl = slice(b * hb, (b + 1) * hb)
            bk = yb[p] if nb == 1 else jnp.concatenate([yb[p][sl], yb[p][n + b * hb:n + (b + 1) * hb]], axis=0)
            upd = _mm_tn(jnp.concatenate([u[i][sl], v_s[p][sl]], axis=0), bk)
            per_batch.append((states[i][b] + upd) * prep["g_last"][p][b])
        new_states.append(per_batch)

    head_sum = prep["head_sum"]
    inv_n = 1.0 / RWKV_HEAD
    out = []
    for i, p in enumerate(sel):
        y = jnp.concatenate([y_s[i][b * hb:b * hb + c] + y_s[i][b * hb + c:(b + 1) * hb] for b in range(nb)],
                            axis=0)
        mean = head_sum(y) * inv_n
        yc = y - mean
        var = head_sum(yc * yc) * inv_n
        out.append(yc * lax.rsqrt(var + GN_EPS) * gn_g[i] + gn_b[i] + prep["bonus"][p])
    return out, new_states


def _pair_lanes(p):
    return slice(p * V7X_LANES, (p + 1) * V7X_LANES)


def _state_to_blockdiag(s0, s1):
    z = jnp.zeros_like(s0)
    return jnp.concatenate([jnp.concatenate([s0, z], axis=1), jnp.concatenate([z, s1], axis=1)], axis=0)


def _wkv_prompt_kernel(r_ref, k_ref, v_ref, lw_ref, a_ref, kk_ref, ka_ref, rk_ref, gg_ref, gb_ref,
                       y_ref, sfin_ref, s_scr, *, n_batch, n_pairs, c, n_sub):
    ci = pl.program_id(0)

    @pl.when(ci == 0)
    def _():
        s_scr[...] = jnp.zeros_like(s_scr)

    seq_pairs = [(b, p) for b in range(n_batch) for p in range(n_pairs)]
    chains = [(j, b, p) for j in range(n_sub) for b, p in seq_pairs]
    data = [[ref[b, j * c:(j + 1) * c, _pair_lanes(p)] for j, b, p in chains]
            for ref in (r_ref, k_ref, v_ref, lw_ref, a_ref)]
    prm = [[ref[:, _pair_lanes(p)] for _, _, p in chains] for ref in (kk_ref, ka_ref, rk_ref)]
    prep = _wkv_prepare(*data, *prm, 1, c)
    gn = [[ref[:, _pair_lanes(p)] for _, p in seq_pairs] for ref in (gg_ref, gb_ref)]
    states = [[s_scr[i]] for i in range(len(seq_pairs))]
    for j in range(n_sub):
        sel = [j * len(seq_pairs) + i for i in range(len(seq_pairs))]
        yn, states = _wkv_apply(prep, sel, states, *gn, 1, c)
        for i, (b, p) in enumerate(seq_pairs):
            y_ref[b, j * c:(j + 1) * c, _pair_lanes(p)] = yn[i]
    for i in range(len(seq_pairs)):
        s_scr[i] = states[i][0]

    @pl.when(ci == pl.num_programs(0) - 1)
    def _():
        for i, (b, p) in enumerate(seq_pairs):
            s = s_scr[i]
            sfin_ref[b, 2 * p] = s[:RWKV_HEAD, :RWKV_HEAD]
            sfin_ref[b, 2 * p + 1] = s[RWKV_HEAD:, RWKV_HEAD:]


def _wkv_sample_kernel(r_ref, k_ref, v_ref, lw_ref, a_ref, kk_ref, ka_ref, rk_ref, gg_ref, gb_ref,
                       s0_ref, y_ref, sfin_ref, *, n_pairs, nb, c):
    pairs = range(n_pairs)
    args = [[ref[:, _pair_lanes(p)] for p in pairs]
            for ref in (r_ref, k_ref, v_ref, lw_ref, a_ref, kk_ref, ka_ref, rk_ref)]
    prep = _wkv_prepare(*args, nb, c)
    states = [[_state_to_blockdiag(s0_ref[b, 2 * p], s0_ref[b, 2 * p + 1]) for b in range(nb)] for p in pairs]
    gn = [[ref[:, _pair_lanes(p)] for p in pairs] for ref in (gg_ref, gb_ref)]
    yn, new = _wkv_apply(prep, list(pairs), states, *gn, nb, c)
    for p in pairs:
        y_ref[:, _pair_lanes(p)] = yn[p]
        for b in range(nb):
            sfin_ref[b, 2 * p] = new[p][b][:RWKV_HEAD, :RWKV_HEAD]
            sfin_ref[b, 2 * p + 1] = new[p][b][RWKV_HEAD:, RWKV_HEAD:]


def _wkv_prompt(r, k, v, lw, a, k_k, k_a, r_k, gn_g, gn_b, batch, seq):
    rows, d = r.shape
    heads = d // RWKV_HEAD
    n_pairs = heads // HEADS_PER_VREG
    c = PROMPT_CHUNK
    n_sub = PROMPT_CHUNKS_PER_STEP
    row_spec = pl.BlockSpec((batch, n_sub * c, d), lambda i: (0, i, 0))
    vec = _const_spec((1, d))
    kern = functools.partial(_wkv_prompt_kernel, n_batch=batch, n_pairs=n_pairs, c=c, n_sub=n_sub)
    seq3 = lambda t: t.reshape(batch, seq, d)
    y, s_fin = pl.pallas_call(
        kern,
        grid=(seq // (n_sub * c),),
        in_specs=[row_spec] * 5 + [vec] * 5,
        out_specs=[row_spec, pl.BlockSpec((batch, heads, RWKV_HEAD, RWKV_HEAD), lambda i: (0, 0, 0, 0))],
        out_shape=[jax.ShapeDtypeStruct((batch, seq, d), F32),
                   jax.ShapeDtypeStruct((batch, heads, RWKV_HEAD, RWKV_HEAD), F32)],
        scratch_shapes=[pltpu.VMEM((batch * n_pairs, V7X_LANES, V7X_LANES), F32)],
        compiler_params=_cparams("arbitrary"),
        name="wkv_prompt",
    )(seq3(r), seq3(k), seq3(v), seq3(lw), seq3(a), k_k, k_a, r_k, gn_g, gn_b)
    return y.reshape(rows, d), s_fin


def _wkv_sample(r, k, v, lw, a, k_k, k_a, r_k, gn_g, gn_b, state, batch, seq):
    rows, d = r.shape
    heads = d // RWKV_HEAD
    nb = WKV_ROWS // seq
    row_spec = pl.BlockSpec((WKV_ROWS, d), lambda i: (i, 0))
    st_spec = pl.BlockSpec((nb, heads, RWKV_HEAD, RWKV_HEAD), lambda i: (i, 0, 0, 0))
    vec = _const_spec((1, d))
    kern = functools.partial(_wkv_sample_kernel, n_pairs=heads // HEADS_PER_VREG, nb=nb, c=seq)
    return pl.pallas_call(
        kern,
        grid=(batch // nb,),
        in_specs=[row_spec] * 5 + [vec] * 5 + [st_spec],
        out_specs=[row_spec, st_spec],
        out_shape=[jax.ShapeDtypeStruct((rows, d), F32),
                   jax.ShapeDtypeStruct((batch, heads, RWKV_HEAD, RWKV_HEAD), F32)],
        compiler_params=_cparams("parallel"),
        name="wkv_sample",
    )(r, k, v, lw, a, k_k, k_a, r_k, gn_g, gn_b, state)


def _branch_out(y_ref, g_ref, x_ref, w_ref, lng_ref, lnb_ref):
    g = g_ref[...]
    h = (y_ref[...] * (g / (1.0 + jnp.exp(-g)))).astype(BF16)
    z = ALPHA * x_ref[...] + jnp.dot(h, w_ref[...], preferred_element_type=F32)
    mu = jnp.mean(z, axis=-1, keepdims=True)
    zc = z - mu
    var = jnp.mean(zc * zc, axis=-1, keepdims=True)
    return zc * lax.rsqrt(var + LN_EPS) * lng_ref[...] + lnb_ref[...]


def _post_kernel(y_ref, g_ref, x_ref, w_ref, lng_ref, lnb_ref, o_ref):
    o_ref[...] = _branch_out(y_ref, g_ref, x_ref, w_ref, lng_ref, lnb_ref)


def _post_project_kernel(y_ref, g_ref, x_ref, w_ref, lng_ref, lnb_ref, win_ref,
                         o_ref, q_ref, k_ref, v_ref, gate_ref, *, hq, hkv):
    x1 = _branch_out(y_ref, g_ref, x_ref, w_ref, lng_ref, lnb_ref)
    o_ref[...] = x1
    z = jnp.dot(x1.astype(BF16), win_ref[...], preferred_element_type=F32)
    q_ref[...] = (z[:, :hq] * (ATT_SCALE * LOG2E)).astype(q_ref.dtype)
    k_ref[...] = z[:, hq:hq + hkv]
    v_ref[...] = z[:, hq + hkv:hq + 2 * hkv]
    gate_ref[...] = z[:, hq + 2 * hkv:]


def _post(y, g, x2, w_out, ln_g, ln_b, tm, project=None):
    rows, d = x2.shape
    din = y.shape[1]
    in_spec = pl.BlockSpec((tm, din), lambda i: (i, 0))
    row_spec = pl.BlockSpec((tm, d), lambda i: (i, 0))
    in_specs = [in_spec, in_spec, row_spec, _const_spec((din, d)), _const_spec((1, d)), _const_spec((1, d))]
    args = (y, g, x2, w_out.astype(BF16), ln_g.reshape(1, d), ln_b.reshape(1, d))
    x_shape = jax.ShapeDtypeStruct((rows, d), F32)
    if project is None:
        return pl.pallas_call(
            _post_kernel, grid=(rows // tm,), in_specs=in_specs, out_specs=row_spec, out_shape=x_shape,
            compiler_params=_cparams("parallel"), name="branch_post",
        )(*args)
    w_in, q_dtype = project
    hq = ATT_KV_HEADS * ATT_GROUP * ATT_HEAD_DIM
    hkv = ATT_KV_HEADS * ATT_HEAD_DIM
    wide = pl.BlockSpec((tm, hq), lambda i: (i, 0))
    narrow = pl.BlockSpec((tm, hkv), lambda i: (i, 0))
    return pl.pallas_call(
        functools.partial(_post_project_kernel, hq=hq, hkv=hkv),
        grid=(rows // tm,),
        in_specs=in_specs + [_const_spec(w_in.shape)],
        out_specs=[row_spec, wide, narrow, narrow, wide],
        out_shape=[x_shape, jax.ShapeDtypeStruct((rows, hq), q_dtype), jax.ShapeDtypeStruct((rows, hkv), F32),
                   jax.ShapeDtypeStruct((rows, hkv), F32), jax.ShapeDtypeStruct((rows, hq), F32)],
        compiler_params=_cparams("parallel"),
        name="branch_post_project",
    )(*args, w_in.astype(BF16))


def _alibi_slope(h):
    n_heads = ATT_KV_HEADS * ATT_GROUP
    return LOG2E * 2.0 ** (-8.0 * (h + 1) / n_heads)


def _fold_masks(q_time):
    col = lax.broadcasted_iota(jnp.int32, q_time.shape, 1)
    from_prev = col > q_time
    dist = jnp.where(from_prev, q_time + WINDOW - col, q_time - col).astype(F32)
    return from_prev, dist


def _softmax_folded(sp, sc, slopes, sinks, from_prev, dist, dead):
    idx = range(len(sp))
    s = [jnp.where(from_prev, sp[i], sc[i]) - slopes[i] * dist for i in idx]
    if dead is not None:
        s = [jnp.where(dead, NEG, s[i]) for i in idx]
    m = [jnp.maximum(jnp.max(s[i], axis=-1, keepdims=True), sinks[i]) for i in idx]
    p = [jnp.exp2(s[i] - m[i]) for i in idx]
    den = [jnp.sum(p[i], axis=-1, keepdims=True) + jnp.exp2(sinks[i] - m[i]) for i in idx]
    pp = [jnp.where(from_prev, p[i], 0.0).astype(BF16) for i in idx]
    pc = [jnp.where(from_prev, 0.0, p[i]).astype(BF16) for i in idx]
    return pp, pc, den


_NT = (((1,), (1,)), ((), ()))


def _att_prompt_kernel(sink_ref, q_ref, kc_ref, vc_ref, kp_ref, vp_ref, o_ref):
    blk = pl.program_id(1)
    from_prev, dist = _fold_masks(lax.broadcasted_iota(jnp.int32, (WINDOW, WINDOW), 0))
    dead = from_prev & (blk == 0)
    hd = ATT_HEAD_DIM
    heads = range(ATT_KV_HEADS * ATT_GROUP)
    kcat, vcat = [[jnp.concatenate([prev[:, kvh * hd:(kvh + 1) * hd], cur[:, kvh * hd:(kvh + 1) * hd]],
                                   axis=0).astype(BF16) for kvh in range(ATT_KV_HEADS)]
                  for prev, cur in ((kp_ref, kc_ref), (vp_ref, vc_ref))]
    s2 = [lax.dot_general(q_ref[:, h * hd:(h + 1) * hd].astype(BF16), kcat[h // ATT_GROUP], _NT,
                          preferred_element_type=F32) for h in heads]
    pp, pc, den = _softmax_folded([s2[h][:, :WINDOW] for h in heads], [s2[h][:, WINDOW:] for h in heads],
                                  [_alibi_slope(h) for h in heads], [sink_ref[h] * LOG2E for h in heads],
                                  from_prev, dist, dead)
    for h in heads:
        o = jnp.dot(jnp.concatenate([pp[h], pc[h]], axis=1), vcat[h // ATT_GROUP], preferred_element_type=F32)
        o_ref[:, h * hd:(h + 1) * hd] = o / den[h]


def _att_prompt(q, k, v, sinks, batch, seq):
    rows, hq = q.shape
    hkv = k.shape[1]
    nb = seq // WINDOW
    cur = lambda b, n: (b * nb + n, 0)
    prev = lambda b, n: (b * nb + jnp.maximum(n - 1, 0), 0)
    return pl.pallas_call(
        _att_prompt_kernel,
        grid=(batch, nb),
        in_specs=[pl.BlockSpec(memory_space=pltpu.SMEM),
                  pl.BlockSpec((WINDOW, hq), cur),
                  pl.BlockSpec((WINDOW, hkv), cur), pl.BlockSpec((WINDOW, hkv), cur),
                  pl.BlockSpec((WINDOW, hkv), prev), pl.BlockSpec((WINDOW, hkv), prev)],
        out_specs=pl.BlockSpec((WINDOW, hq), cur),
        out_shape=jax.ShapeDtypeStruct((rows, hq), F32),
        compiler_params=_cparams("parallel", "parallel"),
        name="att_prompt",
    )(sinks, q, k, v, k, v)


def _att_sample_kernel(sink_ref, q_ref, kn_ref, vn_ref, ckt_ref, cvt_ref, o_ref, kwt_ref, vwt_ref, *, nb, t):
    m = ATT_GROUP * t
    grp = lax.broadcasted_iota(jnp.int32, (m, 1), 0) // t
    from_prev, dist = _fold_masks(lax.broadcasted_iota(jnp.int32, (m, WINDOW), 0) % t)
    hd = ATT_HEAD_DIM
    pad = jnp.zeros((WINDOW - t, hd), F32)
    items = [(kvh, b) for kvh in range(ATT_KV_HEADS) for b in range(nb)]
    idx = range(len(items))
    slopes, sinks = [], []
    for kvh in range(ATT_KV_HEADS):
        slope = jnp.zeros((m, 1), F32)
        sink = jnp.zeros((m, 1), F32)
        for g in range(ATT_GROUP):
            h = kvh * ATT_GROUP + g
            slope = jnp.where(grp == g, _alibi_slope(h), slope)
            sink = jnp.where(grp == g, sink_ref[h] * LOG2E, sink)
        slopes += [slope] * nb
        sinks += [sink] * nb
    qs = [jnp.concatenate([q_ref[b * t:(b + 1) * t, (kvh * ATT_GROUP + g) * hd:(kvh * ATT_GROUP + g + 1) * hd]
                           for g in range(ATT_GROUP)], axis=0).astype(BF16) for kvh, b in items]
    k_new = [kn_ref[b * t:(b + 1) * t, kvh * hd:(kvh + 1) * hd] for kvh, b in items]
    v_new = [vn_ref[b * t:(b + 1) * t, kvh * hd:(kvh + 1) * hd] for kvh, b in items]
    kt_old = [ckt_ref[b, kvh] for kvh, b in items]
    vt_old = [cvt_ref[b, kvh] for kvh, b in items]

    sp = [jnp.dot(qs[i], kt_old[i].astype(BF16), preferred_element_type=F32) for i in idx]
    sc = [lax.dot_general(qs[i], jnp.concatenate([k_new[i], pad], axis=0).astype(BF16), _NT,
                          preferred_element_type=F32) for i in idx]
    pp, pc, den = _softmax_folded(sp, sc, slopes, sinks, from_prev, dist, None)
    o = [(lax.dot_general(pp[i], vt_old[i].astype(BF16), _NT, preferred_element_type=F32)
          + jnp.dot(pc[i], jnp.concatenate([v_new[i], pad], axis=0).astype(BF16), preferred_element_type=F32))
         / den[i] for i in idx]
    for i, (kvh, b) in enumerate(items):
        for g in range(ATT_GROUP):
            h = kvh * ATT_GROUP + g
            o_ref[b * t:(b + 1) * t, h * hd:(h + 1) * hd] = o[i][g * t:(g + 1) * t]

    lane = lax.broadcasted_iota(jnp.int32, (hd, WINDOW), 1)
    for new, old, out_ref in ((k_new, kt_old, kwt_ref), (v_new, vt_old, vwt_ref)):
        new_t = [jnp.concatenate([pad, new[i]], axis=0).T for i in idx]
        for i, (kvh, b) in enumerate(items):
            out_ref[b, kvh] = jnp.where(lane >= WINDOW - t, new_t[i], pltpu.roll(old[i], WINDOW - t, axis=1))


def _att_sample(q, k, v, cache_kt, cache_vt, sinks, batch, t, nb):
    rows, hq = q.shape
    hkv = k.shape[1]
    row = lambda i: (i, 0)
    cache_spec = pl.BlockSpec((nb,) + cache_kt.shape[1:], lambda i: (i, 0, 0, 0))
    cache_shape = jax.ShapeDtypeStruct(cache_kt.shape, F32)
    return pl.pallas_call(
        functools.partial(_att_sample_kernel, nb=nb, t=t),
        grid=(batch // nb,),
        in_specs=[pl.BlockSpec(memory_space=pltpu.SMEM),
                  pl.BlockSpec((nb * t, hq), row), pl.BlockSpec((nb * t, hkv), row),
                  pl.BlockSpec((nb * t, hkv), row), cache_spec, cache_spec],
        out_specs=[pl.BlockSpec((nb * t, hq), row), cache_spec, cache_spec],
        out_shape=[jax.ShapeDtypeStruct((rows, hq), F32), cache_shape, cache_shape],
        compiler_params=_cparams("parallel"),
        name="att_sample",
    )(sinks, q, k, v, cache_kt, cache_vt)


ROW_TILE = 512


def _row_tile(rows):
    return ROW_TILE if rows % ROW_TILE == 0 else rows


def kernel(x_prompt, x_sample, state_wkv, state_shift, cache_win_k, cache_win_v, ln_g, ln_b, rwkv_mu, rwkv_w_in, rwkv_w0, rwkv_w1, rwkv_w2, rwkv_a0, rwkv_a1, rwkv_a2, rwkv_k_k, rwkv_k_a, rwkv_r_k, rwkv_gn_g, rwkv_gn_b, rwkv_w_out, att_w_in, att_sinks, att_w_out):
    bp, tp, d = x_prompt.shape
    bs, ts, _ = x_sample.shape
    assert tp % (PROMPT_CHUNK * PROMPT_CHUNKS_PER_STEP) == 0 and tp % WINDOW == 0 and WKV_ROWS % ts == 0 and bs % (WKV_ROWS // ts) == 0
    assert cache_win_k.shape[2] == WINDOW
    heads = d // RWKV_HEAD
    hkv = ATT_KV_HEADS * ATT_HEAD_DIM

    j = 0
    vec = lambda t: t.reshape(1, d)
    pre_w = (rwkv_mu[j], rwkv_w_in[j], rwkv_w0[j], rwkv_w1[j], rwkv_w2[j], rwkv_a0[j], rwkv_a1[j], rwkv_a2[j])
    head_w = (vec(rwkv_k_k[j]), vec(rwkv_k_a[j]), vec(rwkv_r_k[j]), vec(rwkv_gn_g[j]), vec(rwkv_gn_b[j]))

    xp2 = x_prompt.reshape(bp * tp, d)
    xs2 = x_sample.reshape(bs * ts, d)

    r, k, v, g, lw, a = _rwkv_pre(x_prompt, jnp.zeros((bp, d), F32), *pre_w, tm=_row_tile(bp * tp))
    yp, wkv_p = _wkv_prompt(r, k, v, lw, a, *head_w, batch=bp, seq=tp)
    xp1, q_p, k_p, v_p, gate_p = _post(yp, g, xp2, rwkv_w_out[j], ln_g[0], ln_b[0], tm=_row_tile(bp * tp),
                                       project=(att_w_in[j], BF16))

    r, k, v, g, lw, a = _rwkv_pre(x_sample, state_shift[j], *pre_w, tm=_row_tile(bs * ts))
    ys, wkv_s = _wkv_sample(r, k, v, lw, a, *head_w, state_wkv[j], batch=bs, seq=ts)
    xs1, q_s, k_s, v_s, gate_s = _post(ys, g, xs2, rwkv_w_out[j], ln_g[0], ln_b[0], tm=_row_tile(bs * ts),
                                       project=(att_w_in[j], F32))

    o = _att_prompt(q_p, k_p, v_p, att_sinks[j], batch=bp, seq=tp)
    y_prompt = _post(o, gate_p, xp1, att_w_out[j], ln_g[1], ln_b[1], tm=_row_tile(bp * tp)).reshape(bp, tp, d)
    win_shape = (bp, WINDOW, ATT_KV_HEADS, ATT_HEAD_DIM)
    win_k_p = k_p.reshape(bp, tp, hkv)[:, tp - WINDOW:].reshape(win_shape)
    win_v_p = v_p.reshape(bp, tp, hkv)[:, tp - WINDOW:].reshape(win_shape)

    to_t = lambda c: jnp.transpose(c, (0, 2, 3, 1))
    o, win_k_s, win_v_s = _att_sample(q_s, k_s, v_s, to_t(cache_win_k[j]), to_t(cache_win_v[j]), att_sinks[j],
                                      batch=bs, t=ts, nb=WKV_ROWS // ts)
    y_sample = _post(o, gate_s, xs1, att_w_out[j], ln_g[1], ln_b[1], tm=_row_tile(bs * ts)).reshape(bs, ts, d)
    win_k_s = jnp.transpose(win_k_s, (0, 3, 1, 2))
    win_v_s = jnp.transpose(win_v_s, (0, 3, 1, 2))

    return (y_prompt, y_sample,
            wkv_p[None], x_prompt[:, -1][None], win_k_p[None], win_v_p[None],
            wkv_s[None], x_sample[:, -1][None], win_k_s[None], win_v_s[None])
```

```python
import functools

import jax
import jax.numpy as jnp
from jax import lax
from jax.experimental import pallas as pl
from jax.experimental.pallas import tpu as pltpu

F32 = jnp.float32
BF16 = jnp.bfloat16

RWKV_HEAD = 64
ATT_HEAD_DIM = 64
ATT_KV_HEADS = 4
ATT_GROUP = 4
WINDOW = 128
ATT_SCALE = ATT_HEAD_DIM ** -0.5
LOG2E = 1.4426950408889634
NEG = -1e30
DEPTH = 2
ALPHA = (2 * DEPTH) ** 0.25
LN_EPS = 1e-5
GN_EPS = 64e-5

V7X_LANES = 128
V7X_SUBLANES = 8
V7X_VMEM_LIMIT_BYTES = 56 * 1024 * 1024

HEADS_PER_VREG = V7X_LANES // RWKV_HEAD
STACK_ROWS = 128
WKV_ROWS = STACK_ROWS // HEADS_PER_VREG
PROMPT_CHUNK = WKV_ROWS
PROMPT_CHUNKS_PER_STEP = 2


def _cparams(*sem):
    return pltpu.CompilerParams(dimension_semantics=sem, vmem_limit_bytes=V7X_VMEM_LIMIT_BYTES)


def _const_spec(shape):
    nd = len(shape)
    return pl.BlockSpec(shape, lambda *_: (0,) * nd, pipeline_mode=pl.Buffered(1))


def _rwkv_pre_kernel(x_ref, first_ref, halo_ref, mu_ref, win_ref, w0_ref, w1_ref, w2_ref, a0_ref, a1_ref, a2_ref,
                     r_ref, k_ref, v_ref, g_ref, lw_ref, a_ref, *, tiles_per_seq):
    x3 = x_ref[...]
    nseq, rows, d = x3.shape
    first = first_ref[...]
    if tiles_per_seq > 1:
        first = jnp.where(pl.program_id(0) % tiles_per_seq == 0, first, halo_ref[:, -1:, :])
    t_idx = lax.broadcasted_iota(jnp.int32, (1, rows, 1), 1)
    x_prev = jnp.where(t_idx == 0, first, pltpu.roll(x3, 1, axis=1))
    x = x3.reshape(nseq * rows, d)
    xx = x_prev.reshape(nseq * rows, d) - x

    def mix(p):
        return (x + xx * mu_ref[p:p + 1, :]).astype(BF16)

    def proj(p):
        return jnp.dot(mix(p), win_ref[p], preferred_element_type=F32)

    r_ref[...] = proj(0)
    k_ref[...] = proj(1)
    v_ref[...] = proj(2)
    g_ref[...] = proj(3)
    hw = jnp.tanh(jnp.dot(mix(4), w1_ref[...], preferred_element_type=F32))
    wl = w0_ref[...] + jnp.dot(hw.astype(BF16), w2_ref[...], preferred_element_type=F32)
    z = -wl
    softplus = jnp.maximum(z, 0.0) + jnp.log(1.0 + jnp.exp(-jnp.abs(z)))
    lw_ref[...] = -jnp.exp(-softplus - 0.5)
    ha = jnp.dot(mix(5), a1_ref[...], preferred_element_type=F32)
    al = a0_ref[...] + jnp.dot(ha.astype(BF16), a2_ref[...], preferred_element_type=F32)
    a_ref[...] = 1.0 / (1.0 + jnp.exp(-al))


def _rwkv_pre(x3, first, mu, w_in, w0, w1, w2, a0, a1, a2, tm):
    nseq, seq, d = x3.shape
    rows = nseq * seq
    lora = w1.shape[1]
    if seq >= tm:
        tps = seq // tm
        x_spec = pl.BlockSpec((1, tm, d), lambda i: (i // tps, i % tps, 0))
        first_spec = pl.BlockSpec((1, 1, d), lambda i: (i // tps, 0, 0))
        halo_spec = pl.BlockSpec(
            (1, V7X_SUBLANES, d),
            lambda i: (i // tps, jnp.maximum((i % tps) * (tm // V7X_SUBLANES) - 1, 0), 0))
    else:
        tps = 1
        x_spec = pl.BlockSpec((tm // seq, seq, d), lambda i: (i, 0, 0))
        first_spec = pl.BlockSpec((tm // seq, 1, d), lambda i: (i, 0, 0))
        halo_spec = pl.BlockSpec((1, V7X_SUBLANES, d), lambda i: (0, 0, 0))
    row_spec = pl.BlockSpec((tm, d), lambda i: (i, 0))
    out = jax.ShapeDtypeStruct((rows, d), F32)
    return pl.pallas_call(
        functools.partial(_rwkv_pre_kernel, tiles_per_seq=tps),
        grid=(rows // tm,),
        in_specs=[x_spec, first_spec, halo_spec, _const_spec((6, d)), _const_spec((4, d, d)),
                  _const_spec((1, d)), _const_spec((d, lora)), _const_spec((lora, d)), _const_spec((1, d)),
                  _const_spec((d, lora)), _const_spec((lora, d))],
        out_specs=[row_spec] * 6,
        out_shape=[out] * 6,
        compiler_params=_cparams("parallel"),
        name="rwkv_pre",
    )(x3, first.reshape(nseq, 1, d), x3, mu, w_in.astype(BF16), w0.reshape(1, d), w1.astype(BF16),
      w2.astype(BF16), a0.reshape(1, d), a1.astype(BF16), a2.astype(BF16))


def _mm(a, b, dims):
    return lax.dot_general(a.astype(BF16), b.astype(BF16), (dims, ((), ())), preferred_element_type=F32)


def _mm_nn(a, b):
    return _mm(a, b, ((1,), (0,)))


def _mm_nt(a, b):
    return _mm(a, b, ((1,), (1,)))


def _mm_tn(a, b):
    return _mm(a, b, ((0,), (0,)))


def _wkv_prepare(r, k, v, lw, a, k_k, k_a, r_k, nb, c):
    chains = range(len(r))
    rows = nb * c
    n = STACK_ROWS
    lane = lax.broadcasted_iota(jnp.int32, (1, V7X_LANES), 1)
    head0 = lane < RWKV_HEAD
    mk = (jnp.where(head0, 1.0, 0.0).astype(F32), jnp.where(head0, 0.0, 1.0).astype(F32))

    def head_sum(t):
        s0 = jnp.sum(t * mk[0], axis=-1, keepdims=True)
        s1 = jnp.sum(t * mk[1], axis=-1, keepdims=True)
        return jnp.where(head0, s0, s1)

    def stack(t):
        parts = []
        for b in range(nb):
            tb = t[b * c:(b + 1) * c]
            parts += [tb * mk[0], tb * mk[1]]
        return jnp.concatenate(parts, axis=0).astype(BF16)

    ri = lax.broadcasted_iota(jnp.int32, (rows, rows), 0)
    ci = lax.broadcasted_iota(jnp.int32, (rows, rows), 1)
    tri = jnp.where((ri // c == ci // c) & (ri >= ci), 1.0, 0.0).astype(BF16)
    lw_hi = [lw[p].astype(BF16) for p in chains]
    cum = [_mm_nn(tri, lw_hi[p]) + _mm_nn(tri, lw[p] - lw_hi[p].astype(F32)) for p in chains]

    kkr = [k[p] * k_k[p] for p in chains]
    kk = [kkr[p] / jnp.maximum(jnp.sqrt(head_sum(kkr[p] * kkr[p])), 1e-12) for p in chains]
    kf = [k[p] * (1.0 + (a[p] - 1.0) * k_a[p]) for p in chains]
    g_incl = [jnp.exp(cum[p]) for p in chains]
    g_inv = [jnp.exp(-cum[p]) for p in chains]
    at_s = [stack(-kk[p] * jnp.exp(cum[p] - lw[p])) for p in chains]
    rt_s = [stack(r[p] * g_incl[p]) for p in chains]
    bt_s = [stack(kk[p] * a[p] * g_inv[p]) for p in chains]
    kt_s = [stack(kf[p] * g_inv[p]) for p in chains]
    v_s = [stack(v[p]) for p in chains]
    xa = [jnp.concatenate([at_s[p], rt_s[p]], axis=0) for p in chains]
    yb = [jnp.concatenate([bt_s[p], kt_s[p]], axis=0) for p in chains]

    si = lax.broadcasted_iota(jnp.int32, (n, n), 0)
    sj = lax.broadcasted_iota(jnp.int32, (n, n), 1)
    same = si // c == sj // c
    strict = same & (si > sj)
    incl = same & (si >= sj)
    eye = jnp.where(si == sj, 1.0, 0.0).astype(F32)

    gm = [_mm_nt(xa[p], yb[p]) for p in chains]
    a_ab = [jnp.where(strict, gm[p][:n, :n], 0.0) for p in chains]
    a_kr = [jnp.concatenate([jnp.where(strict, gm[p][:n, n:], 0.0),
                             jnp.where(incl, gm[p][n:, n:], 0.0)], axis=0).astype(BF16) for p in chains]
    a_rb = [jnp.where(incl, gm[p][n:, :n], 0.0).astype(BF16) for p in chains]

    tinv = [eye + a_ab[p] for p in chains]
    pw = [a_ab[p].astype(BF16) for p in chains]
    if c > 2:
        pw = [_mm_nn(pw[p], pw[p]).astype(BF16) for p in chains]
        span = 4
        while span < c:
            sq = [_mm_nn(pw[p], jnp.concatenate([pw[p], tinv[p].astype(BF16)], axis=1)) for p in chains]
            tinv = [tinv[p] + sq[p][:, n:] for p in chains]
            pw = [sq[p][:, :n].astype(BF16) for p in chains]
            span *= 2
        tinv = [tinv[p] + _mm_nn(pw[p], tinv[p]) for p in chains]

    av = [_mm_nn(a_kr[p], v_s[p]) for p in chains]
    bonus = [head_sum(r[p] * kf[p] * r_k[p]) * v[p] for p in chains]
    g_last = [[g_incl[p][(b + 1) * c - 1:(b + 1) * c] for b in range(nb)] for p in chains]
    return dict(xa=xa, yb=yb, at_s=at_s, rt_s=rt_s, v_s=v_s, tinv=[t.astype(BF16) for t in tinv], av=av,
                a_rb=a_rb, bonus=bonus, g_last=g_last, head_sum=head_sum)


def _wkv_apply(prep, sel, states, gn_g, gn_b, nb, c):
    n = STACK_ROWS
    hb = 2 * c
    idx = range(len(sel))
    xa, yb, at_s, rt_s, v_s = (prep[key] for key in ("xa", "yb", "at_s", "rt_s", "v_s"))
    if nb == 1:
        ah = [_mm_nt(xa[p], states[i][0]) for i, p in enumerate(sel)]
        ah_a = [ah[i][:n] for i in idx]
        ah_r = [ah[i][n:] for i in idx]
    else:
        ah_a, ah_r = [], []
        for i, p in enumerate(sel):
            parts = []
            for b in range(nb):
                sl = slice(b * hb, (b + 1) * hb)
                parts.append(_mm_nt(jnp.concatenate([at_s[p][sl], rt_s[p][sl]], axis=0), states[i][b]))
            ah_a.append(jnp.concatenate([t[:hb] for t in parts], axis=0))
            ah_r.append(jnp.concatenate([t[hb:] for t in parts], axis=0))

    u = [_mm_nn(prep["tinv"][p], ah_a[i] + prep["av"][p][:n]).astype(BF16) for i, p in enumerate(sel)]
    y_s = [ah_r[i] + prep["av"][p][n:] + _mm_nn(prep["a_rb"][p], u[i]) for i, p in enumerate(sel)]

    new_states = []
    for i, p in enumerate(sel):
        per_batch = []
        for b in range(nb):
            sl = slice(b * hb, (b + 1) * hb)
            bk = yb[p] if nb == 1 else jnp.concatenate([yb[p][sl], yb[p][n + b * hb:n + (b + 1) * hb]], axis=0)
            upd = _mm_tn(jnp.concatenate([u[i][sl], v_s[p][sl]], axis=0), bk)
            per_batch.append((states[i][b] + upd) * prep["g_last"][p][b])
        new_states.append(per_batch)

    head_sum = prep["head_sum"]
    inv_n = 1.0 / RWKV_HEAD
    out = []
    for i, p in enumerate(sel):
        y = jnp.concatenate([y_s[i][b * hb:b * hb + c] + y_s[i][b * hb + c:(b + 1) * hb] for b in range(nb)],
                            axis=0)
        mean = head_sum(y) * inv_n
        yc = y - mean
        var = head_sum(yc * yc) * inv_n
        out.append(yc * lax.rsqrt(var + GN_EPS) * gn_g[i] + gn_b[i] + prep["bonus"][p])
    return out, new_states


def _pair_lanes(p):
    return slice(p * V7X_LANES, (p + 1) * V7X_LANES)


def _state_to_blockdiag(s0, s1):
    z = jnp.zeros_like(s0)
    return jnp.concatenate([jnp.concatenate([s0, z], axis=1), jnp.concatenate([z, s1], axis=1)], axis=0)


def _wkv_prompt_kernel(r_ref, k_ref, v_ref, lw_ref, a_ref, kk_ref, ka_ref, rk_ref, gg_ref, gb_ref,
                       y_ref, sfin_ref, s_scr, *, n_batch, n_pairs, c, n_sub):
    ci = pl.program_id(0)

    @pl.when(ci == 0)
    def _():
        s_scr[...] = jnp.zeros_like(s_scr)

    seq_pairs = [(b, p) for b in range(n_batch) for p in range(n_pairs)]
    chains = [(j, b, p) for j in range(n_sub) for b, p in seq_pairs]
    data = [[ref[b, j * c:(j + 1) * c, _pair_lanes(p)] for j, b, p in chains]
            for ref in (r_ref, k_ref, v_ref, lw_ref, a_ref)]
    prm = [[ref[:, _pair_lanes(p)] for _, _, p in chains] for ref in (kk_ref, ka_ref, rk_ref)]
    prep = _wkv_prepare(*data, *prm, 1, c)
    gn = [[ref[:, _pair_lanes(p)] for _, p in seq_pairs] for ref in (gg_ref, gb_ref)]
    states = [[s_scr[i]] for i in range(len(seq_pairs))]
    for j in range(n_sub):
        sel = [j * len(seq_pairs) + i for i in range(len(seq_pairs))]
        yn, states = _wkv_apply(prep, sel, states, *gn, 1, c)
        for i, (b, p) in enumerate(seq_pairs):
            y_ref[b, j * c:(j + 1) * c, _pair_lanes(p)] = yn[i]
    for i in range(len(seq_pairs)):
        s_scr[i] = states[i][0]

    @pl.when(ci == pl.num_programs(0) - 1)
    def _():
        for i, (b, p) in enumerate(seq_pairs):
            s = s_scr[i]
            sfin_ref[b, 2 * p] = s[:RWKV_HEAD, :RWKV_HEAD]
            sfin_ref[b, 2 * p + 1] = s[RWKV_HEAD:, RWKV_HEAD:]


def _wkv_sample_kernel(r_ref, k_ref, v_ref, lw_ref, a_ref, kk_ref, ka_ref, rk_ref, gg_ref, gb_ref,
                       s0_ref, y_ref, sfin_ref, *, n_pairs, nb, c):
    pairs = range(n_pairs)
    args = [[ref[:, _pair_lanes(p)] for p in pairs]
            for ref in (r_ref, k_ref, v_ref, lw_ref, a_ref, kk_ref, ka_ref, rk_ref)]
    prep = _wkv_prepare(*args, nb, c)
    states = [[_state_to_blockdiag(s0_ref[b, 2 * p], s0_ref[b, 2 * p + 1]) for b in range(nb)] for p in pairs]
    gn = [[ref[:, _pair_lanes(p)] for p in pairs] for ref in (gg_ref, gb_ref)]
    yn, new = _wkv_apply(prep, list(pairs), states, *gn, nb, c)
    for p in pairs:
        y_ref[:, _pair_lanes(p)] = yn[p]
        for b in range(nb):
            sfin_ref[b, 2 * p] = new[p][b][:RWKV_HEAD, :RWKV_HEAD]
            sfin_ref[b, 2 * p + 1] = new[p][b][RWKV_HEAD:, RWKV_HEAD:]


def _wkv_prompt(r, k, v, lw, a, k_k, k_a, r_k, gn_g, gn_b, batch, seq):
    rows, d = r.shape
    heads = d // RWKV_HEAD
    n_pairs = heads // HEADS_PER_VREG
    c = PROMPT_CHUNK
    n_sub = PROMPT_CHUNKS_PER_STEP
    row_spec = pl.BlockSpec((batch, n_sub * c, d), lambda i: (0, i, 0))
    vec = _const_spec((1, d))
    kern = functools.partial(_wkv_prompt_kernel, n_batch=batch, n_pairs=n_pairs, c=c, n_sub=n_sub)
    seq3 = lambda t: t.reshape(batch, seq, d)
    y, s_fin = pl.pallas_call(
        kern,
        grid=(seq // (n_sub * c),),
        in_specs=[row_spec] * 5 + [vec] * 5,
        out_specs=[row_spec, pl.BlockSpec((batch, heads, RWKV_HEAD, RWKV_HEAD), lambda i: (0, 0, 0, 0))],
        out_shape=[jax.ShapeDtypeStruct((batch, seq, d), F32),
                   jax.ShapeDtypeStruct((batch, heads, RWKV_HEAD, RWKV_HEAD), F32)],
        scratch_shapes=[pltpu.VMEM((batch * n_pairs, V7X_LANES, V7X_LANES), F32)],
        compiler_params=_cparams("arbitrary"),
        name="wkv_prompt",
    )(seq3(r), seq3(k), seq3(v), seq3(lw), seq3(a), k_k, k_a, r_k, gn_g, gn_b)
    return y.reshape(rows, d), s_fin


def _wkv_sample(r, k, v, lw, a, k_k, k_a, r_k, gn_g, gn_b, state, batch, seq):
    rows, d = r.shape
    heads = d // RWKV_HEAD
    nb = WKV_ROWS // seq
    row_spec = pl.BlockSpec((WKV_ROWS, d), lambda i: (i, 0))
    st_spec = pl.BlockSpec((nb, heads, RWKV_HEAD, RWKV_HEAD), lambda i: (i, 0, 0, 0))
    vec = _const_spec((1, d))
    kern = functools.partial(_wkv_sample_kernel, n_pairs=heads // HEADS_PER_VREG, nb=nb, c=seq)
    return pl.pallas_call(
        kern,
        grid=(batch // nb,),
        in_specs=[row_spec] * 5 + [vec] * 5 + [st_spec],
        out_specs=[row_spec, st_spec],
        out_shape=[jax.ShapeDtypeStruct((rows, d), F32),
                   jax.ShapeDtypeStruct((batch, heads, RWKV_HEAD, RWKV_HEAD), F32)],
        compiler_params=_cparams("parallel"),
        name="wkv_sample",
    )(r, k, v, lw, a, k_k, k_a, r_k, gn_g, gn_b, state)


def _branch_out(y_ref, g_ref, x_ref, w_ref, lng_ref, lnb_ref):
    g = g_ref[...]
    h = (y_ref[...] * (g / (1.0 + jnp.exp(-g)))).astype(BF16)
    z = ALPHA * x_ref[...] + jnp.dot(h, w_ref[...], preferred_element_type=F32)
    mu = jnp.mean(z, axis=-1, keepdims=True)
    zc = z - mu
    var = jnp.mean(zc * zc, axis=-1, keepdims=True)
    return zc * lax.rsqrt(var + LN_EPS) * lng_ref[...] + lnb_ref[...]


def _post_kernel(y_ref, g_ref, x_ref, w_ref, lng_ref, lnb_ref, o_ref):
    o_ref[...] = _branch_out(y_ref, g_ref, x_ref, w_ref, lng_ref, lnb_ref)


def _post_project_kernel(y_ref, g_ref, x_ref, w_ref, lng_ref, lnb_ref, win_ref,
                         o_ref, q_ref, k_ref, v_ref, gate_ref, *, hq, hkv):
    x1 = _branch_out(y_ref, g_ref, x_ref, w_ref, lng_ref, lnb_ref)
    o_ref[...] = x1
    z = jnp.dot(x1.astype(BF16), win_ref[...], preferred_element_type=F32)
    q_ref[...] = (z[:, :hq] * (ATT_SCALE * LOG2E)).astype(q_ref.dtype)
    k_ref[...] = z[:, hq:hq + hkv]
    v_ref[...] = z[:, hq + hkv:hq + 2 * hkv]
    gate_ref[...] = z[:, hq + 2 * hkv:]


def _post(y, g, x2, w_out, ln_g, ln_b, tm, project=None):
    rows, d = x2.shape
    din = y.shape[1]
    in_spec = pl.BlockSpec((tm, din), lambda i: (i, 0))
    row_spec = pl.BlockSpec((tm, d), lambda i: (i, 0))
    in_specs = [in_spec, in_spec, row_spec, _const_spec((din, d)), _const_spec((1, d)), _const_spec((1, d))]
    args = (y, g, x2, w_out.astype(BF16), ln_g.reshape(1, d), ln_b.reshape(1, d))
    x_shape = jax.ShapeDtypeStruct((rows, d), F32)
    if project is None:
        return pl.pallas_call(
            _post_kernel, grid=(rows // tm,), in_specs=in_specs, out_specs=row_spec, out_shape=x_shape,
            compiler_params=_cparams("parallel"), name="branch_post",
        )(*args)
    w_in, q_dtype = project
    hq = ATT_KV_HEADS * ATT_GROUP * ATT_HEAD_DIM
    hkv = ATT_KV_HEADS * ATT_HEAD_DIM
    wide = pl.BlockSpec((tm, hq), lambda i: (i, 0))
    narrow = pl.BlockSpec((tm, hkv), lambda i: (i, 0))
    return pl.pallas_call(
        functools.partial(_post_project_kernel, hq=hq, hkv=hkv),
        grid=(rows // tm,),
        in_specs=in_specs + [_const_spec(w_in.shape)],
        out_specs=[row_spec, wide, narrow, narrow, wide],
        out_shape=[x_shape, jax.ShapeDtypeStruct((rows, hq), q_dtype), jax.ShapeDtypeStruct((rows, hkv), F32),
                   jax.ShapeDtypeStruct((rows, hkv), F32), jax.ShapeDtypeStruct((rows, hq), F32)],
        compiler_params=_cparams("parallel"),
        name="branch_post_project",
    )(*args, w_in.astype(BF16))


def _alibi_slope(h):
    n_heads = ATT_KV_HEADS * ATT_GROUP
    return LOG2E * 2.0 ** (-8.0 * (h + 1) / n_heads)


def _fold_masks(q_time):
    col = lax.broadcasted_iota(jnp.int32, q_time.shape, 1)
    from_prev = col > q_time
    dist = jnp.where(from_prev, q_time + WINDOW - col, q_time - col).astype(F32)
    return from_prev, dist


def _softmax_folded(sp, sc, slopes, sinks, from_prev, dist, dead):
    idx = range(len(sp))
    s = [jnp.where(from_prev, sp[i], sc[i]) - slopes[i] * dist for i in idx]
    if dead is not None:
        s = [jnp.where(dead, NEG, s[i]) for i in idx]
    m = [jnp.maximum(jnp.max(s[i], axis=-1, keepdims=True), sinks[i]) for i in idx]
    p = [jnp.exp2(s[i] - m[i]) for i in idx]
    den = [jnp.sum(p[i], axis=-1, keepdims=True) + jnp.exp2(sinks[i] - m[i]) for i in idx]
    pp = [jnp.where(from_prev, p[i], 0.0).astype(BF16) for i in idx]
    pc = [jnp.where(from_prev, 0.0, p[i]).astype(BF16) for i in idx]
    return pp, pc, den


_NT = (((1,), (1,)), ((), ()))


def _att_prompt_kernel(sink_ref, q_ref, kc_ref, vc_ref, kp_ref, vp_ref, gate_ref, x_ref, w_ref, lng_ref, lnb_ref,
                       y_ref, o_scr, *, n_acc):
    blk = pl.program_id(1)
    from_prev, dist = _fold_masks(lax.broadcasted_iota(jnp.int32, (WINDOW, WINDOW), 0))
    dead = from_prev & (blk == 0)
    hd = ATT_HEAD_DIM
    heads = range(ATT_KV_HEADS * ATT_GROUP)
    kcat, vcat = [[jnp.concatenate([prev[:, kvh * hd:(kvh + 1) * hd], cur[:, kvh * hd:(kvh + 1) * hd]],
                                   axis=0).astype(BF16) for kvh in range(ATT_KV_HEADS)]
                  for prev, cur in ((kp_ref, kc_ref), (vp_ref, vc_ref))]
    s2 = [lax.dot_general(q_ref[:, h * hd:(h + 1) * hd], kcat[h // ATT_GROUP], _NT, preferred_element_type=F32)
          for h in heads]
    pp, pc, den = _softmax_folded([s2[h][:, :WINDOW] for h in heads], [s2[h][:, WINDOW:] for h in heads],
                                  [_alibi_slope(h) for h in heads], [sink_ref[h] * LOG2E for h in heads],
                                  from_prev, dist, dead)
    row0 = pl.multiple_of((blk % n_acc) * WINDOW, WINDOW)
    for h in heads:
        o = jnp.dot(jnp.concatenate([pp[h], pc[h]], axis=1), vcat[h // ATT_GROUP], preferred_element_type=F32)
        o_scr[pl.ds(row0, WINDOW), h * hd:(h + 1) * hd] = o / den[h]

    @pl.when(blk % n_acc == n_acc - 1)
    def _():
        y_ref[...] = _branch_out(o_scr, gate_ref, x_ref, w_ref, lng_ref, lnb_ref)


def _att_prompt(q, k, v, gate, x2, sinks, w_out, ln_g, ln_b, batch, seq, tm):
    rows, hq = q.shape
    hkv = k.shape[1]
    d = x2.shape[1]
    nb = seq // WINDOW
    n_acc = tm // WINDOW
    cur = lambda b, n: (b * nb + n, 0)
    prev = lambda b, n: (b * nb + jnp.maximum(n - 1, 0), 0)
    tile = lambda b, n: ((b * nb + n) // n_acc, 0)
    return pl.pallas_call(
        functools.partial(_att_prompt_kernel, n_acc=n_acc),
        grid=(batch, nb),
        in_specs=[pl.BlockSpec(memory_space=pltpu.SMEM),
                  pl.BlockSpec((WINDOW, hq), cur),
                  pl.BlockSpec((WINDOW, hkv), cur), pl.BlockSpec((WINDOW, hkv), cur),
                  pl.BlockSpec((WINDOW, hkv), prev), pl.BlockSpec((WINDOW, hkv), prev),
                  pl.BlockSpec((tm, hq), tile), pl.BlockSpec((tm, d), tile),
                  _const_spec((hq, d)), _const_spec((1, d)), _const_spec((1, d))],
        out_specs=pl.BlockSpec((tm, d), tile),
        out_shape=jax.ShapeDtypeStruct((rows, d), F32),
        scratch_shapes=[pltpu.VMEM((tm, hq), F32)],
        compiler_params=_cparams("parallel", "arbitrary"),
        name="att_prompt",
    )(sinks, q, k, v, k, v, gate, x2, w_out.astype(BF16), ln_g.reshape(1, d), ln_b.reshape(1, d))


def _att_sample_kernel(sink_ref, q_ref, kn_ref, vn_ref, ckt_ref, cvt_ref, o_ref, kwt_ref, vwt_ref, *, nb, t):
    m = ATT_GROUP * t
    grp = lax.broadcasted_iota(jnp.int32, (m, 1), 0) // t
    from_prev, dist = _fold_masks(lax.broadcasted_iota(jnp.int32, (m, WINDOW), 0) % t)
    hd = ATT_HEAD_DIM
    pad = jnp.zeros((WINDOW - t, hd), F32)
    items = [(kvh, b) for kvh in range(ATT_KV_HEADS) for b in range(nb)]
    idx = range(len(items))
    slopes, sinks = [], []
    for kvh in range(ATT_KV_HEADS):
        slope = jnp.zeros((m, 1), F32)
        sink = jnp.zeros((m, 1), F32)
        for g in range(ATT_GROUP):
            h = kvh * ATT_GROUP + g
            slope = jnp.where(grp == g, _alibi_slope(h), slope)
            sink = jnp.where(grp == g, sink_ref[h] * LOG2E, sink)
        slopes += [slope] * nb
        sinks += [sink] * nb
    qs = [jnp.concatenate([q_ref[b * t:(b + 1) * t, (kvh * ATT_GROUP + g) * hd:(kvh * ATT_GROUP + g + 1) * hd]
                           for g in range(ATT_GROUP)], axis=0).astype(BF16) for kvh, b in items]
    k_new = [kn_ref[b * t:(b + 1) * t, kvh * hd:(kvh + 1) * hd] for kvh, b in items]
    v_new = [vn_ref[b * t:(b + 1) * t, kvh * hd:(kvh + 1) * hd] for kvh, b in items]
    kt_old = [ckt_ref[b, kvh] for kvh, b in items]
    vt_old = [cvt_ref[b, kvh] for kvh, b in items]

    sp = [jnp.dot(qs[i], kt_old[i].astype(BF16), preferred_element_type=F32) for i in idx]
    sc = [lax.dot_general(qs[i], jnp.concatenate([k_new[i], pad], axis=0).astype(BF16), _NT,
                          preferred_element_type=F32) for i in idx]
    pp, pc, den = _softmax_folded(sp, sc, slopes, sinks, from_prev, dist, None)
    o = [(lax.dot_general(pp[i], vt_old[i].astype(BF16), _NT, preferred_element_type=F32)
          + jnp.dot(pc[i], jnp.concatenate([v_new[i], pad], axis=0).astype(BF16), preferred_element_type=F32))
         / den[i] for i in idx]
    for i, (kvh, b) in enumerate(items):
        for g in range(ATT_GROUP):
            h = kvh * ATT_GROUP + g
            o_ref[b * t:(b + 1) * t, h * hd:(h + 1) * hd] = o[i][g * t:(g + 1) * t]

    lane = lax.broadcasted_iota(jnp.int32, (hd, WINDOW), 1)
    for new, old, out_ref in ((k_new, kt_old, kwt_ref), (v_new, vt_old, vwt_ref)):
        new_t = [jnp.concatenate([pad, new[i]], axis=0).T for i in idx]
        for i, (kvh, b) in enumerate(items):
            out_ref[b, kvh] = jnp.where(lane >= WINDOW - t, new_t[i], pltpu.roll(old[i], WINDOW - t, axis=1))


def _att_sample(q, k, v, cache_kt, cache_vt, sinks, batch, t, nb):
    rows, hq = q.shape
    hkv = k.shape[1]
    row = lambda i: (i, 0)
    cache_spec = pl.BlockSpec((nb,) + cache_kt.shape[1:], lambda i: (i, 0, 0, 0))
    cache_shape = jax.ShapeDtypeStruct(cache_kt.shape, F32)
    return pl.pallas_call(
        functools.partial(_att_sample_kernel, nb=nb, t=t),
        grid=(batch // nb,),
        in_specs=[pl.BlockSpec(memory_space=pltpu.SMEM),
                  pl.BlockSpec((nb * t, hq), row), pl.BlockSpec((nb * t, hkv), row),
                  pl.BlockSpec((nb * t, hkv), row), cache_spec, cache_spec],
        out_specs=[pl.BlockSpec((nb * t, hq), row), cache_spec, cache_spec],
        out_shape=[jax.ShapeDtypeStruct((rows, hq), F32), cache_shape, cache_shape],
        compiler_params=_cparams("parallel"),
        name="att_sample",
    )(sinks, q, k, v, cache_kt, cache_vt)


ROW_TILE = 512


def _row_tile(rows):
    return ROW_TILE if rows % ROW_TILE == 0 else rows


def kernel(x_prompt, x_sample, state_wkv, state_shift, cache_win_k, cache_win_v, ln_g, ln_b, rwkv_mu, rwkv_w_in, rwkv_w0, rwkv_w1, rwkv_w2, rwkv_a0, rwkv_a1, rwkv_a2, rwkv_k_k, rwkv_k_a, rwkv_r_k, rwkv_gn_g, rwkv_gn_b, rwkv_w_out, att_w_in, att_sinks, att_w_out):
    bp, tp, d = x_prompt.shape
    bs, ts, _ = x_sample.shape
    assert tp % (PROMPT_CHUNK * PROMPT_CHUNKS_PER_STEP) == 0 and tp % ROW_TILE == 0 and ROW_TILE % WINDOW == 0 and WKV_ROWS % ts == 0 and bs % (WKV_ROWS // ts) == 0
    assert cache_win_k.shape[2] == WINDOW
    heads = d // RWKV_HEAD
    hkv = ATT_KV_HEADS * ATT_HEAD_DIM

    j = 0
    vec = lambda t: t.reshape(1, d)
    pre_w = (rwkv_mu[j], rwkv_w_in[j], rwkv_w0[j], rwkv_w1[j], rwkv_w2[j], rwkv_a0[j], rwkv_a1[j], rwkv_a2[j])
    head_w = (vec(rwkv_k_k[j]), vec(rwkv_k_a[j]), vec(rwkv_r_k[j]), vec(rwkv_gn_g[j]), vec(rwkv_gn_b[j]))

    xp2 = x_prompt.reshape(bp * tp, d)
    xs2 = x_sample.reshape(bs * ts, d)

    r, k, v, g, lw, a = _rwkv_pre(x_prompt, jnp.zeros((bp, d), F32), *pre_w, tm=_row_tile(bp * tp))
    yp, wkv_p = _wkv_prompt(r, k, v, lw, a, *head_w, batch=bp, seq=tp)
    xp1, q_p, k_p, v_p, gate_p = _post(yp, g, xp2, rwkv_w_out[j], ln_g[0], ln_b[0], tm=_row_tile(bp * tp),
                                       project=(att_w_in[j], BF16))

    r, k, v, g, lw, a = _rwkv_pre(x_sample, state_shift[j], *pre_w, tm=_row_tile(bs * ts))
    ys, wkv_s = _wkv_sample(r, k, v, lw, a, *head_w, state_wkv[j], batch=bs, seq=ts)
    xs1, q_s, k_s, v_s, gate_s = _post(ys, g, xs2, rwkv_w_out[j], ln_g[0], ln_b[0], tm=_row_tile(bs * ts),
                                       project=(att_w_in[j], F32))

    y_prompt = _att_prompt(q_p, k_p, v_p, gate_p, xp1, att_sinks[j], att_w_out[j], ln_g[1], ln_b[1],
                           batch=bp, seq=tp, tm=ROW_TILE).reshape(bp, tp, d)
    win_shape = (bp, WINDOW, ATT_KV_HEADS, ATT_HEAD_DIM)
    win_k_p = k_p.reshape(bp, tp, hkv)[:, tp - WINDOW:].reshape(win_shape)
    win_v_p = v_p.reshape(bp, tp, hkv)[:, tp - WINDOW:].reshape(win_shape)

    to_t = lambda c: jnp.transpose(c, (0, 2, 3, 1))
    o, win_k_s, win_v_s = _att_sample(q_s, k_s, v_s, to_t(cache_win_k[j]), to_t(cache_win_v[j]), att_sinks[j],
                                      batch=bs, t=ts, nb=WKV_ROWS // ts)
    y_sample = _post(o, gate_s, xs1, att_w_out[j], ln_g[1], ln_b[1], tm=_row_tile(bs * ts)).reshape(bs, ts, d)
    win_k_s = jnp.transpose(win_k_s, (0, 3, 1, 2))
    win_v_s = jnp.transpose(win_v_s, (0, 3, 1, 2))

    return (y_prompt, y_sample,
            wkv_p[None], x_prompt[:, -1][None], win_k_p[None], win_v_p[None],
            wkv_s[None], x_sample[:, -1][None], win_k_s[None], win_v_s[None])
```

```python
import functools

import jax
import jax.numpy as jnp
from jax import lax
from jax.experimental import pallas as pl
from jax.experimental.pallas import tpu as pltpu

F32 = jnp.float32
BF16 = jnp.bfloat16

RWKV_HEAD = 64
ATT_HEAD_DIM = 64
ATT_KV_HEADS = 4
ATT_GROUP = 4
WINDOW = 128
ATT_SCALE = ATT_HEAD_DIM ** -0.5
LOG2E = 1.4426950408889634
NEG = -1e30
DEPTH = 2
ALPHA = (2 * DEPTH) ** 0.25
LN_EPS = 1e-5
GN_EPS = 64e-5

V7X_LANES = 128
V7X_SUBLANES = 8
V7X_VMEM_LIMIT_BYTES = 56 * 1024 * 1024

HEADS_PER_VREG = V7X_LANES // RWKV_HEAD
STACK_ROWS = 128
WKV_ROWS = STACK_ROWS // HEADS_PER_VREG
PROMPT_CHUNK = WKV_ROWS
PROMPT_CHUNKS_PER_STEP = 2


def _cparams(*sem):
    return pltpu.CompilerParams(dimension_semantics=sem, vmem_limit_bytes=V7X_VMEM_LIMIT_BYTES)


def _const_spec(shape):
    nd = len(shape)
    return pl.BlockSpec(shape, lambda *_: (0,) * nd, pipeline_mode=pl.Buffered(1))


def _rwkv_pre_kernel(x_ref, first_ref, halo_ref, mu_ref, win_ref, w0_ref, w1_ref, w2_ref, a0_ref, a1_ref, a2_ref,
                     r_ref, k_ref, v_ref, g_ref, lw_ref, a_ref, *, tiles_per_seq):
    x3 = x_ref[...]
    nseq, rows, d = x3.shape
    first = first_ref[...]
    if tiles_per_seq > 1:
        first = jnp.where(pl.program_id(0) % tiles_per_seq == 0, first, halo_ref[:, -1:, :])
    t_idx = lax.broadcasted_iota(jnp.int32, (1, rows, 1), 1)
    x_prev = jnp.where(t_idx == 0, first, pltpu.roll(x3, 1, axis=1))
    x = x3.reshape(nseq * rows, d)
    xx = x_prev.reshape(nseq * rows, d) - x

    def mix(p):
        return (x + xx * mu_ref[p:p + 1, :]).astype(BF16)

    def proj(p):
        return jnp.dot(mix(p), win_ref[p], preferred_element_type=F32)

    r_ref[...] = proj(0)
    k_ref[...] = proj(1)
    v_ref[...] = proj(2)
    g_ref[...] = proj(3)
    hw = jnp.tanh(jnp.dot(mix(4), w1_ref[...], preferred_element_type=F32))
    wl = w0_ref[...] + jnp.dot(hw.astype(BF16), w2_ref[...], preferred_element_type=F32)
    z = -wl
    softplus = jnp.maximum(z, 0.0) + jnp.log(1.0 + jnp.exp(-jnp.abs(z)))
    lw_ref[...] = -jnp.exp(-softplus - 0.5)
    ha = jnp.dot(mix(5), a1_ref[...], preferred_element_type=F32)
    al = a0_ref[...] + jnp.dot(ha.astype(BF16), a2_ref[...], preferred_element_type=F32)
    a_ref[...] = 1.0 / (1.0 + jnp.exp(-al))


def _rwkv_pre(x3, first, mu, w_in, w0, w1, w2, a0, a1, a2, tm):
    nseq, seq, d = x3.shape
    rows = nseq * seq
    lora = w1.shape[1]
    if seq >= tm:
        tps = seq // tm
        x_spec = pl.BlockSpec((1, tm, d), lambda i: (i // tps, i % tps, 0))
        first_spec = pl.BlockSpec((1, 1, d), lambda i: (i // tps, 0, 0))
        halo_spec = pl.BlockSpec(
            (1, V7X_SUBLANES, d),
            lambda i: (i // tps, jnp.maximum((i % tps) * (tm // V7X_SUBLANES) - 1, 0), 0))
    else:
        tps = 1
        x_spec = pl.BlockSpec((tm // seq, seq, d), lambda i: (i, 0, 0))
        first_spec = pl.BlockSpec((tm // seq, 1, d), lambda i: (i, 0, 0))
        halo_spec = pl.BlockSpec((1, V7X_SUBLANES, d), lambda i: (0, 0, 0))
    row_spec = pl.BlockSpec((tm, d), lambda i: (i, 0))
    out = jax.ShapeDtypeStruct((rows, d), F32)
    return pl.pallas_call(
        functools.partial(_rwkv_pre_kernel, tiles_per_seq=tps),
        grid=(rows // tm,),
        in_specs=[x_spec, first_spec, halo_spec, _const_spec((6, d)), _const_spec((4, d, d)),
                  _const_spec((1, d)), _const_spec((d, lora)), _const_spec((lora, d)), _const_spec((1, d)),
                  _const_spec((d, lora)), _const_spec((lora, d))],
        out_specs=[row_spec] * 6,
        out_shape=[out] * 6,
        compiler_params=_cparams("parallel"),
        name="rwkv_pre",
    )(x3, first.reshape(nseq, 1, d), x3, mu, w_in.astype(BF16), w0.reshape(1, d), w1.astype(BF16),
      w2.astype(BF16), a0.reshape(1, d), a1.astype(BF16), a2.astype(BF16))


def _mm(a, b, dims):
    return lax.dot_general(a.astype(BF16), b.astype(BF16), (dims, ((), ())), preferred_element_type=F32)


def _mm_nn(a, b):
    return _mm(a, b, ((1,), (0,)))


def _mm_nt(a, b):
    return _mm(a, b, ((1,), (1,)))


def _mm_tn(a, b):
    return _mm(a, b, ((0,), (0,)))


def _wkv_prepare(r, k, v, lw, a, k_k, k_a, r_k, nb, c):
    chains = range(len(r))
    rows = nb * c
    n = STACK_ROWS
    lane = lax.broadcasted_iota(jnp.int32, (1, V7X_LANES), 1)
    head0 = lane < RWKV_HEAD
    mk = (jnp.where(head0, 1.0, 0.0).astype(F32), jnp.where(head0, 0.0, 1.0).astype(F32))
    col_head0 = (lane // c) % HEADS_PER_VREG == 0
    cm = (jnp.where(col_head0, 1.0, 0.0).astype(F32), jnp.where(col_head0, 0.0, 1.0).astype(F32))

    def head_sum(t):
        s0 = jnp.sum(t * mk[0], axis=-1, keepdims=True)
        s1 = jnp.sum(t * mk[1], axis=-1, keepdims=True)
        return jnp.where(head0, s0, s1)

    def stack(t, masks):
        if c % (2 * V7X_SUBLANES) == 0:
            t = t.astype(BF16)
            masks = [m.astype(BF16) for m in masks]
        parts = []
        for b in range(nb):
            tb = t[b * c:(b + 1) * c]
            parts += [tb * masks[0], tb * masks[1]]
        return jnp.concatenate(parts, axis=0).astype(BF16)

    ri = lax.broadcasted_iota(jnp.int32, (rows, rows), 0)
    ci = lax.broadcasted_iota(jnp.int32, (rows, rows), 1)
    tri = jnp.where((ri // c == ci // c) & (ri >= ci), 1.0, 0.0).astype(BF16)
    lw_hi = [lw[p].astype(BF16) for p in chains]
    cum = [_mm_nn(tri, lw_hi[p]) + _mm_nn(tri, lw[p] - lw_hi[p].astype(F32)) for p in chains]

    kkr = [k[p] * k_k[p] for p in chains]
    kk = [kkr[p] / jnp.maximum(jnp.sqrt(head_sum(kkr[p] * kkr[p])), 1e-12) for p in chains]
    kf = [k[p] * (1.0 + (a[p] - 1.0) * k_a[p]) for p in chains]
    g_incl = [jnp.exp(cum[p]) for p in chains]
    g_inv = [jnp.exp(-cum[p]) for p in chains]
    at = [(-kk[p] * jnp.exp(cum[p] - lw[p])).astype(BF16) for p in chains]
    rt = [(r[p] * g_incl[p]).astype(BF16) for p in chains]
    bt = [(kk[p] * a[p] * g_inv[p]).astype(BF16) for p in chains]
    kt = [(kf[p] * g_inv[p]).astype(BF16) for p in chains]
    vb = [v[p].astype(BF16) for p in chains]
    xa = [jnp.concatenate([at[p], rt[p]], axis=0) for p in chains]
    yb = [jnp.concatenate([stack(bt[p], mk), stack(kt[p], mk)], axis=0) for p in chains]
    v_s = [stack(vb[p], mk) for p in chains]

    qi = lax.broadcasted_iota(jnp.int32, (rows, n), 0)
    qj = lax.broadcasted_iota(jnp.int32, (rows, n), 1)
    same = qi // c == qj // (HEADS_PER_VREG * c)
    strict = same & (qj % c < qi % c)
    incl = same & (qj % c <= qi % c)
    eye = jnp.where(same & (qj % c == qi % c), 1.0, 0.0).astype(F32)

    gm = [_mm_nt(xa[p], yb[p]) for p in chains]
    a_ab = [jnp.where(strict, gm[p][:rows, :n], 0.0) for p in chains]
    a_kr = [jnp.concatenate([jnp.where(strict, gm[p][:rows, n:], 0.0),
                             jnp.where(incl, gm[p][rows:, n:], 0.0)], axis=0).astype(BF16) for p in chains]
    a_rb = [jnp.where(incl, gm[p][rows:, :n], 0.0).astype(BF16) for p in chains]

    tinv = [eye + a_ab[p] for p in chains]
    pw = [a_ab[p] for p in chains]
    if c > 2:
        pw = [_mm_nn(pw[p], stack(pw[p], cm)) for p in chains]
        span = 4
        while span < c:
            sq = [_mm_nn(pw[p], jnp.concatenate([stack(pw[p], cm), stack(tinv[p], cm)], axis=1)) for p in chains]
            tinv = [tinv[p] + sq[p][:, n:] for p in chains]
            pw = [sq[p][:, :n] for p in chains]
            span *= 2
        tinv = [tinv[p] + _mm_nn(pw[p], stack(tinv[p], cm)) for p in chains]

    av = [_mm_nn(a_kr[p], v_s[p]) for p in chains]
    bonus = [head_sum(r[p] * kf[p] * r_k[p]) * v[p] for p in chains]
    g_last = [[g_incl[p][(b + 1) * c - 1:(b + 1) * c] for b in range(nb)] for p in chains]
    return dict(xa=xa, at=at, rt=rt, bt=bt, kt=kt, vb=vb, tinv=[t.astype(BF16) for t in tinv], av=av, a_rb=a_rb,
                bonus=bonus, g_last=g_last, head_sum=head_sum, stack=lambda t: stack(t, mk))


def _wkv_apply(prep, sel, states, gn_g, gn_b, nb, c):
    rows = nb * c
    xa, at, rt, bt, kt, vb = (prep[key] for key in ("xa", "at", "rt", "bt", "kt", "vb"))
    stack = prep["stack"]
    if nb == 1:
        ah = [_mm_nt(xa[p], states[i][0]) for i, p in enumerate(sel)]
    else:
        ah = []
        for i, p in enumerate(sel):
            parts = [_mm_nt(jnp.concatenate([at[p][b * c:(b + 1) * c], rt[p][b * c:(b + 1) * c]], axis=0), states[i][b])
                     for b in range(nb)]
            ah.append(jnp.concatenate([t[:c] for t in parts] + [t[c:] for t in parts], axis=0))

    u = [_mm_nn(prep["tinv"][p], stack(ah[i][:rows] + prep["av"][p][:rows])) for i, p in enumerate(sel)]
    y = [ah[i][rows:] + prep["av"][p][rows:] + _mm_nn(prep["a_rb"][p], stack(u[i])) for i, p in enumerate(sel)]

    vi = lax.broadcasted_iota(jnp.int32, (V7X_LANES, V7X_LANES), 0)
    ki = lax.broadcasted_iota(jnp.int32, (V7X_LANES, V7X_LANES), 1)
    same_head = vi // RWKV_HEAD == ki // RWKV_HEAD
    new_states = []
    for i, p in enumerate(sel):
        ub = u[i].astype(BF16)
        per_batch = []
        for b in range(nb):
            sl = slice(b * c, (b + 1) * c)
            upd = _mm_tn(jnp.concatenate([ub[sl], vb[p][sl]], axis=0), jnp.concatenate([bt[p][sl], kt[p][sl]], axis=0))
            per_batch.append((states[i][b] + jnp.where(same_head, upd, 0.0)) * prep["g_last"][p][b])
        new_states.append(per_batch)

    head_sum = prep["head_sum"]
    inv_n = 1.0 / RWKV_HEAD
    out = []
    for i, p in enumerate(sel):
        mean = head_sum(y[i]) * inv_n
        yc = y[i] - mean
        var = head_sum(yc * yc) * inv_n
        out.append(yc * lax.rsqrt(var + GN_EPS) * gn_g[i] + gn_b[i] + prep["bonus"][p])
    return out, new_states


def _pair_lanes(p):
    return slice(p * V7X_LANES, (p + 1) * V7X_LANES)


def _state_to_blockdiag(s0, s1):
    z = jnp.zeros_like(s0)
    return jnp.concatenate([jnp.concatenate([s0, z], axis=1), jnp.concatenate([z, s1], axis=1)], axis=0)


def _wkv_prompt_kernel(r_ref, k_ref, v_ref, lw_ref, a_ref, kk_ref, ka_ref, rk_ref, gg_ref, gb_ref,
                       y_ref, sfin_ref, s_scr, *, n_batch, n_pairs, c, n_sub):
    ci = pl.program_id(0)

    @pl.when(ci == 0)
    def _():
        s_scr[...] = jnp.zeros_like(s_scr)

    seq_pairs = [(b, p) for b in range(n_batch) for p in range(n_pairs)]
    chains = [(j, b, p) for j in range(n_sub) for b, p in seq_pairs]
    data = [[ref[b, j * c:(j + 1) * c, _pair_lanes(p)] for j, b, p in chains]
            for ref in (r_ref, k_ref, v_ref, lw_ref, a_ref)]
    prm = [[ref[:, _pair_lanes(p)] for _, _, p in chains] for ref in (kk_ref, ka_ref, rk_ref)]
    prep = _wkv_prepare(*data, *prm, 1, c)
    gn = [[ref[:, _pair_lanes(p)] for _, p in seq_pairs] for ref in (gg_ref, gb_ref)]
    states = [[s_scr[i]] for i in range(len(seq_pairs))]
    for j in range(n_sub):
        sel = [j * len(seq_pairs) + i for i in range(len(seq_pairs))]
        yn, states = _wkv_apply(prep, sel, states, *gn, 1, c)
        for i, (b, p) in enumerate(seq_pairs):
            y_ref[b, j * c:(j + 1) * c, _pair_lanes(p)] = yn[i]
    for i in range(len(seq_pairs)):
        s_scr[i] = states[i][0]

    @pl.when(ci == pl.num_programs(0) - 1)
    def _():
        for i, (b, p) in enumerate(seq_pairs):
            s = s_scr[i]
            sfin_ref[b, 2 * p] = s[:RWKV_HEAD, :RWKV_HEAD]
            sfin_ref[b, 2 * p + 1] = s[RWKV_HEAD:, RWKV_HEAD:]


def _wkv_sample_kernel(r_ref, k_ref, v_ref, lw_ref, a_ref, kk_ref, ka_ref, rk_ref, gg_ref, gb_ref,
                       s0_ref, y_ref, sfin_ref, *, n_pairs, nb, c):
    pairs = range(n_pairs)
    args = [[ref[:, _pair_lanes(p)] for p in pairs]
            for ref in (r_ref, k_ref, v_ref, lw_ref, a_ref, kk_ref, ka_ref, rk_ref)]
    prep = _wkv_prepare(*args, nb, c)
    states = [[_state_to_blockdiag(s0_ref[b, 2 * p], s0_ref[b, 2 * p + 1]) for b in range(nb)] for p in pairs]
    gn = [[ref[:, _pair_lanes(p)] for p in pairs] for ref in (gg_ref, gb_ref)]
    yn, new = _wkv_apply(prep, list(pairs), states, *gn, nb, c)
    for p in pairs:
        y_ref[:, _pair_lanes(p)] = yn[p]
        for b in range(nb):
            sfin_ref[b, 2 * p] = new[p][b][:RWKV_HEAD, :RWKV_HEAD]
            sfin_ref[b, 2 * p + 1] = new[p][b][RWKV_HEAD:, RWKV_HEAD:]


def _wkv_prompt(r, k, v, lw, a, k_k, k_a, r_k, gn_g, gn_b, batch, seq):
    rows, d = r.shape
    heads = d // RWKV_HEAD
    n_pairs = heads // HEADS_PER_VREG
    c = PROMPT_CHUNK
    n_sub = PROMPT_CHUNKS_PER_STEP
    row_spec = pl.BlockSpec((batch, n_sub * c, d), lambda i: (0, i, 0))
    vec = _const_spec((1, d))
    kern = functools.partial(_wkv_prompt_kernel, n_batch=batch, n_pairs=n_pairs, c=c, n_sub=n_sub)
    seq3 = lambda t: t.reshape(batch, seq, d)
    y, s_fin = pl.pallas_call(
        kern,
        grid=(seq // (n_sub * c),),
        in_specs=[row_spec] * 5 + [vec] * 5,
        out_specs=[row_spec, pl.BlockSpec((batch, heads, RWKV_HEAD, RWKV_HEAD), lambda i: (0, 0, 0, 0))],
        out_shape=[jax.ShapeDtypeStruct((batch, seq, d), F32),
                   jax.ShapeDtypeStruct((batch, heads, RWKV_HEAD, RWKV_HEAD), F32)],
        scratch_shapes=[pltpu.VMEM((batch * n_pairs, V7X_LANES, V7X_LANES), F32)],
        compiler_params=_cparams("arbitrary"),
        name="wkv_prompt",
    )(seq3(r), seq3(k), seq3(v), seq3(lw), seq3(a), k_k, k_a, r_k, gn_g, gn_b)
    return y.reshape(rows, d), s_fin


def _wkv_sample(r, k, v, lw, a, k_k, k_a, r_k, gn_g, gn_b, state, batch, seq):
    rows, d = r.shape
    heads = d // RWKV_HEAD
    nb = WKV_ROWS // seq
    row_spec = pl.BlockSpec((WKV_ROWS, d), lambda i: (i, 0))
    st_spec = pl.BlockSpec((nb, heads, RWKV_HEAD, RWKV_HEAD), lambda i: (i, 0, 0, 0))
    vec = _const_spec((1, d))
    kern = functools.partial(_wkv_sample_kernel, n_pairs=heads // HEADS_PER_VREG, nb=nb, c=seq)
    return pl.pallas_call(
        kern,
        grid=(batch // nb,),
        in_specs=[row_spec] * 5 + [vec] * 5 + [st_spec],
        out_specs=[row_spec, st_spec],
        out_shape=[jax.ShapeDtypeStruct((rows, d), F32),
                   jax.ShapeDtypeStruct((batch, heads, RWKV_HEAD, RWKV_HEAD), F32)],
        compiler_params=_cparams("parallel"),
        name="wkv_sample",
    )(r, k, v, lw, a, k_k, k_a, r_k, gn_g, gn_b, state)


def _branch_out(y_ref, g_ref, x_ref, w_ref, lng_ref, lnb_ref):
    g = g_ref[...]
    h = (y_ref[...] * (g / (1.0 + jnp.exp(-g)))).astype(BF16)
    z = ALPHA * x_ref[...] + jnp.dot(h, w_ref[...], preferred_element_type=F32)
    mu = jnp.mean(z, axis=-1, keepdims=True)
    zc = z - mu
    var = jnp.mean(zc * zc, axis=-1, keepdims=True)
    return zc * lax.rsqrt(var + LN_EPS) * lng_ref[...] + lnb_ref[...]


def _post_kernel(y_ref, g_ref, x_ref, w_ref, lng_ref, lnb_ref, o_ref):
    o_ref[...] = _branch_out(y_ref, g_ref, x_ref, w_ref, lng_ref, lnb_ref)


def _post_project_kernel(y_ref, g_ref, x_ref, w_ref, lng_ref, lnb_ref, win_ref,
                         o_ref, q_ref, k_ref, v_ref, gate_ref, *, hq, hkv):
    x1 = _branch_out(y_ref, g_ref, x_ref, w_ref, lng_ref, lnb_ref)
    o_ref[...] = x1
    z = jnp.dot(x1.astype(BF16), win_ref[...], preferred_element_type=F32)
    q_ref[...] = (z[:, :hq] * (ATT_SCALE * LOG2E)).astype(q_ref.dtype)
    k_ref[...] = z[:, hq:hq + hkv]
    v_ref[...] = z[:, hq + hkv:hq + 2 * hkv]
    gate_ref[...] = z[:, hq + 2 * hkv:]


def _post(y, g, x2, w_out, ln_g, ln_b, tm, project=None):
    rows, d = x2.shape
    din = y.shape[1]
    in_spec = pl.BlockSpec((tm, din), lambda i: (i, 0))
    row_spec = pl.BlockSpec((tm, d), lambda i: (i, 0))
    in_specs = [in_spec, in_spec, row_spec, _const_spec((din, d)), _const_spec((1, d)), _const_spec((1, d))]
    args = (y, g, x2, w_out.astype(BF16), ln_g.reshape(1, d), ln_b.reshape(1, d))
    x_shape = jax.ShapeDtypeStruct((rows, d), F32)
    if project is None:
        return pl.pallas_call(
            _post_kernel, grid=(rows // tm,), in_specs=in_specs, out_specs=row_spec, out_shape=x_shape,
            compiler_params=_cparams("parallel"), name="branch_post",
        )(*args)
    w_in, q_dtype = project
    hq = ATT_KV_HEADS * ATT_GROUP * ATT_HEAD_DIM
    hkv = ATT_KV_HEADS * ATT_HEAD_DIM
    wide = pl.BlockSpec((tm, hq), lambda i: (i, 0))
    narrow = pl.BlockSpec((tm, hkv), lambda i: (i, 0))
    return pl.pallas_call(
        functools.partial(_post_project_kernel, hq=hq, hkv=hkv),
        grid=(rows // tm,),
        in_specs=in_specs + [_const_spec(w_in.shape)],
        out_specs=[row_spec, wide, narrow, narrow, wide],
        out_shape=[x_shape, jax.ShapeDtypeStruct((rows, hq), q_dtype), jax.ShapeDtypeStruct((rows, hkv), F32),
                   jax.ShapeDtypeStruct((rows, hkv), F32), jax.ShapeDtypeStruct((rows, hq), F32)],
        compiler_params=_cparams("parallel"),
        name="branch_post_project",
    )(*args, w_in.astype(BF16))


def _alibi_slope(h):
    n_heads = ATT_KV_HEADS * ATT_GROUP
    return LOG2E * 2.0 ** (-8.0 * (h + 1) / n_heads)


def _fold_masks(q_time):
    col = lax.broadcasted_iota(jnp.int32, q_time.shape, 1)
    from_prev = col > q_time
    dist = jnp.where(from_prev, q_time + WINDOW - col, q_time - col).astype(F32)
    return from_prev, dist


def _softmax_folded(sp, sc, slopes, sinks, from_prev, dist, dead):
    idx = range(len(sp))
    s = [jnp.where(from_prev, sp[i], sc[i]) - slopes[i] * dist for i in idx]
    if dead is not None:
        s = [jnp.where(dead, NEG, s[i]) for i in idx]
    m = [jnp.maximum(jnp.max(s[i], axis=-1, keepdims=True), sinks[i]) for i in idx]
    p = [jnp.exp2(s[i] - m[i]) for i in idx]
    den = [jnp.sum(p[i], axis=-1, keepdims=True) + jnp.exp2(sinks[i] - m[i]) for i in idx]
    pp = [jnp.where(from_prev, p[i], 0.0).astype(BF16) for i in idx]
    pc = [jnp.where(from_prev, 0.0, p[i]).astype(BF16) for i in idx]
    return pp, pc, den


_NT = (((1,), (1,)), ((), ()))


def _att_prompt_kernel(sink_ref, q_ref, kc_ref, vc_ref, kp_ref, vp_ref, gate_ref, x_ref, w_ref, lng_ref, lnb_ref,
                       y_ref, o_scr, *, n_acc):
    blk = pl.program_id(1)
    from_prev, dist = _fold_masks(lax.broadcasted_iota(jnp.int32, (WINDOW, WINDOW), 0))
    dead = from_prev & (blk == 0)
    hd = ATT_HEAD_DIM
    heads = range(ATT_KV_HEADS * ATT_GROUP)
    kcat, vcat = [[jnp.concatenate([prev[:, kvh * hd:(kvh + 1) * hd], cur[:, kvh * hd:(kvh + 1) * hd]],
                                   axis=0).astype(BF16) for kvh in range(ATT_KV_HEADS)]
                  for prev, cur in ((kp_ref, kc_ref), (vp_ref, vc_ref))]
    s2 = [lax.dot_general(q_ref[:, h * hd:(h + 1) * hd], kcat[h // ATT_GROUP], _NT, preferred_element_type=F32)
          for h in heads]
    pp, pc, den = _softmax_folded([s2[h][:, :WINDOW] for h in heads], [s2[h][:, WINDOW:] for h in heads],
                                  [_alibi_slope(h) for h in heads], [sink_ref[h] * LOG2E for h in heads],
                                  from_prev, dist, dead)
    row0 = pl.multiple_of((blk % n_acc) * WINDOW, WINDOW)
    for h in heads:
        o = jnp.dot(jnp.concatenate([pp[h], pc[h]], axis=1), vcat[h // ATT_GROUP], preferred_element_type=F32)
        o_scr[pl.ds(row0, WINDOW), h * hd:(h + 1) * hd] = o / den[h]

    @pl.when(blk % n_acc == n_acc - 1)
    def _():
        y_ref[...] = _branch_out(o_scr, gate_ref, x_ref, w_ref, lng_ref, lnb_ref)


def _att_prompt(q, k, v, gate, x2, sinks, w_out, ln_g, ln_b, batch, seq, tm):
    rows, hq = q.shape
    hkv = k.shape[1]
    d = x2.shape[1]
    nb = seq // WINDOW
    n_acc = tm // WINDOW
    cur = lambda b, n: (b * nb + n, 0)
    prev = lambda b, n: (b * nb + jnp.maximum(n - 1, 0), 0)
    tile = lambda b, n: ((b * nb + n) // n_acc, 0)
    return pl.pallas_call(
        functools.partial(_att_prompt_kernel, n_acc=n_acc),
        grid=(batch, nb),
        in_specs=[pl.BlockSpec(memory_space=pltpu.SMEM),
                  pl.BlockSpec((WINDOW, hq), cur),
                  pl.BlockSpec((WINDOW, hkv), cur), pl.BlockSpec((WINDOW, hkv), cur),
                  pl.BlockSpec((WINDOW, hkv), prev), pl.BlockSpec((WINDOW, hkv), prev),
                  pl.BlockSpec((tm, hq), tile), pl.BlockSpec((tm, d), tile),
                  _const_spec((hq, d)), _const_spec((1, d)), _const_spec((1, d))],
        out_specs=pl.BlockSpec((tm, d), tile),
        out_shape=jax.ShapeDtypeStruct((rows, d), F32),
        scratch_shapes=[pltpu.VMEM((tm, hq), F32)],
        compiler_params=_cparams("parallel", "arbitrary"),
        name="att_prompt",
    )(sinks, q, k, v, k, v, gate, x2, w_out.astype(BF16), ln_g.reshape(1, d), ln_b.reshape(1, d))


def _att_sample_kernel(sink_ref, q_ref, kn_ref, vn_ref, ckt_ref, cvt_ref, o_ref, kwt_ref, vwt_ref, *, nb, t):
    m = ATT_GROUP * t
    grp = lax.broadcasted_iota(jnp.int32, (m, 1), 0) // t
    from_prev, dist = _fold_masks(lax.broadcasted_iota(jnp.int32, (m, WINDOW), 0) % t)
    hd = ATT_HEAD_DIM
    pad = jnp.zeros((WINDOW - t, hd), F32)
    items = [(kvh, b) for kvh in range(ATT_KV_HEADS) for b in range(nb)]
    idx = range(len(items))
    slopes, sinks = [], []
    for kvh in range(ATT_KV_HEADS):
        slope = jnp.zeros((m, 1), F32)
        sink = jnp.zeros((m, 1), F32)
        for g in range(ATT_GROUP):
            h = kvh * ATT_GROUP + g
            slope = jnp.where(grp == g, _alibi_slope(h), slope)
            sink = jnp.where(grp == g, sink_ref[h] * LOG2E, sink)
        slopes += [slope] * nb
        sinks += [sink] * nb
    qs = [jnp.concatenate([q_ref[b * t:(b + 1) * t, (kvh * ATT_GROUP + g) * hd:(kvh * ATT_GROUP + g + 1) * hd]
                           for g in range(ATT_GROUP)], axis=0).astype(BF16) for kvh, b in items]
    k_new = [kn_ref[b * t:(b + 1) * t, kvh * hd:(kvh + 1) * hd] for kvh, b in items]
    v_new = [vn_ref[b * t:(b + 1) * t, kvh * hd:(kvh + 1) * hd] for kvh, b in items]
    kt_old = [ckt_ref[b, kvh] for kvh, b in items]
    vt_old = [cvt_ref[b, kvh] for kvh, b in items]

    sp = [jnp.dot(qs[i], kt_old[i].astype(BF16), preferred_element_type=F32) for i in idx]
    sc = [lax.dot_general(qs[i], jnp.concatenate([k_new[i], pad], axis=0).astype(BF16), _NT,
                          preferred_element_type=F32) for i in idx]
    pp, pc, den = _softmax_folded(sp, sc, slopes, sinks, from_prev, dist, None)
    o = [(lax.dot_general(pp[i], vt_old[i].astype(BF16), _NT, preferred_element_type=F32)
          + jnp.dot(pc[i], jnp.concatenate([v_new[i], pad], axis=0).astype(BF16), preferred_element_type=F32))
         / den[i] for i in idx]
    for i, (kvh, b) in enumerate(items):
        for g in range(ATT_GROUP):
            h = kvh * ATT_GROUP + g
            o_ref[b * t:(b + 1) * t, h * hd:(h + 1) * hd] = o[i][g * t:(g + 1) * t]

    lane = lax.broadcasted_iota(jnp.int32, (hd, WINDOW), 1)
    for new, old, out_ref in ((k_new, kt_old, kwt_ref), (v_new, vt_old, vwt_ref)):
        new_t = [jnp.concatenate([pad, new[i]], axis=0).T for i in idx]
        for i, (kvh, b) in enumerate(items):
            out_ref[b, kvh] = jnp.where(lane >= WINDOW - t, new_t[i], pltpu.roll(old[i], WINDOW - t, axis=1))


def _att_sample(q, k, v, cache_kt, cache_vt, sinks, batch, t, nb):
    rows, hq = q.shape
    hkv = k.shape[1]
    row = lambda i: (i, 0)
    cache_spec = pl.BlockSpec((nb,) + cache_kt.shape[1:], lambda i: (i, 0, 0, 0))
    cache_shape = jax.ShapeDtypeStruct(cache_kt.shape, F32)
    return pl.pallas_call(
        functools.partial(_att_sample_kernel, nb=nb, t=t),
        grid=(batch // nb,),
        in_specs=[pl.BlockSpec(memory_space=pltpu.SMEM),
                  pl.BlockSpec((nb * t, hq), row), pl.BlockSpec((nb * t, hkv), row),
                  pl.BlockSpec((nb * t, hkv), row), cache_spec, cache_spec],
        out_specs=[pl.BlockSpec((nb * t, hq), row), cache_spec, cache_spec],
        out_shape=[jax.ShapeDtypeStruct((rows, hq), F32), cache_shape, cache_shape],
        compiler_params=_cparams("parallel"),
        name="att_sample",
    )(sinks, q, k, v, cache_kt, cache_vt)


ROW_TILE = 512


def _row_tile(rows):
    return ROW_TILE if rows % ROW_TILE == 0 else rows


def kernel(x_prompt, x_sample, state_wkv, state_shift, cache_win_k, cache_win_v, ln_g, ln_b, rwkv_mu, rwkv_w_in, rwkv_w0, rwkv_w1, rwkv_w2, rwkv_a0, rwkv_a1, rwkv_a2, rwkv_k_k, rwkv_k_a, rwkv_r_k, rwkv_gn_g, rwkv_gn_b, rwkv_w_out, att_w_in, att_sinks, att_w_out):
    bp, tp, d = x_prompt.shape
    bs, ts, _ = x_sample.shape
    assert tp % (PROMPT_CHUNK * PROMPT_CHUNKS_PER_STEP) == 0 and tp % ROW_TILE == 0 and ROW_TILE % WINDOW == 0 and WKV_ROWS % ts == 0 and bs % (WKV_ROWS // ts) == 0
    assert cache_win_k.shape[2] == WINDOW
    heads = d // RWKV_HEAD
    hkv = ATT_KV_HEADS * ATT_HEAD_DIM

    j = 0
    vec = lambda t: t.reshape(1, d)
    pre_w = (rwkv_mu[j], rwkv_w_in[j], rwkv_w0[j], rwkv_w1[j], rwkv_w2[j], rwkv_a0[j], rwkv_a1[j], rwkv_a2[j])
    head_w = (vec(rwkv_k_k[j]), vec(rwkv_k_a[j]), vec(rwkv_r_k[j]), vec(rwkv_gn_g[j]), vec(rwkv_gn_b[j]))

    xp2 = x_prompt.reshape(bp * tp, d)
    xs2 = x_sample.reshape(bs * ts, d)

    r, k, v, g, lw, a = _rwkv_pre(x_prompt, jnp.zeros((bp, d), F32), *pre_w, tm=_row_tile(bp * tp))
    yp, wkv_p = _wkv_prompt(r, k, v, lw, a, *head_w, batch=bp, seq=tp)
    xp1, q_p, k_p, v_p, gate_p = _post(yp, g, xp2, rwkv_w_out[j], ln_g[0], ln_b[0], tm=_row_tile(bp * tp),
                                       project=(att_w_in[j], BF16))

    r, k, v, g, lw, a = _rwkv_pre(x_sample, state_shift[j], *pre_w, tm=_row_tile(bs * ts))
    ys, wkv_s = _wkv_sample(r, k, v, lw, a, *head_w, state_wkv[j], batch=bs, seq=ts)
    xs1, q_s, k_s, v_s, gate_s = _post(ys, g, xs2, rwkv_w_out[j], ln_g[0], ln_b[0], tm=_row_tile(bs * ts),
                                       project=(att_w_in[j], F32))

    y_prompt = _att_prompt(q_p, k_p, v_p, gate_p, xp1, att_sinks[j], att_w_out[j], ln_g[1], ln_b[1],
                           batch=bp, seq=tp, tm=ROW_TILE).reshape(bp, tp, d)
    win_shape = (bp, WINDOW, ATT_KV_HEADS, ATT_HEAD_DIM)
    win_k_p = k_p.reshape(bp, tp, hkv)[:, tp - WINDOW:].reshape(win_shape)
    win_v_p = v_p.reshape(bp, tp, hkv)[:, tp - WINDOW:].reshape(win_shape)

    to_t = lambda c: jnp.transpose(c, (0, 2, 3, 1))
    o, win_k_s, win_v_s = _att_sample(q_s, k_s, v_s, to_t(cache_win_k[j]), to_t(cache_win_v[j]), att_sinks[j],
                                      batch=bs, t=ts, nb=WKV_ROWS // ts)
    y_sample = _post(o, gate_s, xs1, att_w_out[j], ln_g[1], ln_b[1], tm=_row_tile(bs * ts)).reshape(bs, ts, d)
    win_k_s = jnp.transpose(win_k_s, (0, 3, 1, 2))
    win_v_s = jnp.transpose(win_v_s, (0, 3, 1, 2))

    return (y_prompt, y_sample,
            wkv_p[None], x_prompt[:, -1][None], win_k_p[None], win_v_p[None],
            wkv_s[None], x_sample[:, -1][None], win_k_s[None], win_v_s[None])
```

```python
import functools

import jax
import jax.numpy as jnp
from jax import lax
from jax.experimental import pallas as pl
from jax.experimental.pallas import tpu as pltpu

F32 = jnp.float32
BF16 = jnp.bfloat16

RWKV_HEAD = 64
ATT_HEAD_DIM = 64
ATT_KV_HEADS = 4
ATT_GROUP = 4
WINDOW = 128
ATT_SCALE = ATT_HEAD_DIM ** -0.5
LOG2E = 1.4426950408889634
NEG = -1e30
DEPTH = 2
ALPHA = (2 * DEPTH) ** 0.25
LN_EPS = 1e-5
GN_EPS = 64e-5

V7X_LANES = 128
V7X_SUBLANES = 8
V7X_VMEM_LIMIT_BYTES = 56 * 1024 * 1024

HEADS_PER_VREG = V7X_LANES // RWKV_HEAD
STACK_ROWS = 128
WKV_ROWS = STACK_ROWS // HEADS_PER_VREG
PROMPT_CHUNK = WKV_ROWS
PROMPT_CHUNKS_PER_STEP = 4


def _cparams(*sem):
    return pltpu.CompilerParams(dimension_semantics=sem, vmem_limit_bytes=V7X_VMEM_LIMIT_BYTES)


def _const_spec(shape):
    nd = len(shape)
    return pl.BlockSpec(shape, lambda *_: (0,) * nd, pipeline_mode=pl.Buffered(1))


def _rwkv_pre_kernel(x_ref, first_ref, halo_ref, mu_ref, win_ref, w0_ref, w1_ref, w2_ref, a0_ref, a1_ref, a2_ref,
                     r_ref, k_ref, v_ref, g_ref, lw_ref, a_ref, *, tiles_per_seq):
    x3 = x_ref[...]
    nseq, rows, d = x3.shape
    first = first_ref[...]
    if tiles_per_seq > 1:
        first = jnp.where(pl.program_id(0) % tiles_per_seq == 0, first, halo_ref[:, -1:, :])
    t_idx = lax.broadcasted_iota(jnp.int32, (1, rows, 1), 1)
    x_prev = jnp.where(t_idx == 0, first, pltpu.roll(x3, 1, axis=1))
    x = x3.reshape(nseq * rows, d)
    xx = x_prev.reshape(nseq * rows, d) - x

    def mix(p):
        return (x + xx * mu_ref[p:p + 1, :]).astype(BF16)

    def proj(p):
        return jnp.dot(mix(p), win_ref[p], preferred_element_type=F32)

    r_ref[...] = proj(0)
    k_ref[...] = proj(1)
    v_ref[...] = proj(2)
    g_ref[...] = proj(3)
    hw = jnp.tanh(jnp.dot(mix(4), w1_ref[...], preferred_element_type=F32))
    wl = w0_ref[...] + jnp.dot(hw.astype(BF16), w2_ref[...], preferred_element_type=F32)
    z = -wl
    softplus = jnp.maximum(z, 0.0) + jnp.log(1.0 + jnp.exp(-jnp.abs(z)))
    lw_ref[...] = -jnp.exp(-softplus - 0.5)
    ha = jnp.dot(mix(5), a1_ref[...], preferred_element_type=F32)
    al = a0_ref[...] + jnp.dot(ha.astype(BF16), a2_ref[...], preferred_element_type=F32)
    a_ref[...] = 1.0 / (1.0 + jnp.exp(-al))


def _rwkv_pre(x3, first, mu, w_in, w0, w1, w2, a0, a1, a2, tm):
    nseq, seq, d = x3.shape
    rows = nseq * seq
    lora = w1.shape[1]
    if seq >= tm:
        tps = seq // tm
        x_spec = pl.BlockSpec((1, tm, d), lambda i: (i // tps, i % tps, 0))
        first_spec = pl.BlockSpec((1, 1, d), lambda i: (i // tps, 0, 0))
        halo_spec = pl.BlockSpec(
            (1, V7X_SUBLANES, d),
            lambda i: (i // tps, jnp.maximum((i % tps) * (tm // V7X_SUBLANES) - 1, 0), 0))
    else:
        tps = 1
        x_spec = pl.BlockSpec((tm // seq, seq, d), lambda i: (i, 0, 0))
        first_spec = pl.BlockSpec((tm // seq, 1, d), lambda i: (i, 0, 0))
        halo_spec = pl.BlockSpec((1, V7X_SUBLANES, d), lambda i: (0, 0, 0))
    row_spec = pl.BlockSpec((tm, d), lambda i: (i, 0))
    out = jax.ShapeDtypeStruct((rows, d), F32)
    return pl.pallas_call(
        functools.partial(_rwkv_pre_kernel, tiles_per_seq=tps),
        grid=(rows // tm,),
        in_specs=[x_spec, first_spec, halo_spec, _const_spec((6, d)), _const_spec((4, d, d)),
                  _const_spec((1, d)), _const_spec((d, lora)), _const_spec((lora, d)), _const_spec((1, d)),
                  _const_spec((d, lora)), _const_spec((lora, d))],
        out_specs=[row_spec] * 6,
        out_shape=[out] * 6,
        compiler_params=_cparams("parallel"),
        name="rwkv_pre",
    )(x3, first.reshape(nseq, 1, d), x3, mu, w_in.astype(BF16), w0.reshape(1, d), w1.astype(BF16),
      w2.astype(BF16), a0.reshape(1, d), a1.astype(BF16), a2.astype(BF16))


def _mm(a, b, dims):
    return lax.dot_general(a.astype(BF16), b.astype(BF16), (dims, ((), ())), preferred_element_type=F32)


def _mm_nn(a, b):
    return _mm(a, b, ((1,), (0,)))


def _mm_nt(a, b):
    return _mm(a, b, ((1,), (1,)))


def _mm_tn(a, b):
    return _mm(a, b, ((0,), (0,)))


def _wkv_prepare(r, k, v, lw, a, k_k, k_a, r_k, nb, c):
    chains = range(len(r))
    rows = nb * c
    n = STACK_ROWS
    lane = lax.broadcasted_iota(jnp.int32, (1, V7X_LANES), 1)
    head0 = lane < RWKV_HEAD
    mk = (jnp.where(head0, 1.0, 0.0).astype(F32), jnp.where(head0, 0.0, 1.0).astype(F32))
    col_head0 = (lane // c) % HEADS_PER_VREG == 0
    cm = (jnp.where(col_head0, 1.0, 0.0).astype(F32), jnp.where(col_head0, 0.0, 1.0).astype(F32))

    def head_sum(t):
        s0 = jnp.sum(t * mk[0], axis=-1, keepdims=True)
        s1 = jnp.sum(t * mk[1], axis=-1, keepdims=True)
        return jnp.where(head0, s0, s1)

    def stack(t, masks):
        if c % (2 * V7X_SUBLANES) == 0:
            t = t.astype(BF16)
            masks = [m.astype(BF16) for m in masks]
        parts = []
        for b in range(nb):
            tb = t[b * c:(b + 1) * c]
            parts += [tb * masks[0], tb * masks[1]]
        return jnp.concatenate(parts, axis=0).astype(BF16)

    ri = lax.broadcasted_iota(jnp.int32, (rows, rows), 0)
    ci = lax.broadcasted_iota(jnp.int32, (rows, rows), 1)
    tri = jnp.where((ri // c == ci // c) & (ri >= ci), 1.0, 0.0).astype(BF16)
    lw_hi = [lw[p].astype(BF16) for p in chains]
    cum = [_mm_nn(tri, lw_hi[p]) + _mm_nn(tri, lw[p] - lw_hi[p].astype(F32)) for p in chains]

    kkr = [k[p] * k_k[p] for p in chains]
    kk = [kkr[p] / jnp.maximum(jnp.sqrt(head_sum(kkr[p] * kkr[p])), 1e-12) for p in chains]
    kf = [k[p] * (1.0 + (a[p] - 1.0) * k_a[p]) for p in chains]
    g_incl = [jnp.exp(cum[p]) for p in chains]
    g_inv = [jnp.exp(-cum[p]) for p in chains]
    at = [(-kk[p] * jnp.exp(cum[p] - lw[p])).astype(BF16) for p in chains]
    rt = [(r[p] * g_incl[p]).astype(BF16) for p in chains]
    bt = [(kk[p] * a[p] * g_inv[p]).astype(BF16) for p in chains]
    kt = [(kf[p] * g_inv[p]).astype(BF16) for p in chains]
    vb = [v[p].astype(BF16) for p in chains]
    xa = [jnp.concatenate([at[p], rt[p]], axis=0) for p in chains]
    yb = [jnp.concatenate([stack(bt[p], mk), stack(kt[p], mk)], axis=0) for p in chains]
    v_s = [stack(vb[p], mk) for p in chains]

    qi = lax.broadcasted_iota(jnp.int32, (rows, n), 0)
    qj = lax.broadcasted_iota(jnp.int32, (rows, n), 1)
    same = qi // c == qj // (HEADS_PER_VREG * c)
    strict = same & (qj % c < qi % c)
    incl = same & (qj % c <= qi % c)
    eye = jnp.where(same & (qj % c == qi % c), 1.0, 0.0).astype(F32)

    gm = [_mm_nt(xa[p], yb[p]) for p in chains]
    a_ab = [jnp.where(strict, gm[p][:rows, :n], 0.0) for p in chains]
    a_kr = [jnp.concatenate([jnp.where(strict, gm[p][:rows, n:], 0.0),
                             jnp.where(incl, gm[p][rows:, n:], 0.0)], axis=0).astype(BF16) for p in chains]
    a_rb = [jnp.where(incl, gm[p][rows:, :n], 0.0).astype(BF16) for p in chains]

    tinv = [eye + a_ab[p] for p in chains]
    pw = [a_ab[p] for p in chains]
    if c > 2:
        pw = [_mm_nn(pw[p], stack(pw[p], cm)) for p in chains]
        span = 4
        while span < c:
            sq = [_mm_nn(pw[p], jnp.concatenate([stack(pw[p], cm), stack(tinv[p], cm)], axis=1)) for p in chains]
            tinv = [tinv[p] + sq[p][:, n:] for p in chains]
            pw = [sq[p][:, :n] for p in chains]
            span *= 2
        tinv = [tinv[p] + _mm_nn(pw[p], stack(tinv[p], cm)) for p in chains]

    av = [_mm_nn(a_kr[p], v_s[p]) for p in chains]
    bonus = [head_sum(r[p] * kf[p] * r_k[p]) * v[p] for p in chains]
    g_last = [[g_incl[p][(b + 1) * c - 1:(b + 1) * c] for b in range(nb)] for p in chains]
    return dict(xa=xa, at=at, rt=rt, bt=bt, kt=kt, vb=vb, tinv=[t.astype(BF16) for t in tinv], av=av, a_rb=a_rb,
                bonus=bonus, g_last=g_last, head_sum=head_sum, stack=lambda t: stack(t, mk))


def _wkv_apply(prep, sel, states, gn_g, gn_b, nb, c):
    rows = nb * c
    xa, at, rt, bt, kt, vb = (prep[key] for key in ("xa", "at", "rt", "bt", "kt", "vb"))
    stack = prep["stack"]
    if nb == 1:
        ah = [_mm_nt(xa[p], states[i][0]) for i, p in enumerate(sel)]
    else:
        ah = []
        for i, p in enumerate(sel):
            parts = [_mm_nt(jnp.concatenate([at[p][b * c:(b + 1) * c], rt[p][b * c:(b + 1) * c]], axis=0), states[i][b])
                     for b in range(nb)]
            ah.append(jnp.concatenate([t[:c] for t in parts] + [t[c:] for t in parts], axis=0))

    u = [_mm_nn(prep["tinv"][p], stack(ah[i][:rows] + prep["av"][p][:rows])) for i, p in enumerate(sel)]
    y = [ah[i][rows:] + prep["av"][p][rows:] + _mm_nn(prep["a_rb"][p], stack(u[i])) for i, p in enumerate(sel)]

    vi = lax.broadcasted_iota(jnp.int32, (V7X_LANES, V7X_LANES), 0)
    ki = lax.broadcasted_iota(jnp.int32, (V7X_LANES, V7X_LANES), 1)
    same_head = vi // RWKV_HEAD == ki // RWKV_HEAD
    new_states = []
    for i, p in enumerate(sel):
        ub = u[i].astype(BF16)
        per_batch = []
        for b in range(nb):
            sl = slice(b * c, (b + 1) * c)
            upd = _mm_tn(jnp.concatenate([ub[sl], vb[p][sl]], axis=0), jnp.concatenate([bt[p][sl], kt[p][sl]], axis=0))
            per_batch.append((states[i][b] + jnp.where(same_head, upd, 0.0)) * prep["g_last"][p][b])
        new_states.append(per_batch)

    head_sum = prep["head_sum"]
    inv_n = 1.0 / RWKV_HEAD
    out = []
    for i, p in enumerate(sel):
        mean = head_sum(y[i]) * inv_n
        yc = y[i] - mean
        var = head_sum(yc * yc) * inv_n
        out.append(yc * lax.rsqrt(var + GN_EPS) * gn_g[i] + gn_b[i] + prep["bonus"][p])
    return out, new_states


def _pair_lanes(p):
    return slice(p * V7X_LANES, (p + 1) * V7X_LANES)


def _state_to_blockdiag(s0, s1):
    z = jnp.zeros_like(s0)
    return jnp.concatenate([jnp.concatenate([s0, z], axis=1), jnp.concatenate([z, s1], axis=1)], axis=0)


def _wkv_prompt_kernel(r_ref, k_ref, v_ref, lw_ref, a_ref, kk_ref, ka_ref, rk_ref, gg_ref, gb_ref,
                       y_ref, sfin_ref, s_scr, *, n_batch, n_pairs, c, n_sub):
    ci = pl.program_id(0)

    @pl.when(ci == 0)
    def _():
        s_scr[...] = jnp.zeros_like(s_scr)

    seq_pairs = [(b, p) for b in range(n_batch) for p in range(n_pairs)]
    chains = [(j, b, p) for j in range(n_sub) for b, p in seq_pairs]
    data = [[ref[b, j * c:(j + 1) * c, _pair_lanes(p)] for j, b, p in chains]
            for ref in (r_ref, k_ref, v_ref, lw_ref, a_ref)]
    prm = [[ref[:, _pair_lanes(p)] for _, _, p in chains] for ref in (kk_ref, ka_ref, rk_ref)]
    prep = _wkv_prepare(*data, *prm, 1, c)
    gn = [[ref[:, _pair_lanes(p)] for _, p in seq_pairs] for ref in (gg_ref, gb_ref)]
    states = [[s_scr[i]] for i in range(len(seq_pairs))]
    for j in range(n_sub):
        sel = [j * len(seq_pairs) + i for i in range(len(seq_pairs))]
        yn, states = _wkv_apply(prep, sel, states, *gn, 1, c)
        for i, (b, p) in enumerate(seq_pairs):
            y_ref[b, j * c:(j + 1) * c, _pair_lanes(p)] = yn[i]
    for i in range(len(seq_pairs)):
        s_scr[i] = states[i][0]

    @pl.when(ci == pl.num_programs(0) - 1)
    def _():
        for i, (b, p) in enumerate(seq_pairs):
            s = s_scr[i]
            sfin_ref[b, 2 * p] = s[:RWKV_HEAD, :RWKV_HEAD]
            sfin_ref[b, 2 * p + 1] = s[RWKV_HEAD:, RWKV_HEAD:]


def _wkv_sample_kernel(r_ref, k_ref, v_ref, lw_ref, a_ref, kk_ref, ka_ref, rk_ref, gg_ref, gb_ref,
                       s0_ref, y_ref, sfin_ref, *, n_pairs, nb, c):
    pairs = range(n_pairs)
    args = [[ref[:, _pair_lanes(p)] for p in pairs]
            for ref in (r_ref, k_ref, v_ref, lw_ref, a_ref, kk_ref, ka_ref, rk_ref)]
    prep = _wkv_prepare(*args, nb, c)
    states = [[_state_to_blockdiag(s0_ref[b, 2 * p], s0_ref[b, 2 * p + 1]) for b in range(nb)] for p in pairs]
    gn = [[ref[:, _pair_lanes(p)] for p in pairs] for ref in (gg_ref, gb_ref)]
    yn, new = _wkv_apply(prep, list(pairs), states, *gn, nb, c)
    for p in pairs:
        y_ref[:, _pair_lanes(p)] = yn[p]
        for b in range(nb):
            sfin_ref[b, 2 * p] = new[p][b][:RWKV_HEAD, :RWKV_HEAD]
            sfin_ref[b, 2 * p + 1] = new[p][b][RWKV_HEAD:, RWKV_HEAD:]


def _wkv_prompt(r, k, v, lw, a, k_k, k_a, r_k, gn_g, gn_b, batch, seq):
    rows, d = r.shape
    heads = d // RWKV_HEAD
    n_pairs = heads // HEADS_PER_VREG
    c = PROMPT_CHUNK
    n_sub = PROMPT_CHUNKS_PER_STEP
    row_spec = pl.BlockSpec((batch, n_sub * c, d), lambda i: (0, i, 0))
    vec = _const_spec((1, d))
    kern = functools.partial(_wkv_prompt_kernel, n_batch=batch, n_pairs=n_pairs, c=c, n_sub=n_sub)
    seq3 = lambda t: t.reshape(batch, seq, d)
    y, s_fin = pl.pallas_call(
        kern,
        grid=(seq // (n_sub * c),),
        in_specs=[row_spec] * 5 + [vec] * 5,
        out_specs=[row_spec, pl.BlockSpec((batch, heads, RWKV_HEAD, RWKV_HEAD), lambda i: (0, 0, 0, 0))],
        out_shape=[jax.ShapeDtypeStruct((batch, seq, d), F32),
                   jax.ShapeDtypeStruct((batch, heads, RWKV_HEAD, RWKV_HEAD), F32)],
        scratch_shapes=[pltpu.VMEM((batch * n_pairs, V7X_LANES, V7X_LANES), F32)],
        compiler_params=_cparams("arbitrary"),
        name="wkv_prompt",
    )(seq3(r), seq3(k), seq3(v), seq3(lw), seq3(a), k_k, k_a, r_k, gn_g, gn_b)
    return y.reshape(rows, d), s_fin


def _wkv_sample(r, k, v, lw, a, k_k, k_a, r_k, gn_g, gn_b, state, batch, seq):
    rows, d = r.shape
    heads = d // RWKV_HEAD
    nb = WKV_ROWS // seq
    row_spec = pl.BlockSpec((WKV_ROWS, d), lambda i: (i, 0))
    st_spec = pl.BlockSpec((nb, heads, RWKV_HEAD, RWKV_HEAD), lambda i: (i, 0, 0, 0))
    vec = _const_spec((1, d))
    kern = functools.partial(_wkv_sample_kernel, n_pairs=heads // HEADS_PER_VREG, nb=nb, c=seq)
    return pl.pallas_call(
        kern,
        grid=(batch // nb,),
        in_specs=[row_spec] * 5 + [vec] * 5 + [st_spec],
        out_specs=[row_spec, st_spec],
        out_shape=[jax.ShapeDtypeStruct((rows, d), F32),
                   jax.ShapeDtypeStruct((batch, heads, RWKV_HEAD, RWKV_HEAD), F32)],
        compiler_params=_cparams("parallel"),
        name="wkv_sample",
    )(r, k, v, lw, a, k_k, k_a, r_k, gn_g, gn_b, state)


def _gated(y, g):
    return (y * (g / (1.0 + jnp.exp(-g)))).astype(BF16)


def _residual_norm(h, x, w_ref, lng_ref, lnb_ref):
    z = ALPHA * x + jnp.dot(h, w_ref[...], preferred_element_type=F32)
    mu = jnp.mean(z, axis=-1, keepdims=True)
    zc = z - mu
    var = jnp.mean(zc * zc, axis=-1, keepdims=True)
    return zc * lax.rsqrt(var + LN_EPS) * lng_ref[...] + lnb_ref[...]


def _branch_out(y_ref, g_ref, x_ref, w_ref, lng_ref, lnb_ref):
    return _residual_norm(_gated(y_ref[...], g_ref[...]), x_ref[...], w_ref, lng_ref, lnb_ref)


def _post_kernel(y_ref, g_ref, x_ref, w_ref, lng_ref, lnb_ref, o_ref):
    o_ref[...] = _branch_out(y_ref, g_ref, x_ref, w_ref, lng_ref, lnb_ref)


def _post_project_kernel(y_ref, g_ref, x_ref, w_ref, lng_ref, lnb_ref, win_ref,
                         o_ref, q_ref, k_ref, v_ref, gate_ref, *, hq, hkv):
    x1 = _branch_out(y_ref, g_ref, x_ref, w_ref, lng_ref, lnb_ref)
    o_ref[...] = x1
    z = jnp.dot(x1.astype(BF16), win_ref[...], preferred_element_type=F32)
    q_ref[...] = (z[:, :hq] * (ATT_SCALE * LOG2E)).astype(q_ref.dtype)
    k_ref[...] = z[:, hq:hq + hkv]
    v_ref[...] = z[:, hq + hkv:hq + 2 * hkv]
    gate_ref[...] = z[:, hq + 2 * hkv:]


def _post(y, g, x2, w_out, ln_g, ln_b, tm, project=None):
    rows, d = x2.shape
    din = y.shape[1]
    in_spec = pl.BlockSpec((tm, din), lambda i: (i, 0))
    row_spec = pl.BlockSpec((tm, d), lambda i: (i, 0))
    in_specs = [in_spec, in_spec, row_spec, _const_spec((din, d)), _const_spec((1, d)), _const_spec((1, d))]
    args = (y, g, x2, w_out.astype(BF16), ln_g.reshape(1, d), ln_b.reshape(1, d))
    x_shape = jax.ShapeDtypeStruct((rows, d), F32)
    if project is None:
        return pl.pallas_call(
            _post_kernel, grid=(rows // tm,), in_specs=in_specs, out_specs=row_spec, out_shape=x_shape,
            compiler_params=_cparams("parallel"), name="branch_post",
        )(*args)
    w_in, q_dtype = project
    hq = ATT_KV_HEADS * ATT_GROUP * ATT_HEAD_DIM
    hkv = ATT_KV_HEADS * ATT_HEAD_DIM
    wide = pl.BlockSpec((tm, hq), lambda i: (i, 0))
    narrow = pl.BlockSpec((tm, hkv), lambda i: (i, 0))
    return pl.pallas_call(
        functools.partial(_post_project_kernel, hq=hq, hkv=hkv),
        grid=(rows // tm,),
        in_specs=in_specs + [_const_spec(w_in.shape)],
        out_specs=[row_spec, wide, narrow, narrow, wide],
        out_shape=[x_shape, jax.ShapeDtypeStruct((rows, hq), q_dtype), jax.ShapeDtypeStruct((rows, hkv), F32),
                   jax.ShapeDtypeStruct((rows, hkv), F32), jax.ShapeDtypeStruct((rows, hq), F32)],
        compiler_params=_cparams("parallel"),
        name="branch_post_project",
    )(*args, w_in.astype(BF16))


def _alibi_slope(h):
    n_heads = ATT_KV_HEADS * ATT_GROUP
    return LOG2E * 2.0 ** (-8.0 * (h + 1) / n_heads)


def _fold_masks(q_time):
    col = lax.broadcasted_iota(jnp.int32, q_time.shape, 1)
    from_prev = col > q_time
    dist = jnp.where(from_prev, q_time + WINDOW - col, q_time - col).astype(F32)
    return from_prev, dist


def _softmax_folded(sp, sc, slopes, sinks, from_prev, dist, dead):
    idx = range(len(sp))
    s = [jnp.where(from_prev, sp[i], sc[i]) - slopes[i] * dist for i in idx]
    if dead is not None:
        s = [jnp.where(dead, NEG, s[i]) for i in idx]
    m = [jnp.maximum(jnp.max(s[i], axis=-1, keepdims=True), sinks[i]) for i in idx]
    p = [jnp.exp2(s[i] - m[i]) for i in idx]
    den = [jnp.sum(p[i], axis=-1, keepdims=True) + jnp.exp2(sinks[i] - m[i]) for i in idx]
    pp = [jnp.where(from_prev, p[i], 0.0).astype(BF16) for i in idx]
    pc = [jnp.where(from_prev, 0.0, p[i]).astype(BF16) for i in idx]
    return pp, pc, den


_NT = (((1,), (1,)), ((), ()))


def _att_prompt_kernel(sink_ref, q_ref, kc_ref, vc_ref, kp_ref, vp_ref, gate_ref, x_ref, w_ref, lng_ref, lnb_ref,
                       y_ref, o_scr, h_scr, x_scr, *, n_acc):
    blk = pl.program_id(1)
    from_prev, dist = _fold_masks(lax.broadcasted_iota(jnp.int32, (WINDOW, WINDOW), 0))
    dead = from_prev & (blk == 0)
    hd = ATT_HEAD_DIM
    heads = range(ATT_KV_HEADS * ATT_GROUP)
    kcat, vcat = [[jnp.concatenate([prev[:, kvh * hd:(kvh + 1) * hd], cur[:, kvh * hd:(kvh + 1) * hd]],
                                   axis=0).astype(BF16) for kvh in range(ATT_KV_HEADS)]
                  for prev, cur in ((kp_ref, kc_ref), (vp_ref, vc_ref))]
    s2 = [lax.dot_general(q_ref[:, h * hd:(h + 1) * hd], kcat[h // ATT_GROUP], _NT, preferred_element_type=F32)
          for h in heads]
    pp, pc, den = _softmax_folded([s2[h][:, :WINDOW] for h in heads], [s2[h][:, WINDOW:] for h in heads],
                                  [_alibi_slope(h) for h in heads], [sink_ref[h] * LOG2E for h in heads],
                                  from_prev, dist, dead)
    for h in heads:
        o = jnp.dot(jnp.concatenate([pp[h], pc[h]], axis=1), vcat[h // ATT_GROUP], preferred_element_type=F32)
        o_scr[:, h * hd:(h + 1) * hd] = o / den[h]
    row0 = pl.multiple_of((blk % n_acc) * WINDOW, WINDOW)
    h_scr[pl.ds(row0, WINDOW), :] = _gated(o_scr[...], gate_ref[...])
    x_scr[pl.ds(row0, WINDOW), :] = x_ref[...]

    @pl.when(blk % n_acc == n_acc - 1)
    def _():
        y_ref[...] = _residual_norm(h_scr[...], x_scr[...], w_ref, lng_ref, lnb_ref)


def _att_prompt(q, k, v, gate, x2, sinks, w_out, ln_g, ln_b, batch, seq, tm):
    rows, hq = q.shape
    hkv = k.shape[1]
    d = x2.shape[1]
    nb = seq // WINDOW
    n_acc = tm // WINDOW
    cur = lambda b, n: (b * nb + n, 0)
    prev = lambda b, n: (b * nb + jnp.maximum(n - 1, 0), 0)
    tile = lambda b, n: ((b * nb + n) // n_acc, 0)
    return pl.pallas_call(
        functools.partial(_att_prompt_kernel, n_acc=n_acc),
        grid=(batch, nb),
        in_specs=[pl.BlockSpec(memory_space=pltpu.SMEM),
                  pl.BlockSpec((WINDOW, hq), cur),
                  pl.BlockSpec((WINDOW, hkv), cur), pl.BlockSpec((WINDOW, hkv), cur),
                  pl.BlockSpec((WINDOW, hkv), prev), pl.BlockSpec((WINDOW, hkv), prev),
                  pl.BlockSpec((WINDOW, hq), cur), pl.BlockSpec((WINDOW, d), cur),
                  _const_spec((hq, d)), _const_spec((1, d)), _const_spec((1, d))],
        out_specs=pl.BlockSpec((tm, d), tile),
        out_shape=jax.ShapeDtypeStruct((rows, d), F32),
        scratch_shapes=[pltpu.VMEM((WINDOW, hq), F32), pltpu.VMEM((tm, hq), BF16), pltpu.VMEM((tm, d), F32)],
        compiler_params=_cparams("parallel", "arbitrary"),
        name="att_prompt",
    )(sinks, q, k, v, k, v, gate, x2, w_out.astype(BF16), ln_g.reshape(1, d), ln_b.reshape(1, d))


def _att_sample_kernel(sink_ref, q_ref, kn_ref, vn_ref, ckt_ref, cvt_ref, o_ref, kwt_ref, vwt_ref, *, nb, t):
    m = ATT_GROUP * t
    grp = lax.broadcasted_iota(jnp.int32, (m, 1), 0) // t
    from_prev, dist = _fold_masks(lax.broadcasted_iota(jnp.int32, (m, WINDOW), 0) % t)
    hd = ATT_HEAD_DIM
    pad = jnp.zeros((WINDOW - t, hd), F32)
    items = [(kvh, b) for kvh in range(ATT_KV_HEADS) for b in range(nb)]
    idx = range(len(items))
    slopes, sinks = [], []
    for kvh in range(ATT_KV_HEADS):
        slope = jnp.zeros((m, 1), F32)
        sink = jnp.zeros((m, 1), F32)
        for g in range(ATT_GROUP):
            h = kvh * ATT_GROUP + g
            slope = jnp.where(grp == g, _alibi_slope(h), slope)
            sink = jnp.where(grp == g, sink_ref[h] * LOG2E, sink)
        slopes += [slope] * nb
        sinks += [sink] * nb
    qs = [jnp.concatenate([q_ref[b * t:(b + 1) * t, (kvh * ATT_GROUP + g) * hd:(kvh * ATT_GROUP + g + 1) * hd]
                           for g in range(ATT_GROUP)], axis=0).astype(BF16) for kvh, b in items]
    k_new = [kn_ref[b * t:(b + 1) * t, kvh * hd:(kvh + 1) * hd] for kvh, b in items]
    v_new = [vn_ref[b * t:(b + 1) * t, kvh * hd:(kvh + 1) * hd] for kvh, b in items]
    kt_old = [ckt_ref[b, kvh] for kvh, b in items]
    vt_old = [cvt_ref[b, kvh] for kvh, b in items]

    sp = [jnp.dot(qs[i], kt_old[i].astype(BF16), preferred_element_type=F32) for i in idx]
    sc = [lax.dot_general(qs[i], jnp.concatenate([k_new[i], pad], axis=0).astype(BF16), _NT,
                          preferred_element_type=F32) for i in idx]
    pp, pc, den = _softmax_folded(sp, sc, slopes, sinks, from_prev, dist, None)
    o = [(lax.dot_general(pp[i], vt_old[i].astype(BF16), _NT, preferred_element_type=F32)
          + jnp.dot(pc[i], jnp.concatenate([v_new[i], pad], axis=0).astype(BF16), preferred_element_type=F32))
         / den[i] for i in idx]
    for i, (kvh, b) in enumerate(items):
        for g in range(ATT_GROUP):
            h = kvh * ATT_GROUP + g
            o_ref[b * t:(b + 1) * t, h * hd:(h + 1) * hd] = o[i][g * t:(g + 1) * t]

    lane = lax.broadcasted_iota(jnp.int32, (hd, WINDOW), 1)
    for new, old, out_ref in ((k_new, kt_old, kwt_ref), (v_new, vt_old, vwt_ref)):
        new_t = [jnp.concatenate([pad, new[i]], axis=0).T for i in idx]
        for i, (kvh, b) in enumerate(items):
            out_ref[b, kvh] = jnp.where(lane >= WINDOW - t, new_t[i], pltpu.roll(old[i], WINDOW - t, axis=1))


def _att_sample(q, k, v, cache_kt, cache_vt, sinks, batch, t, nb):
    rows, hq = q.shape
    hkv = k.shape[1]
    row = lambda i: (i, 0)
    cache_spec = pl.BlockSpec((nb,) + cache_kt.shape[1:], lambda i: (i, 0, 0, 0))
    cache_shape = jax.ShapeDtypeStruct(cache_kt.shape, F32)
    return pl.pallas_call(
        functools.partial(_att_sample_kernel, nb=nb, t=t),
        grid=(batch // nb,),
        in_specs=[pl.BlockSpec(memory_space=pltpu.SMEM),
                  pl.BlockSpec((nb * t, hq), row), pl.BlockSpec((nb * t, hkv), row),
                  pl.BlockSpec((nb * t, hkv), row), cache_spec, cache_spec],
        out_specs=[pl.BlockSpec((nb * t, hq), row), cache_spec, cache_spec],
        out_shape=[jax.ShapeDtypeStruct((rows, hq), F32), cache_shape, cache_shape],
        compiler_params=_cparams("parallel"),
        name="att_sample",
    )(sinks, q, k, v, cache_kt, cache_vt)


ROW_TILE = 512


def _row_tile(rows):
    return ROW_TILE if rows % ROW_TILE == 0 else rows


def kernel(x_prompt, x_sample, state_wkv, state_shift, cache_win_k, cache_win_v, ln_g, ln_b, rwkv_mu, rwkv_w_in, rwkv_w0, rwkv_w1, rwkv_w2, rwkv_a0, rwkv_a1, rwkv_a2, rwkv_k_k, rwkv_k_a, rwkv_r_k, rwkv_gn_g, rwkv_gn_b, rwkv_w_out, att_w_in, att_sinks, att_w_out):
    bp, tp, d = x_prompt.shape
    bs, ts, _ = x_sample.shape
    assert tp % (PROMPT_CHUNK * PROMPT_CHUNKS_PER_STEP) == 0 and tp % ROW_TILE == 0 and ROW_TILE % WINDOW == 0 and WKV_ROWS % ts == 0 and bs % (WKV_ROWS // ts) == 0
    assert cache_win_k.shape[2] == WINDOW
    heads = d // RWKV_HEAD
    hkv = ATT_KV_HEADS * ATT_HEAD_DIM

    j = 0
    vec = lambda t: t.reshape(1, d)
    pre_w = (rwkv_mu[j], rwkv_w_in[j], rwkv_w0[j], rwkv_w1[j], rwkv_w2[j], rwkv_a0[j], rwkv_a1[j], rwkv_a2[j])
    head_w = (vec(rwkv_k_k[j]), vec(rwkv_k_a[j]), vec(rwkv_r_k[j]), vec(rwkv_gn_g[j]), vec(rwkv_gn_b[j]))

    xp2 = x_prompt.reshape(bp * tp, d)
    xs2 = x_sample.reshape(bs * ts, d)

    r, k, v, g, lw, a = _rwkv_pre(x_prompt, jnp.zeros((bp, d), F32), *pre_w, tm=_row_tile(bp * tp))
    yp, wkv_p = _wkv_prompt(r, k, v, lw, a, *head_w, batch=bp, seq=tp)
    xp1, q_p, k_p, v_p, gate_p = _post(yp, g, xp2, rwkv_w_out[j], ln_g[0], ln_b[0], tm=_row_tile(bp * tp),
                                       project=(att_w_in[j], BF16))

    r, k, v, g, lw, a = _rwkv_pre(x_sample, state_shift[j], *pre_w, tm=_row_tile(bs * ts))
    ys, wkv_s = _wkv_sample(r, k, v, lw, a, *head_w, state_wkv[j], batch=bs, seq=ts)
    xs1, q_s, k_s, v_s, gate_s = _post(ys, g, xs2, rwkv_w_out[j], ln_g[0], ln_b[0], tm=_row_tile(bs * ts),
                                       project=(att_w_in[j], F32))

    y_prompt = _att_prompt(q_p, k_p, v_p, gate_p, xp1, att_sinks[j], att_w_out[j], ln_g[1], ln_b[1],
                           batch=bp, seq=tp, tm=ROW_TILE).reshape(bp, tp, d)
    win_shape = (bp, WINDOW, ATT_KV_HEADS, ATT_HEAD_DIM)
    win_k_p = k_p.reshape(bp, tp, hkv)[:, tp - WINDOW:].reshape(win_shape)
    win_v_p = v_p.reshape(bp, tp, hkv)[:, tp - WINDOW:].reshape(win_shape)

    to_t = lambda c: jnp.transpose(c, (0, 2, 3, 1))
    o, win_k_s, win_v_s = _att_sample(q_s, k_s, v_s, to_t(cache_win_k[j]), to_t(cache_win_v[j]), att_sinks[j],
                                      batch=bs, t=ts, nb=WKV_ROWS // ts)
    y_sample = _post(o, gate_s, xs1, att_w_out[j], ln_g[1], ln_b[1], tm=_row_tile(bs * ts)).reshape(bs, ts, d)
    win_k_s = jnp.transpose(win_k_s, (0, 3, 1, 2))
    win_v_s = jnp.transpose(win_v_s, (0, 3, 1, 2))

    return (y_prompt, y_sample,
            wkv_p[None], x_prompt[:, -1][None], win_k_p[None], win_v_p[None],
            wkv_s[None], x_sample[:, -1][None], win_k_s[None], win_v_s[None])
```

```python
import functools

import jax
import jax.numpy as jnp
from jax import lax
from jax.experimental import pallas as pl
from jax.experimental.pallas import tpu as pltpu

F32 = jnp.float32
BF16 = jnp.bfloat16

RWKV_HEAD = 64
ATT_HEAD_DIM = 64
ATT_KV_HEADS = 4
ATT_GROUP = 4
WINDOW = 128
ATT_SCALE = ATT_HEAD_DIM ** -0.5
LOG2E = 1.4426950408889634
NEG = -1e30
DEPTH = 2
ALPHA = (2 * DEPTH) ** 0.25
LN_EPS = 1e-5
GN_EPS = 64e-5

V7X_LANES = 128
V7X_SUBLANES = 8
V7X_VMEM_LIMIT_BYTES = 56 * 1024 * 1024

HEADS_PER_VREG = V7X_LANES // RWKV_HEAD
STACK_ROWS = 128
WKV_ROWS = STACK_ROWS // HEADS_PER_VREG
PROMPT_CHUNK = WKV_ROWS
PROMPT_CHUNKS_PER_STEP = 4


def _cparams(*sem):
    return pltpu.CompilerParams(dimension_semantics=sem, vmem_limit_bytes=V7X_VMEM_LIMIT_BYTES)


def _const_spec(shape):
    nd = len(shape)
    return pl.BlockSpec(shape, lambda *_: (0,) * nd, pipeline_mode=pl.Buffered(1))


def _rwkv_pre_kernel(x_ref, first_ref, halo_ref, mu_ref, win_ref, w0_ref, w1_ref, w2_ref, a0_ref, a1_ref, a2_ref,
                     r_ref, k_ref, v_ref, g_ref, lw_ref, a_ref, *, tiles_per_seq):
    x3 = x_ref[...]
    nseq, rows, d = x3.shape
    first = first_ref[...]
    if tiles_per_seq > 1:
        first = jnp.where(pl.program_id(0) % tiles_per_seq == 0, first, halo_ref[:, -1:, :])
    t_idx = lax.broadcasted_iota(jnp.int32, (1, rows, 1), 1)
    x_prev = jnp.where(t_idx == 0, first, pltpu.roll(x3, 1, axis=1))
    x = x3.reshape(nseq * rows, d)
    xx = x_prev.reshape(nseq * rows, d) - x

    def mix(p):
        return (x + xx * mu_ref[p:p + 1, :]).astype(BF16)

    def proj(p):
        return jnp.dot(mix(p), win_ref[p], preferred_element_type=F32)

    r_ref[...] = proj(0)
    k_ref[...] = proj(1)
    v_ref[...] = proj(2)
    g_ref[...] = proj(3)
    hw = jnp.tanh(jnp.dot(mix(4), w1_ref[...], preferred_element_type=F32))
    wl = w0_ref[...] + jnp.dot(hw.astype(BF16), w2_ref[...], preferred_element_type=F32)
    z = -wl
    softplus = jnp.maximum(z, 0.0) + jnp.log(1.0 + jnp.exp(-jnp.abs(z)))
    lw_ref[...] = -jnp.exp(-softplus - 0.5)
    ha = jnp.dot(mix(5), a1_ref[...], preferred_element_type=F32)
    al = a0_ref[...] + jnp.dot(ha.astype(BF16), a2_ref[...], preferred_element_type=F32)
    a_ref[...] = 1.0 / (1.0 + jnp.exp(-al))


def _rwkv_pre(x3, first, mu, w_in, w0, w1, w2, a0, a1, a2, tm):
    nseq, seq, d = x3.shape
    rows = nseq * seq
    lora = w1.shape[1]
    if seq >= tm:
        tps = seq // tm
        x_spec = pl.BlockSpec((1, tm, d), lambda i: (i // tps, i % tps, 0))
        first_spec = pl.BlockSpec((1, 1, d), lambda i: (i // tps, 0, 0))
        halo_spec = pl.BlockSpec(
            (1, V7X_SUBLANES, d),
            lambda i: (i // tps, jnp.maximum((i % tps) * (tm // V7X_SUBLANES) - 1, 0), 0))
    else:
        tps = 1
        x_spec = pl.BlockSpec((tm // seq, seq, d), lambda i: (i, 0, 0))
        first_spec = pl.BlockSpec((tm // seq, 1, d), lambda i: (i, 0, 0))
        halo_spec = pl.BlockSpec((1, V7X_SUBLANES, d), lambda i: (0, 0, 0))
    row_spec = pl.BlockSpec((tm, d), lambda i: (i, 0))
    out = jax.ShapeDtypeStruct((rows, d), F32)
    return pl.pallas_call(
        functools.partial(_rwkv_pre_kernel, tiles_per_seq=tps),
        grid=(rows // tm,),
        in_specs=[x_spec, first_spec, halo_spec, _const_spec((6, d)), _const_spec((4, d, d)),
                  _const_spec((1, d)), _const_spec((d, lora)), _const_spec((lora, d)), _const_spec((1, d)),
                  _const_spec((d, lora)), _const_spec((lora, d))],
        out_specs=[row_spec] * 6,
        out_shape=[out] * 6,
        compiler_params=_cparams("parallel"),
        name="rwkv_pre",
    )(x3, first.reshape(nseq, 1, d), x3, mu, w_in.astype(BF16), w0.reshape(1, d), w1.astype(BF16),
      w2.astype(BF16), a0.reshape(1, d), a1.astype(BF16), a2.astype(BF16))


def _mm(a, b, dims):
    return lax.dot_general(a.astype(BF16), b.astype(BF16), (dims, ((), ())), preferred_element_type=F32)


def _mm_nn(a, b):
    return _mm(a, b, ((1,), (0,)))


def _mm_nt(a, b):
    return _mm(a, b, ((1,), (1,)))


def _mm_tn(a, b):
    return _mm(a, b, ((0,), (0,)))


def _wkv_prepare(r, k, v, lw, a, k_k, k_a, r_k, nb, c):
    chains = range(len(r))
    rows = nb * c
    n = STACK_ROWS
    lane = lax.broadcasted_iota(jnp.int32, (1, V7X_LANES), 1)
    head0 = lane < RWKV_HEAD
    mk = (jnp.where(head0, 1.0, 0.0).astype(F32), jnp.where(head0, 0.0, 1.0).astype(F32))
    col_head0 = (lane // c) % HEADS_PER_VREG == 0
    cm = (jnp.where(col_head0, 1.0, 0.0).astype(F32), jnp.where(col_head0, 0.0, 1.0).astype(F32))

    def head_sum(t):
        s0 = jnp.sum(t * mk[0], axis=-1, keepdims=True)
        s1 = jnp.sum(t * mk[1], axis=-1, keepdims=True)
        return jnp.where(head0, s0, s1)

    def stack(t, masks):
        if c % (2 * V7X_SUBLANES) == 0:
            t = t.astype(BF16)
            masks = [m.astype(BF16) for m in masks]
        parts = []
        for b in range(nb):
            tb = t[b * c:(b + 1) * c]
            parts += [tb * masks[0], tb * masks[1]]
        return jnp.concatenate(parts, axis=0).astype(BF16)

    ri = lax.broadcasted_iota(jnp.int32, (rows, rows), 0)
    ci = lax.broadcasted_iota(jnp.int32, (rows, rows), 1)
    tri = jnp.where((ri // c == ci // c) & (ri >= ci), 1.0, 0.0).astype(BF16)
    lw_hi = [lw[p].astype(BF16) for p in chains]
    cum = [_mm_nn(tri, lw_hi[p]) + _mm_nn(tri, lw[p] - lw_hi[p].astype(F32)) for p in chains]

    kkr = [k[p] * k_k[p] for p in chains]
    kk = [kkr[p] / jnp.maximum(jnp.sqrt(head_sum(kkr[p] * kkr[p])), 1e-12) for p in chains]
    kf = [k[p] * (1.0 + (a[p] - 1.0) * k_a[p]) for p in chains]
    g_incl = [jnp.exp(cum[p]) for p in chains]
    g_inv = [jnp.exp(-cum[p]) for p in chains]
    at = [(-kk[p] * jnp.exp(cum[p] - lw[p])).astype(BF16) for p in chains]
    rt = [(r[p] * g_incl[p]).astype(BF16) for p in chains]
    bt = [(kk[p] * a[p] * g_inv[p]).astype(BF16) for p in chains]
    kt = [(kf[p] * g_inv[p]).astype(BF16) for p in chains]
    vb = [v[p].astype(BF16) for p in chains]
    xa = [jnp.concatenate([at[p], rt[p]], axis=0) for p in chains]
    yb = [jnp.concatenate([stack(bt[p], mk), stack(kt[p], mk)], axis=0) for p in chains]
    v_s = [stack(vb[p], mk) for p in chains]

    qi = lax.broadcasted_iota(jnp.int32, (rows, n), 0)
    qj = lax.broadcasted_iota(jnp.int32, (rows, n), 1)
    same = qi // c == qj // (HEADS_PER_VREG * c)
    strict = same & (qj % c < qi % c)
    incl = same & (qj % c <= qi % c)
    eye = jnp.where(same & (qj % c == qi % c), 1.0, 0.0).astype(F32)

    gm = [_mm_nt(xa[p], yb[p]) for p in chains]
    a_ab = [jnp.where(strict, gm[p][:rows, :n], 0.0) for p in chains]
    a_kr = [jnp.concatenate([jnp.where(strict, gm[p][:rows, n:], 0.0),
                             jnp.where(incl, gm[p][rows:, n:], 0.0)], axis=0).astype(BF16) for p in chains]
    a_rb = [jnp.where(incl, gm[p][rows:, :n], 0.0).astype(BF16) for p in chains]

    tinv = [eye + a_ab[p] for p in chains]
    pw = [a_ab[p] for p in chains]
    if c > 2:
        pw = [_mm_nn(pw[p], stack(pw[p], cm)) for p in chains]
        span = 4
        while span < c:
            sq = [_mm_nn(pw[p], jnp.concatenate([stack(pw[p], cm), stack(tinv[p], cm)], axis=1)) for p in chains]
            tinv = [tinv[p] + sq[p][:, n:] for p in chains]
            pw = [sq[p][:, :n] for p in chains]
            span *= 2
        tinv = [tinv[p] + _mm_nn(pw[p], stack(tinv[p], cm)) for p in chains]

    av = [_mm_nn(a_kr[p], v_s[p]) for p in chains]
    bonus = [head_sum(r[p] * kf[p] * r_k[p]) * v[p] for p in chains]
    g_last = [[g_incl[p][(b + 1) * c - 1:(b + 1) * c] for b in range(nb)] for p in chains]
    return dict(xa=xa, at=at, rt=rt, bt=bt, kt=kt, vb=vb, tinv=[t.astype(BF16) for t in tinv], av=av, a_rb=a_rb,
                bonus=bonus, g_last=g_last, head_sum=head_sum, stack=lambda t: stack(t, mk))


def _wkv_apply(prep, sel, states, gn_g, gn_b, nb, c):
    rows = nb * c
    xa, at, rt, bt, kt, vb = (prep[key] for key in ("xa", "at", "rt", "bt", "kt", "vb"))
    stack = prep["stack"]
    if nb == 1:
        ah = [_mm_nt(xa[p], states[i][0]) for i, p in enumerate(sel)]
    else:
        ah = []
        for i, p in enumerate(sel):
            parts = [_mm_nt(jnp.concatenate([at[p][b * c:(b + 1) * c], rt[p][b * c:(b + 1) * c]], axis=0), states[i][b])
                     for b in range(nb)]
            ah.append(jnp.concatenate([t[:c] for t in parts] + [t[c:] for t in parts], axis=0))

    u = [_mm_nn(prep["tinv"][p], stack(ah[i][:rows] + prep["av"][p][:rows])) for i, p in enumerate(sel)]
    y = [ah[i][rows:] + prep["av"][p][rows:] + _mm_nn(prep["a_rb"][p], stack(u[i])) for i, p in enumerate(sel)]

    vi = lax.broadcasted_iota(jnp.int32, (V7X_LANES, V7X_LANES), 0)
    ki = lax.broadcasted_iota(jnp.int32, (V7X_LANES, V7X_LANES), 1)
    same_head = vi // RWKV_HEAD == ki // RWKV_HEAD
    new_states = []
    for i, p in enumerate(sel):
        ub = u[i].astype(BF16)
        per_batch = []
        for b in range(nb):
            sl = slice(b * c, (b + 1) * c)
            upd = _mm_tn(jnp.concatenate([ub[sl], vb[p][sl]], axis=0), jnp.concatenate([bt[p][sl], kt[p][sl]], axis=0))
            per_batch.append((states[i][b] + jnp.where(same_head, upd, 0.0)) * prep["g_last"][p][b])
        new_states.append(per_batch)

    head_sum = prep["head_sum"]
    inv_n = 1.0 / RWKV_HEAD
    out = []
    for i, p in enumerate(sel):
        mean = head_sum(y[i]) * inv_n
        yc = y[i] - mean
        var = head_sum(yc * yc) * inv_n
        out.append(yc * lax.rsqrt(var + GN_EPS) * gn_g[i] + gn_b[i] + prep["bonus"][p])
    return out, new_states


def _pair_lanes(p):
    return slice(p * V7X_LANES, (p + 1) * V7X_LANES)


def _state_to_blockdiag(s0, s1):
    z = jnp.zeros_like(s0)
    return jnp.concatenate([jnp.concatenate([s0, z], axis=1), jnp.concatenate([z, s1], axis=1)], axis=0)


def _wkv_prompt_kernel(r_ref, k_ref, v_ref, lw_ref, a_ref, kk_ref, ka_ref, rk_ref, gg_ref, gb_ref,
                       y_ref, sfin_ref, s_scr, *, n_batch, n_pairs, c, n_sub):
    ci = pl.program_id(0)

    @pl.when(ci == 0)
    def _():
        s_scr[...] = jnp.zeros_like(s_scr)

    seq_pairs = [(b, p) for b in range(n_batch) for p in range(n_pairs)]
    chains = [(j, b, p) for j in range(n_sub) for b, p in seq_pairs]
    data = [[ref[b, j * c:(j + 1) * c, _pair_lanes(p)] for j, b, p in chains]
            for ref in (r_ref, k_ref, v_ref, lw_ref, a_ref)]
    prm = [[ref[:, _pair_lanes(p)] for _, _, p in chains] for ref in (kk_ref, ka_ref, rk_ref)]
    prep = _wkv_prepare(*data, *prm, 1, c)
    gn = [[ref[:, _pair_lanes(p)] for _, p in seq_pairs] for ref in (gg_ref, gb_ref)]
    states = [[s_scr[i]] for i in range(len(seq_pairs))]
    for j in range(n_sub):
        sel = [j * len(seq_pairs) + i for i in range(len(seq_pairs))]
        yn, states = _wkv_apply(prep, sel, states, *gn, 1, c)
        for i, (b, p) in enumerate(seq_pairs):
            y_ref[b, j * c:(j + 1) * c, _pair_lanes(p)] = yn[i]
    for i in range(len(seq_pairs)):
        s_scr[i] = states[i][0]

    @pl.when(ci == pl.num_programs(0) - 1)
    def _():
        for i, (b, p) in enumerate(seq_pairs):
            s = s_scr[i]
            sfin_ref[b, 2 * p] = s[:RWKV_HEAD, :RWKV_HEAD]
            sfin_ref[b, 2 * p + 1] = s[RWKV_HEAD:, RWKV_HEAD:]


def _wkv_sample_kernel(r_ref, k_ref, v_ref, lw_ref, a_ref, kk_ref, ka_ref, rk_ref, gg_ref, gb_ref,
                       s0_ref, y_ref, sfin_ref, *, n_pairs, nb, c):
    pairs = range(n_pairs)
    args = [[ref[:, _pair_lanes(p)] for p in pairs]
            for ref in (r_ref, k_ref, v_ref, lw_ref, a_ref, kk_ref, ka_ref, rk_ref)]
    prep = _wkv_prepare(*args, nb, c)
    states = [[_state_to_blockdiag(s0_ref[b, 2 * p], s0_ref[b, 2 * p + 1]) for b in range(nb)] for p in pairs]
    gn = [[ref[:, _pair_lanes(p)] for p in pairs] for ref in (gg_ref, gb_ref)]
    yn, new = _wkv_apply(prep, list(pairs), states, *gn, nb, c)
    for p in pairs:
        y_ref[:, _pair_lanes(p)] = yn[p]
        for b in range(nb):
            sfin_ref[b, 2 * p] = new[p][b][:RWKV_HEAD, :RWKV_HEAD]
            sfin_ref[b, 2 * p + 1] = new[p][b][RWKV_HEAD:, RWKV_HEAD:]


def _wkv_prompt(r, k, v, lw, a, k_k, k_a, r_k, gn_g, gn_b, batch, seq):
    rows, d = r.shape
    heads = d // RWKV_HEAD
    n_pairs = heads // HEADS_PER_VREG
    c = PROMPT_CHUNK
    n_sub = PROMPT_CHUNKS_PER_STEP
    row_spec = pl.BlockSpec((batch, n_sub * c, d), lambda i: (0, i, 0))
    vec = _const_spec((1, d))
    kern = functools.partial(_wkv_prompt_kernel, n_batch=batch, n_pairs=n_pairs, c=c, n_sub=n_sub)
    seq3 = lambda t: t.reshape(batch, seq, d)
    y, s_fin = pl.pallas_call(
        kern,
        grid=(seq // (n_sub * c),),
        in_specs=[row_spec] * 5 + [vec] * 5,
        out_specs=[row_spec, pl.BlockSpec((batch, heads, RWKV_HEAD, RWKV_HEAD), lambda i: (0, 0, 0, 0))],
        out_shape=[jax.ShapeDtypeStruct((batch, seq, d), F32),
                   jax.ShapeDtypeStruct((batch, heads, RWKV_HEAD, RWKV_HEAD), F32)],
        scratch_shapes=[pltpu.VMEM((batch * n_pairs, V7X_LANES, V7X_LANES), F32)],
        compiler_params=_cparams("arbitrary"),
        name="wkv_prompt",
    )(seq3(r), seq3(k), seq3(v), seq3(lw), seq3(a), k_k, k_a, r_k, gn_g, gn_b)
    return y.reshape(rows, d), s_fin


def _wkv_sample(r, k, v, lw, a, k_k, k_a, r_k, gn_g, gn_b, state, batch, seq):
    rows, d = r.shape
    heads = d // RWKV_HEAD
    nb = WKV_ROWS // seq
    row_spec = pl.BlockSpec((WKV_ROWS, d), lambda i: (i, 0))
    st_spec = pl.BlockSpec((nb, heads, RWKV_HEAD, RWKV_HEAD), lambda i: (i, 0, 0, 0))
    vec = _const_spec((1, d))
    kern = functools.partial(_wkv_sample_kernel, n_pairs=heads // HEADS_PER_VREG, nb=nb, c=seq)
    return pl.pallas_call(
        kern,
        grid=(batch // nb,),
        in_specs=[row_spec] * 5 + [vec] * 5 + [st_spec],
        out_specs=[row_spec, st_spec],
        out_shape=[jax.ShapeDtypeStruct((rows, d), F32),
                   jax.ShapeDtypeStruct((batch, heads, RWKV_HEAD, RWKV_HEAD), F32)],
        compiler_params=_cparams("parallel"),
        name="wkv_sample",
    )(r, k, v, lw, a, k_k, k_a, r_k, gn_g, gn_b, state)


def _branch_out(y_ref, g_ref, x_ref, w_ref, lng_ref, lnb_ref):
    g = g_ref[...]
    h = (y_ref[...] * (g / (1.0 + jnp.exp(-g)))).astype(BF16)
    z = ALPHA * x_ref[...] + jnp.dot(h, w_ref[...], preferred_element_type=F32)
    mu = jnp.mean(z, axis=-1, keepdims=True)
    zc = z - mu
    var = jnp.mean(zc * zc, axis=-1, keepdims=True)
    return zc * lax.rsqrt(var + LN_EPS) * lng_ref[...] + lnb_ref[...]


def _post_kernel(y_ref, g_ref, x_ref, w_ref, lng_ref, lnb_ref, o_ref):
    o_ref[...] = _branch_out(y_ref, g_ref, x_ref, w_ref, lng_ref, lnb_ref)


def _post_project_kernel(y_ref, g_ref, x_ref, w_ref, lng_ref, lnb_ref, win_ref,
                         o_ref, q_ref, k_ref, v_ref, gate_ref, *, hq, hkv):
    x1 = _branch_out(y_ref, g_ref, x_ref, w_ref, lng_ref, lnb_ref)
    o_ref[...] = x1
    z = jnp.dot(x1.astype(BF16), win_ref[...], preferred_element_type=F32)
    q_ref[...] = (z[:, :hq] * (ATT_SCALE * LOG2E)).astype(q_ref.dtype)
    k_ref[...] = z[:, hq:hq + hkv]
    v_ref[...] = z[:, hq + hkv:hq + 2 * hkv]
    gate_ref[...] = z[:, hq + 2 * hkv:]


def _post(y, g, x2, w_out, ln_g, ln_b, tm, project=None):
    rows, d = x2.shape
    din = y.shape[1]
    in_spec = pl.BlockSpec((tm, din), lambda i: (i, 0))
    row_spec = pl.BlockSpec((tm, d), lambda i: (i, 0))
    in_specs = [in_spec, in_spec, row_spec, _const_spec((din, d)), _const_spec((1, d)), _const_spec((1, d))]
    args = (y, g, x2, w_out.astype(BF16), ln_g.reshape(1, d), ln_b.reshape(1, d))
    x_shape = jax.ShapeDtypeStruct((rows, d), F32)
    if project is None:
        return pl.pallas_call(
            _post_kernel, grid=(rows // tm,), in_specs=in_specs, out_specs=row_spec, out_shape=x_shape,
            compiler_params=_cparams("parallel"), name="branch_post",
        )(*args)
    w_in, q_dtype = project
    hq = ATT_KV_HEADS * ATT_GROUP * ATT_HEAD_DIM
    hkv = ATT_KV_HEADS * ATT_HEAD_DIM
    wide = pl.BlockSpec((tm, hq), lambda i: (i, 0))
    narrow = pl.BlockSpec((tm, hkv), lambda i: (i, 0))
    return pl.pallas_call(
        functools.partial(_post_project_kernel, hq=hq, hkv=hkv),
        grid=(rows // tm,),
        in_specs=in_specs + [_const_spec(w_in.shape)],
        out_specs=[row_spec, wide, narrow, narrow, wide],
        out_shape=[x_shape, jax.ShapeDtypeStruct((rows, hq), q_dtype), jax.ShapeDtypeStruct((rows, hkv), F32),
                   jax.ShapeDtypeStruct((rows, hkv), F32), jax.ShapeDtypeStruct((rows, hq), F32)],
        compiler_params=_cparams("parallel"),
        name="branch_post_project",
    )(*args, w_in.astype(BF16))


def _alibi_slope(h):
    n_heads = ATT_KV_HEADS * ATT_GROUP
    return LOG2E * 2.0 ** (-8.0 * (h + 1) / n_heads)


def _fold_masks(q_time):
    col = lax.broadcasted_iota(jnp.int32, q_time.shape, 1)
    from_prev = col > q_time
    dist = jnp.where(from_prev, q_time + WINDOW - col, q_time - col).astype(F32)
    return from_prev, dist


def _softmax_folded(sp, sc, slopes, sinks, from_prev, dist, dead):
    idx = range(len(sp))
    s = [jnp.where(from_prev, sp[i], sc[i]) - slopes[i] * dist for i in idx]
    if dead is not None:
        s = [jnp.where(dead, NEG, s[i]) for i in idx]
    m = [jnp.maximum(jnp.max(s[i], axis=-1, keepdims=True), sinks[i]) for i in idx]
    p = [jnp.exp2(s[i] - m[i]) for i in idx]
    den = [jnp.sum(p[i], axis=-1, keepdims=True) + jnp.exp2(sinks[i] - m[i]) for i in idx]
    pp = [jnp.where(from_prev, p[i], 0.0).astype(BF16) for i in idx]
    pc = [jnp.where(from_prev, 0.0, p[i]).astype(BF16) for i in idx]
    return pp, pc, den


_NT = (((1,), (1,)), ((), ()))


def _att_prompt_kernel(sink_ref, q_ref, kc_ref, vc_ref, kp_ref, vp_ref, gate_ref, x_ref, w_ref, lng_ref, lnb_ref,
                       y_ref, o_scr, *, n_blk, n_acc):
    step = pl.program_id(1)
    from_prev, dist = _fold_masks(lax.broadcasted_iota(jnp.int32, (WINDOW, WINDOW), 0))
    hd = ATT_HEAD_DIM
    heads = range(ATT_KV_HEADS * ATT_GROUP)
    slopes = [_alibi_slope(h) for h in heads]
    sinks = [sink_ref[h] * LOG2E for h in heads]
    row0 = pl.multiple_of((step % n_acc) * (n_blk * WINDOW), n_blk * WINDOW)
    for blk in range(n_blk):
        rows = slice(blk * WINDOW, (blk + 1) * WINDOW)
        prev_of = (lambda ref_p, ref_c: ref_p) if blk == 0 else \
            (lambda ref_p, ref_c: ref_c.at[(blk - 1) * WINDOW:blk * WINDOW])
        kcat, vcat = [[jnp.concatenate([prev_of(prev, cur)[:, kvh * hd:(kvh + 1) * hd],
                                        cur[rows, kvh * hd:(kvh + 1) * hd]], axis=0).astype(BF16)
                       for kvh in range(ATT_KV_HEADS)] for prev, cur in ((kp_ref, kc_ref), (vp_ref, vc_ref))]
        s2 = [lax.dot_general(q_ref[rows, h * hd:(h + 1) * hd], kcat[h // ATT_GROUP], _NT,
                              preferred_element_type=F32) for h in heads]
        dead = from_prev & (step == 0) if blk == 0 else None
        pp, pc, den = _softmax_folded([s2[h][:, :WINDOW] for h in heads], [s2[h][:, WINDOW:] for h in heads],
                                      slopes, sinks, from_prev, dist, dead)
        for h in heads:
            o = jnp.dot(jnp.concatenate([pp[h], pc[h]], axis=1), vcat[h // ATT_GROUP], preferred_element_type=F32)
            o_scr[pl.ds(row0 + blk * WINDOW, WINDOW), h * hd:(h + 1) * hd] = o / den[h]

    @pl.when(step % n_acc == n_acc - 1)
    def _():
        y_ref[...] = _branch_out(o_scr, gate_ref, x_ref, w_ref, lng_ref, lnb_ref)


def _att_prompt(q, k, v, gate, x2, sinks, w_out, ln_g, ln_b, batch, seq, tm):
    rows, hq = q.shape
    hkv = k.shape[1]
    d = x2.shape[1]
    n_blk = ATT_BLOCKS_PER_STEP
    steps = seq // (n_blk * WINDOW)
    n_acc = tm // (n_blk * WINDOW)
    cur = lambda b, n: (b * steps + n, 0)
    prev = lambda b, n: ((b * steps + n) * n_blk - jnp.minimum(n, 1), 0)
    tile = lambda b, n: ((b * steps + n) // n_acc, 0)
    return pl.pallas_call(
        functools.partial(_att_prompt_kernel, n_blk=n_blk, n_acc=n_acc),
        grid=(batch, steps),
        in_specs=[pl.BlockSpec(memory_space=pltpu.SMEM),
                  pl.BlockSpec((n_blk * WINDOW, hq), cur),
                  pl.BlockSpec((n_blk * WINDOW, hkv), cur), pl.BlockSpec((n_blk * WINDOW, hkv), cur),
                  pl.BlockSpec((WINDOW, hkv), prev), pl.BlockSpec((WINDOW, hkv), prev),
                  pl.BlockSpec((tm, hq), tile), pl.BlockSpec((tm, d), tile),
                  _const_spec((hq, d)), _const_spec((1, d)), _const_spec((1, d))],
        out_specs=pl.BlockSpec((tm, d), tile),
        out_shape=jax.ShapeDtypeStruct((rows, d), F32),
        scratch_shapes=[pltpu.VMEM((tm, hq), F32)],
        compiler_params=_cparams("parallel", "arbitrary"),
        name="att_prompt",
    )(sinks, q, k, v, k, v, gate, x2, w_out.astype(BF16), ln_g.reshape(1, d), ln_b.reshape(1, d))


def _att_sample_kernel(sink_ref, q_ref, kn_ref, vn_ref, ckt_ref, cvt_ref, o_ref, kwt_ref, vwt_ref, *, nb, t):
    m = ATT_GROUP * t
    grp = lax.broadcasted_iota(jnp.int32, (m, 1), 0) // t
    from_prev, dist = _fold_masks(lax.broadcasted_iota(jnp.int32, (m, WINDOW), 0) % t)
    hd = ATT_HEAD_DIM
    pad = jnp.zeros((WINDOW - t, hd), F32)
    items = [(kvh, b) for kvh in range(ATT_KV_HEADS) for b in range(nb)]
    idx = range(len(items))
    slopes, sinks = [], []
    for kvh in range(ATT_KV_HEADS):
        slope = jnp.zeros((m, 1), F32)
        sink = jnp.zeros((m, 1), F32)
        for g in range(ATT_GROUP):
            h = kvh * ATT_GROUP + g
            slope = jnp.where(grp == g, _alibi_slope(h), slope)
            sink = jnp.where(grp == g, sink_ref[h] * LOG2E, sink)
        slopes += [slope] * nb
        sinks += [sink] * nb
    qs = [jnp.concatenate([q_ref[b * t:(b + 1) * t, (kvh * ATT_GROUP + g) * hd:(kvh * ATT_GROUP + g + 1) * hd]
                           for g in range(ATT_GROUP)], axis=0).astype(BF16) for kvh, b in items]
    k_new = [kn_ref[b * t:(b + 1) * t, kvh * hd:(kvh + 1) * hd] for kvh, b in items]
    v_new = [vn_ref[b * t:(b + 1) * t, kvh * hd:(kvh + 1) * hd] for kvh, b in items]
    kt_old = [ckt_ref[b, kvh] for kvh, b in items]
    vt_old = [cvt_ref[b, kvh] for kvh, b in items]

    sp = [jnp.dot(qs[i], kt_old[i].astype(BF16), preferred_element_type=F32) for i in idx]
    sc = [lax.dot_general(qs[i], jnp.concatenate([k_new[i], pad], axis=0).astype(BF16), _NT,
                          preferred_element_type=F32) for i in idx]
    pp, pc, den = _softmax_folded(sp, sc, slopes, sinks, from_prev, dist, None)
    o = [(lax.dot_general(pp[i], vt_old[i].astype(BF16), _NT, preferred_element_type=F32)
          + jnp.dot(pc[i], jnp.concatenate([v_new[i], pad], axis=0).astype(BF16), preferred_element_type=F32))
         / den[i] for i in idx]
    for i, (kvh, b) in enumerate(items):
        for g in range(ATT_GROUP):
            h = kvh * ATT_GROUP + g
            o_ref[b * t:(b + 1) * t, h * hd:(h + 1) * hd] = o[i][g * t:(g + 1) * t]

    lane = lax.broadcasted_iota(jnp.int32, (hd, WINDOW), 1)
    for new, old, out_ref in ((k_new, kt_old, kwt_ref), (v_new, vt_old, vwt_ref)):
        new_t = [jnp.concatenate([pad, new[i]], axis=0).T for i in idx]
        for i, (kvh, b) in enumerate(items):
            out_ref[b, kvh] = jnp.where(lane >= WINDOW - t, new_t[i], pltpu.roll(old[i], WINDOW - t, axis=1))


def _att_sample(q, k, v, cache_kt, cache_vt, sinks, batch, t, nb):
    rows, hq = q.shape
    hkv = k.shape[1]
    row = lambda i: (i, 0)
    cache_spec = pl.BlockSpec((nb,) + cache_kt.shape[1:], lambda i: (i, 0, 0, 0))
    cache_shape = jax.ShapeDtypeStruct(cache_kt.shape, F32)
    return pl.pallas_call(
        functools.partial(_att_sample_kernel, nb=nb, t=t),
        grid=(batch // nb,),
        in_specs=[pl.BlockSpec(memory_space=pltpu.SMEM),
                  pl.BlockSpec((nb * t, hq), row), pl.BlockSpec((nb * t, hkv), row),
                  pl.BlockSpec((nb * t, hkv), row), cache_spec, cache_spec],
        out_specs=[pl.BlockSpec((nb * t, hq), row), cache_spec, cache_spec],
        out_shape=[jax.ShapeDtypeStruct((rows, hq), F32), cache_shape, cache_shape],
        compiler_params=_cparams("parallel"),
        name="att_sample",
    )(sinks, q, k, v, cache_kt, cache_vt)


ROW_TILE = 512
ATT_BLOCKS_PER_STEP = 2


def _row_tile(rows):
    return ROW_TILE if rows % ROW_TILE == 0 else rows


def kernel(x_prompt, x_sample, state_wkv, state_shift, cache_win_k, cache_win_v, ln_g, ln_b, rwkv_mu, rwkv_w_in, rwkv_w0, rwkv_w1, rwkv_w2, rwkv_a0, rwkv_a1, rwkv_a2, rwkv_k_k, rwkv_k_a, rwkv_r_k, rwkv_gn_g, rwkv_gn_b, rwkv_w_out, att_w_in, att_sinks, att_w_out):
    bp, tp, d = x_prompt.shape
    bs, ts, _ = x_sample.shape
    assert tp % (PROMPT_CHUNK * PROMPT_CHUNKS_PER_STEP) == 0 and tp % ROW_TILE == 0 and ROW_TILE % (ATT_BLOCKS_PER_STEP * WINDOW) == 0 and WKV_ROWS % ts == 0 and bs % (WKV_ROWS // ts) == 0
    assert cache_win_k.shape[2] == WINDOW
    heads = d // RWKV_HEAD
    hkv = ATT_KV_HEADS * ATT_HEAD_DIM

    j = 0
    vec = lambda t: t.reshape(1, d)
    pre_w = (rwkv_mu[j], rwkv_w_in[j], rwkv_w0[j], rwkv_w1[j], rwkv_w2[j], rwkv_a0[j], rwkv_a1[j], rwkv_a2[j])
    head_w = (vec(rwkv_k_k[j]), vec(rwkv_k_a[j]), vec(rwkv_r_k[j]), vec(rwkv_gn_g[j]), vec(rwkv_gn_b[j]))

    xp2 = x_prompt.reshape(bp * tp, d)
    xs2 = x_sample.reshape(bs * ts, d)

    r, k, v, g, lw, a = _rwkv_pre(x_prompt, jnp.zeros((bp, d), F32), *pre_w, tm=_row_tile(bp * tp))
    yp, wkv_p = _wkv_prompt(r, k, v, lw, a, *head_w, batch=bp, seq=tp)
    xp1, q_p, k_p, v_p, gate_p = _post(yp, g, xp2, rwkv_w_out[j], ln_g[0], ln_b[0], tm=_row_tile(bp * tp),
                                       project=(att_w_in[j], BF16))

    r, k, v, g, lw, a = _rwkv_pre(x_sample, state_shift[j], *pre_w, tm=_row_tile(bs * ts))
    ys, wkv_s = _wkv_sample(r, k, v, lw, a, *head_w, state_wkv[j], batch=bs, seq=ts)
    xs1, q_s, k_s, v_s, gate_s = _post(ys, g, xs2, rwkv_w_out[j], ln_g[0], ln_b[0], tm=_row_tile(bs * ts),
                                       project=(att_w_in[j], F32))

    y_prompt = _att_prompt(q_p, k_p, v_p, gate_p, xp1, att_sinks[j], att_w_out[j], ln_g[1], ln_b[1],
                           batch=bp, seq=tp, tm=ROW_TILE).reshape(bp, tp, d)
    win_shape = (bp, WINDOW, ATT_KV_HEADS, ATT_HEAD_DIM)
    win_k_p = k_p.reshape(bp, tp, hkv)[:, tp - WINDOW:].reshape(win_shape)
    win_v_p = v_p.reshape(bp, tp, hkv)[:, tp - WINDOW:].reshape(win_shape)

    to_t = lambda c: jnp.transpose(c, (0, 2, 3, 1))
    o, win_k_s, win_v_s = _att_sample(q_s, k_s, v_s, to_t(cache_win_k[j]), to_t(cache_win_v[j]), att_sinks[j],
                                      batch=bs, t=ts, nb=WKV_ROWS // ts)
    y_sample = _post(o, gate_s, xs1, att_w_out[j], ln_g[1], ln_b[1], tm=_row_tile(bs * ts)).reshape(bs, ts, d)
    win_k_s = jnp.transpose(win_k_s, (0, 3, 1, 2))
    win_v_s = jnp.transpose(win_v_s, (0, 3, 1, 2))

    return (y_prompt, y_sample,
            wkv_p[None], x_prompt[:, -1][None], win_k_p[None], win_v_p[None],
            wkv_s[None], x_sample[:, -1][None], win_k_s[None], win_v_s[None])
```

```python
import functools

import jax
import jax.numpy as jnp
from jax import lax
from jax.experimental import pallas as pl
from jax.experimental.pallas import tpu as pltpu

F32 = jnp.float32
BF16 = jnp.bfloat16

RWKV_HEAD = 64
ATT_HEAD_DIM = 64
ATT_KV_HEADS = 4
ATT_GROUP = 4
WINDOW = 128
ATT_SCALE = ATT_HEAD_DIM ** -0.5
LOG2E = 1.4426950408889634
NEG = -1e30
DEPTH = 2
ALPHA = (2 * DEPTH) ** 0.25
LN_EPS = 1e-5
GN_EPS = 64e-5

V7X_LANES = 128
V7X_SUBLANES = 8
V7X_VMEM_LIMIT_BYTES = 56 * 1024 * 1024

HEADS_PER_VREG = V7X_LANES // RWKV_HEAD
STACK_ROWS = 128
WKV_ROWS = STACK_ROWS // HEADS_PER_VREG
PROMPT_CHUNK = WKV_ROWS
PROMPT_CHUNKS_PER_STEP = 4


def _cparams(*sem):
    return pltpu.CompilerParams(dimension_semantics=sem, vmem_limit_bytes=V7X_VMEM_LIMIT_BYTES)


def _const_spec(shape):
    nd = len(shape)
    return pl.BlockSpec(shape, lambda *_: (0,) * nd, pipeline_mode=pl.Buffered(1))


def _rwkv_pre_kernel(x_ref, first_ref, halo_ref, mu_ref, win_ref, w0_ref, w1_ref, w2_ref, a0_ref, a1_ref, a2_ref,
                     r_ref, k_ref, v_ref, g_ref, lw_ref, a_ref, *, tiles_per_seq):
    x3 = x_ref[...]
    nseq, rows, d = x3.shape
    first = first_ref[...]
    if tiles_per_seq > 1:
        first = jnp.where(pl.program_id(0) % tiles_per_seq == 0, first, halo_ref[:, -1:, :])
    t_idx = lax.broadcasted_iota(jnp.int32, (1, rows, 1), 1)
    x_prev = jnp.where(t_idx == 0, first, pltpu.roll(x3, 1, axis=1))
    x = x3.reshape(nseq * rows, d)
    xx = x_prev.reshape(nseq * rows, d) - x

    def mix(p):
        return (x + xx * mu_ref[p:p + 1, :]).astype(BF16)

    def proj(p):
        return jnp.dot(mix(p), win_ref[p], preferred_element_type=F32)

    r_ref[...] = proj(0)
    k_ref[...] = proj(1)
    v_ref[...] = proj(2)
    g_ref[...] = proj(3)
    hw = jnp.tanh(jnp.dot(mix(4), w1_ref[...], preferred_element_type=F32))
    wl = w0_ref[...] + jnp.dot(hw.astype(BF16), w2_ref[...], preferred_element_type=F32)
    z = -wl
    softplus = jnp.maximum(z, 0.0) + jnp.log(1.0 + jnp.exp(-jnp.abs(z)))
    lw_ref[...] = -jnp.exp(-softplus - 0.5)
    ha = jnp.dot(mix(5), a1_ref[...], preferred_element_type=F32)
    al = a0_ref[...] + jnp.dot(ha.astype(BF16), a2_ref[...], preferred_element_type=F32)
    a_ref[...] = 1.0 / (1.0 + jnp.exp(-al))


def _rwkv_pre(x3, first, mu, w_in, w0, w1, w2, a0, a1, a2, tm):
    nseq, seq, d = x3.shape
    rows = nseq * seq
    lora = w1.shape[1]
    if seq >= tm:
        tps = seq // tm
        x_spec = pl.BlockSpec((1, tm, d), lambda i: (i // tps, i % tps, 0))
        first_spec = pl.BlockSpec((1, 1, d), lambda i: (i // tps, 0, 0))
        halo_spec = pl.BlockSpec(
            (1, V7X_SUBLANES, d),
            lambda i: (i // tps, jnp.maximum((i % tps) * (tm // V7X_SUBLANES) - 1, 0), 0))
    else:
        tps = 1
        x_spec = pl.BlockSpec((tm // seq, seq, d), lambda i: (i, 0, 0))
        first_spec = pl.BlockSpec((tm // seq, 1, d), lambda i: (i, 0, 0))
        halo_spec = pl.BlockSpec((1, V7X_SUBLANES, d), lambda i: (0, 0, 0))
    row_spec = pl.BlockSpec((tm, d), lambda i: (i, 0))
    out = jax.ShapeDtypeStruct((rows, d), F32)
    return pl.pallas_call(
        functools.partial(_rwkv_pre_kernel, tiles_per_seq=tps),
        grid=(rows // tm,),
        in_specs=[x_spec, first_spec, halo_spec, _const_spec((6, d)), _const_spec((4, d, d)),
                  _const_spec((1, d)), _const_spec((d, lora)), _const_spec((lora, d)), _const_spec((1, d)),
                  _const_spec((d, lora)), _const_spec((lora, d))],
        out_specs=[row_spec] * 6,
        out_shape=[out] * 6,
        compiler_params=_cparams("parallel"),
        name="rwkv_pre",
    )(x3, first.reshape(nseq, 1, d), x3, mu, w_in.astype(BF16), w0.reshape(1, d), w1.astype(BF16),
      w2.astype(BF16), a0.reshape(1, d), a1.astype(BF16), a2.astype(BF16))


def _mm(a, b, dims):
    return lax.dot_general(a.astype(BF16), b.astype(BF16), (dims, ((), ())), preferred_element_type=F32)


def _mm_nn(a, b):
    return _mm(a, b, ((1,), (0,)))


def _mm_nt(a, b):
    return _mm(a, b, ((1,), (1,)))


def _mm_tn(a, b):
    return _mm(a, b, ((0,), (0,)))


def _wkv_prepare(r, k, v, lw, a, k_k, k_a, r_k, nb, c):
    chains = range(len(r))
    rows = nb * c
    n = STACK_ROWS
    lane = lax.broadcasted_iota(jnp.int32, (1, V7X_LANES), 1)
    head0 = lane < RWKV_HEAD
    mk = (jnp.where(head0, 1.0, 0.0).astype(F32), jnp.where(head0, 0.0, 1.0).astype(F32))
    col_head0 = (lane // c) % HEADS_PER_VREG == 0
    cm = (jnp.where(col_head0, 1.0, 0.0).astype(F32), jnp.where(col_head0, 0.0, 1.0).astype(F32))

    def head_sum(t):
        s0 = jnp.sum(t * mk[0], axis=-1, keepdims=True)
        s1 = jnp.sum(t * mk[1], axis=-1, keepdims=True)
        return jnp.where(head0, s0, s1)

    def stack(t, masks):
        if c % (2 * V7X_SUBLANES) == 0:
            t = t.astype(BF16)
            masks = [m.astype(BF16) for m in masks]
        parts = []
        for b in range(nb):
            tb = t[b * c:(b + 1) * c]
            parts += [tb * masks[0], tb * masks[1]]
        return jnp.concatenate(parts, axis=0).astype(BF16)

    ri = lax.broadcasted_iota(jnp.int32, (rows, rows), 0)
    ci = lax.broadcasted_iota(jnp.int32, (rows, rows), 1)
    tri = jnp.where((ri // c == ci // c) & (ri >= ci), 1.0, 0.0).astype(BF16)
    lw_hi = [lw[p].astype(BF16) for p in chains]
    cum = [_mm_nn(tri, lw_hi[p]) + _mm_nn(tri, lw[p] - lw_hi[p].astype(F32)) for p in chains]

    kkr = [k[p] * k_k[p] for p in chains]
    kk = [kkr[p] / jnp.maximum(jnp.sqrt(head_sum(kkr[p] * kkr[p])), 1e-12) for p in chains]
    kf = [k[p] * (1.0 + (a[p] - 1.0) * k_a[p]) for p in chains]
    g_incl = [jnp.exp(cum[p]) for p in chains]
    g_inv = [jnp.exp(-cum[p]) for p in chains]
    at = [(-kk[p] * jnp.exp(cum[p] - lw[p])).astype(BF16) for p in chains]
    rt = [(r[p] * g_incl[p]).astype(BF16) for p in chains]
    bt = [(kk[p] * a[p] * g_inv[p]).astype(BF16) for p in chains]
    kt = [(kf[p] * g_inv[p]).astype(BF16) for p in chains]
    vb = [v[p].astype(BF16) for p in chains]
    xa = [jnp.concatenate([at[p], rt[p]], axis=0) for p in chains]
    yb = [jnp.concatenate([stack(bt[p], mk), stack(kt[p], mk)], axis=0) for p in chains]
    v_s = [stack(vb[p], mk) for p in chains]

    qi = lax.broadcasted_iota(jnp.int32, (rows, n), 0)
    qj = lax.broadcasted_iota(jnp.int32, (rows, n), 1)
    same = qi // c == qj // (HEADS_PER_VREG * c)
    strict = same & (qj % c < qi % c)
    incl = same & (qj % c <= qi % c)
    eye = jnp.where(same & (qj % c == qi % c), 1.0, 0.0).astype(F32)

    gm = [_mm_nt(xa[p], yb[p]) for p in chains]
    a_ab = [jnp.where(strict, gm[p][:rows, :n], 0.0) for p in chains]
    a_kr = [jnp.concatenate([jnp.where(strict, gm[p][:rows, n:], 0.0),
                             jnp.where(incl, gm[p][rows:, n:], 0.0)], axis=0).astype(BF16) for p in chains]
    a_rb = [jnp.where(incl, gm[p][rows:, :n], 0.0).astype(BF16) for p in chains]

    tinv = [eye + a_ab[p] for p in chains]
    pw = [a_ab[p] for p in chains]
    if c > 2:
        pw = [_mm_nn(pw[p], stack(pw[p], cm)) for p in chains]
        span = 4
        while span < c:
            sq = [_mm_nn(pw[p], jnp.concatenate([stack(pw[p], cm), stack(tinv[p], cm)], axis=1)) for p in chains]
            tinv = [tinv[p] + sq[p][:, n:] for p in chains]
            pw = [sq[p][:, :n] for p in chains]
            span *= 2
        tinv = [tinv[p] + _mm_nn(pw[p], stack(tinv[p], cm)) for p in chains]

    av = [_mm_nn(a_kr[p], v_s[p]) for p in chains]
    bonus = [head_sum(r[p] * kf[p] * r_k[p]) * v[p] for p in chains]
    g_last = [[g_incl[p][(b + 1) * c - 1:(b + 1) * c] for b in range(nb)] for p in chains]
    return dict(xa=xa, at=at, rt=rt, bt=bt, kt=kt, vb=vb, tinv=[t.astype(BF16) for t in tinv], av=av, a_rb=a_rb,
                bonus=bonus, g_last=g_last, head_sum=head_sum, stack=lambda t: stack(t, mk))


def _wkv_apply(prep, sel, states, gn_g, gn_b, nb, c):
    rows = nb * c
    xa, at, rt, bt, kt, vb = (prep[key] for key in ("xa", "at", "rt", "bt", "kt", "vb"))
    stack = prep["stack"]
    if nb == 1:
        ah = [_mm_nt(xa[p], states[i][0]) for i, p in enumerate(sel)]
    else:
        ah = []
        for i, p in enumerate(sel):
            parts = [_mm_nt(jnp.concatenate([at[p][b * c:(b + 1) * c], rt[p][b * c:(b + 1) * c]], axis=0), states[i][b])
                     for b in range(nb)]
            ah.append(jnp.concatenate([t[:c] for t in parts] + [t[c:] for t in parts], axis=0))

    u = [_mm_nn(prep["tinv"][p], stack(ah[i][:rows] + prep["av"][p][:rows])) for i, p in enumerate(sel)]
    y = [ah[i][rows:] + prep["av"][p][rows:] + _mm_nn(prep["a_rb"][p], stack(u[i])) for i, p in enumerate(sel)]

    vi = lax.broadcasted_iota(jnp.int32, (V7X_LANES, V7X_LANES), 0)
    ki = lax.broadcasted_iota(jnp.int32, (V7X_LANES, V7X_LANES), 1)
    same_head = vi // RWKV_HEAD == ki // RWKV_HEAD
    new_states = []
    for i, p in enumerate(sel):
        ub = u[i].astype(BF16)
        per_batch = []
        for b in range(nb):
            sl = slice(b * c, (b + 1) * c)
            upd = _mm_tn(jnp.concatenate([ub[sl], vb[p][sl]], axis=0), jnp.concatenate([bt[p][sl], kt[p][sl]], axis=0))
            per_batch.append((states[i][b] + jnp.where(same_head, upd, 0.0)) * prep["g_last"][p][b])
        new_states.append(per_batch)

    head_sum = prep["head_sum"]
    inv_n = 1.0 / RWKV_HEAD
    out = []
    for i, p in enumerate(sel):
        mean = head_sum(y[i]) * inv_n
        yc = y[i] - mean
        var = head_sum(yc * yc) * inv_n
        out.append(yc * lax.rsqrt(var + GN_EPS) * gn_g[i] + gn_b[i] + prep["bonus"][p])
    return out, new_states


def _pair_lanes(p):
    return slice(p * V7X_LANES, (p + 1) * V7X_LANES)


def _state_to_blockdiag(s0, s1):
    z = jnp.zeros_like(s0)
    return jnp.concatenate([jnp.concatenate([s0, z], axis=1), jnp.concatenate([z, s1], axis=1)], axis=0)


def _wkv_prompt_kernel(r_ref, k_ref, v_ref, lw_ref, a_ref, kk_ref, ka_ref, rk_ref, gg_ref, gb_ref,
                       y_ref, sfin_ref, s_scr, *, n_batch, n_pairs, c, n_sub):
    ci = pl.program_id(0)

    @pl.when(ci == 0)
    def _():
        s_scr[...] = jnp.zeros_like(s_scr)

    seq_pairs = [(b, p) for b in range(n_batch) for p in range(n_pairs)]
    chains = [(j, b, p) for j in range(n_sub) for b, p in seq_pairs]
    data = [[ref[b, j * c:(j + 1) * c, _pair_lanes(p)] for j, b, p in chains]
            for ref in (r_ref, k_ref, v_ref, lw_ref, a_ref)]
    prm = [[ref[:, _pair_lanes(p)] for _, _, p in chains] for ref in (kk_ref, ka_ref, rk_ref)]
    prep = _wkv_prepare(*data, *prm, 1, c)
    gn = [[ref[:, _pair_lanes(p)] for _, p in seq_pairs] for ref in (gg_ref, gb_ref)]
    states = [[s_scr[i]] for i in range(len(seq_pairs))]
    for j in range(n_sub):
        sel = [j * len(seq_pairs) + i for i in range(len(seq_pairs))]
        yn, states = _wkv_apply(prep, sel, states, *gn, 1, c)
        for i, (b, p) in enumerate(seq_pairs):
            y_ref[b, j * c:(j + 1) * c, _pair_lanes(p)] = yn[i]
    for i in range(len(seq_pairs)):
        s_scr[i] = states[i][0]

    @pl.when(ci == pl.num_programs(0) - 1)
    def _():
        for i, (b, p) in enumerate(seq_pairs):
            s = s_scr[i]
            sfin_ref[b, 2 * p] = s[:RWKV_HEAD, :RWKV_HEAD]
            sfin_ref[b, 2 * p + 1] = s[RWKV_HEAD:, RWKV_HEAD:]


def _wkv_sample_kernel(r_ref, k_ref, v_ref, lw_ref, a_ref, kk_ref, ka_ref, rk_ref, gg_ref, gb_ref,
                       s0_ref, y_ref, sfin_ref, *, n_pairs, nb, c):
    pairs = range(n_pairs)
    args = [[ref[:, _pair_lanes(p)] for p in pairs]
            for ref in (r_ref, k_ref, v_ref, lw_ref, a_ref, kk_ref, ka_ref, rk_ref)]
    prep = _wkv_prepare(*args, nb, c)
    states = [[_state_to_blockdiag(s0_ref[b, 2 * p], s0_ref[b, 2 * p + 1]) for b in range(nb)] for p in pairs]
    gn = [[ref[:, _pair_lanes(p)] for p in pairs] for ref in (gg_ref, gb_ref)]
    yn, new = _wkv_apply(prep, list(pairs), states, *gn, nb, c)
    for p in pairs:
        y_ref[:, _pair_lanes(p)] = yn[p]
        for b in range(nb):
            sfin_ref[b, 2 * p] = new[p][b][:RWKV_HEAD, :RWKV_HEAD]
            sfin_ref[b, 2 * p + 1] = new[p][b][RWKV_HEAD:, RWKV_HEAD:]


def _wkv_prompt(r, k, v, lw, a, k_k, k_a, r_k, gn_g, gn_b, batch, seq):
    rows, d = r.shape
    heads = d // RWKV_HEAD
    n_pairs = heads // HEADS_PER_VREG
    c = PROMPT_CHUNK
    n_sub = PROMPT_CHUNKS_PER_STEP
    row_spec = pl.BlockSpec((batch, n_sub * c, d), lambda i: (0, i, 0))
    vec = _const_spec((1, d))
    kern = functools.partial(_wkv_prompt_kernel, n_batch=batch, n_pairs=n_pairs, c=c, n_sub=n_sub)
    seq3 = lambda t: t.reshape(batch, seq, d)
    y, s_fin = pl.pallas_call(
        kern,
        grid=(seq // (n_sub * c),),
        in_specs=[row_spec] * 5 + [vec] * 5,
        out_specs=[row_spec, pl.BlockSpec((batch, heads, RWKV_HEAD, RWKV_HEAD), lambda i: (0, 0, 0, 0))],
        out_shape=[jax.ShapeDtypeStruct((batch, seq, d), F32),
                   jax.ShapeDtypeStruct((batch, heads, RWKV_HEAD, RWKV_HEAD), F32)],
        scratch_shapes=[pltpu.VMEM((batch * n_pairs, V7X_LANES, V7X_LANES), F32)],
        compiler_params=_cparams("arbitrary"),
        name="wkv_prompt",
    )(seq3(r), seq3(k), seq3(v), seq3(lw), seq3(a), k_k, k_a, r_k, gn_g, gn_b)
    return y.reshape(rows, d), s_fin


def _wkv_sample(r, k, v, lw, a, k_k, k_a, r_k, gn_g, gn_b, state, batch, seq):
    rows, d = r.shape
    heads = d // RWKV_HEAD
    nb = WKV_ROWS // seq
    row_spec = pl.BlockSpec((WKV_ROWS, d), lambda i: (i, 0))
    st_spec = pl.BlockSpec((nb, heads, RWKV_HEAD, RWKV_HEAD), lambda i: (i, 0, 0, 0))
    vec = _const_spec((1, d))
    kern = functools.partial(_wkv_sample_kernel, n_pairs=heads // HEADS_PER_VREG, nb=nb, c=seq)
    return pl.pallas_call(
        kern,
        grid=(batch // nb,),
        in_specs=[row_spec] * 5 + [vec] * 5 + [st_spec],
        out_specs=[row_spec, st_spec],
        out_shape=[jax.ShapeDtypeStruct((rows, d), F32),
                   jax.ShapeDtypeStruct((batch, heads, RWKV_HEAD, RWKV_HEAD), F32)],
        compiler_params=_cparams("parallel"),
        name="wkv_sample",
    )(r, k, v, lw, a, k_k, k_a, r_k, gn_g, gn_b, state)


def _branch_out(y_ref, g_ref, x_ref, w_ref, lng_ref, lnb_ref):
    g = g_ref[...]
    h = (y_ref[...] * (g / (1.0 + jnp.exp(-g)))).astype(BF16)
    z = ALPHA * x_ref[...] + jnp.dot(h, w_ref[...], preferred_element_type=F32)
    mu = jnp.mean(z, axis=-1, keepdims=True)
    zc = z - mu
    var = jnp.mean(zc * zc, axis=-1, keepdims=True)
    return zc * lax.rsqrt(var + LN_EPS) * lng_ref[...] + lnb_ref[...]


def _post_kernel(y_ref, g_ref, x_ref, w_ref, lng_ref, lnb_ref, o_ref):
    o_ref[...] = _branch_out(y_ref, g_ref, x_ref, w_ref, lng_ref, lnb_ref)


def _post_project_kernel(y_ref, g_ref, x_ref, w_ref, lng_ref, lnb_ref, win_ref,
                         o_ref, q_ref, k_ref, v_ref, gate_ref, *, hq, hkv):
    x1 = _branch_out(y_ref, g_ref, x_ref, w_ref, lng_ref, lnb_ref)
    o_ref[...] = x1
    z = jnp.dot(x1.astype(BF16), win_ref[...], preferred_element_type=F32)
    q_ref[...] = (z[:, :hq] * (ATT_SCALE * LOG2E)).astype(q_ref.dtype)
    k_ref[...] = z[:, hq:hq + hkv]
    v_ref[...] = z[:, hq + hkv:hq + 2 * hkv]
    gate_ref[...] = z[:, hq + 2 * hkv:]


def _post(y, g, x2, w_out, ln_g, ln_b, tm, project=None):
    rows, d = x2.shape
    din = y.shape[1]
    in_spec = pl.BlockSpec((tm, din), lambda i: (i, 0))
    row_spec = pl.BlockSpec((tm, d), lambda i: (i, 0))
    in_specs = [in_spec, in_spec, row_spec, _const_spec((din, d)), _const_spec((1, d)), _const_spec((1, d))]
    args = (y, g, x2, w_out.astype(BF16), ln_g.reshape(1, d), ln_b.reshape(1, d))
    x_shape = jax.ShapeDtypeStruct((rows, d), F32)
    if project is None:
        return pl.pallas_call(
            _post_kernel, grid=(rows // tm,), in_specs=in_specs, out_specs=row_spec, out_shape=x_shape,
            compiler_params=_cparams("parallel"), name="branch_post",
        )(*args)
    w_in, q_dtype = project
    hq = ATT_KV_HEADS * ATT_GROUP * ATT_HEAD_DIM
    hkv = ATT_KV_HEADS * ATT_HEAD_DIM
    wide = pl.BlockSpec((tm, hq), lambda i: (i, 0))
    narrow = pl.BlockSpec((tm, hkv), lambda i: (i, 0))
    return pl.pallas_call(
        functools.partial(_post_project_kernel, hq=hq, hkv=hkv),
        grid=(rows // tm,),
        in_specs=in_specs + [_const_spec(w_in.shape)],
        out_specs=[row_spec, wide, narrow, narrow, wide],
        out_shape=[x_shape, jax.ShapeDtypeStruct((rows, hq), q_dtype), jax.ShapeDtypeStruct((rows, hkv), F32),
                   jax.ShapeDtypeStruct((rows, hkv), F32), jax.ShapeDtypeStruct((rows, hq), F32)],
        compiler_params=_cparams("parallel"),
        name="branch_post_project",
    )(*args, w_in.astype(BF16))


def _alibi_slope(h):
    n_heads = ATT_KV_HEADS * ATT_GROUP
    return LOG2E * 2.0 ** (-8.0 * (h + 1) / n_heads)


def _fold_masks(q_time):
    col = lax.broadcasted_iota(jnp.int32, q_time.shape, 1)
    from_prev = col > q_time
    dist = jnp.where(from_prev, q_time + WINDOW - col, q_time - col).astype(F32)
    return from_prev, dist


def _softmax_folded(sp, sc, slopes, sinks, from_prev, dist, dead):
    idx = range(len(sp))
    s = [jnp.where(from_prev, sp[i], sc[i]) - slopes[i] * dist for i in idx]
    if dead is not None:
        s = [jnp.where(dead, NEG, s[i]) for i in idx]
    m = [jnp.maximum(jnp.max(s[i], axis=-1, keepdims=True), sinks[i]) for i in idx]
    p = [jnp.exp2(s[i] - m[i]) for i in idx]
    den = [jnp.sum(p[i], axis=-1, keepdims=True) + jnp.exp2(sinks[i] - m[i]) for i in idx]
    pp = [jnp.where(from_prev, p[i], 0.0).astype(BF16) for i in idx]
    pc = [jnp.where(from_prev, 0.0, p[i]).astype(BF16) for i in idx]
    return pp, pc, den


_NT = (((1,), (1,)), ((), ()))


def _att_prompt_kernel(sink_ref, q_ref, kc_ref, vc_ref, kp_ref, vp_ref, gate_ref, x_ref, w_ref, lng_ref, lnb_ref,
                       y_ref, o_scr, *, n_blk, n_acc):
    step = pl.program_id(1)
    from_prev, dist = _fold_masks(lax.broadcasted_iota(jnp.int32, (WINDOW, WINDOW), 0))
    hd = ATT_HEAD_DIM
    heads = range(ATT_KV_HEADS * ATT_GROUP)
    slopes = [_alibi_slope(h) for h in heads]
    sinks = [sink_ref[h] * LOG2E for h in heads]
    row0 = pl.multiple_of((step % n_acc) * (n_blk * WINDOW), n_blk * WINDOW)
    items = [(blk, h) for blk in range(n_blk) for h in heads]
    kcat, vcat = [], []
    for blk in range(n_blk):
        rows = slice(blk * WINDOW, (blk + 1) * WINDOW)
        prev_of = (lambda ref_p, ref_c: ref_p) if blk == 0 else \
            (lambda ref_p, ref_c, blk=blk: ref_c.at[(blk - 1) * WINDOW:blk * WINDOW])
        kc, vc = [[jnp.concatenate([prev_of(prev, cur)[:, kvh * hd:(kvh + 1) * hd],
                                    cur[rows, kvh * hd:(kvh + 1) * hd]], axis=0).astype(BF16)
                   for kvh in range(ATT_KV_HEADS)] for prev, cur in ((kp_ref, kc_ref), (vp_ref, vc_ref))]
        kcat.append(kc)
        vcat.append(vc)
    s2 = [lax.dot_general(q_ref[blk * WINDOW:(blk + 1) * WINDOW, h * hd:(h + 1) * hd], kcat[blk][h // ATT_GROUP], _NT,
                          preferred_element_type=F32) for blk, h in items]
    n_h = len(heads)
    groups = [(range(n_h), from_prev & (step == 0))] + ([(range(n_h, len(items)), None)] if n_blk > 1 else [])
    parts = [_softmax_folded([s2[i][:, :WINDOW] for i in sel], [s2[i][:, WINDOW:] for i in sel],
                             [slopes[items[i][1]] for i in sel], [sinks[items[i][1]] for i in sel],
                             from_prev, dist, dead) for sel, dead in groups]
    pp, pc, den = [sum((list(part[j]) for part in parts), []) for j in range(3)]
    for i, (blk, h) in enumerate(items):
        o = jnp.dot(jnp.concatenate([pp[i], pc[i]], axis=1), vcat[blk][h // ATT_GROUP], preferred_element_type=F32)
        o_scr[pl.ds(row0 + blk * WINDOW, WINDOW), h * hd:(h + 1) * hd] = o / den[i]

    @pl.when((step % n_acc == n_acc - 1) & (step >= 0))
    def _():
        y_ref[...] = _branch_out(o_scr, gate_ref, x_ref, w_ref, lng_ref, lnb_ref)


def _att_prompt(q, k, v, gate, x2, sinks, w_out, ln_g, ln_b, batch, seq, tm):
    rows, hq = q.shape
    hkv = k.shape[1]
    d = x2.shape[1]
    n_blk = ATT_BLOCKS_PER_STEP
    steps = seq // (n_blk * WINDOW)
    n_acc = tm // (n_blk * WINDOW)
    cur = lambda b, n: (b * steps + n, 0)
    prev = lambda b, n: ((b * steps + n) * n_blk - jnp.minimum(n, 1), 0)
    tile = lambda b, n: ((b * steps + n) // n_acc, 0)
    return pl.pallas_call(
        functools.partial(_att_prompt_kernel, n_blk=n_blk, n_acc=n_acc),
        grid=(batch, steps),
        in_specs=[pl.BlockSpec(memory_space=pltpu.SMEM),
                  pl.BlockSpec((n_blk * WINDOW, hq), cur),
                  pl.BlockSpec((n_blk * WINDOW, hkv), cur), pl.BlockSpec((n_blk * WINDOW, hkv), cur),
                  pl.BlockSpec((WINDOW, hkv), prev), pl.BlockSpec((WINDOW, hkv), prev),
                  pl.BlockSpec((tm, hq), tile), pl.BlockSpec((tm, d), tile),
                  _const_spec((hq, d)), _const_spec((1, d)), _const_spec((1, d))],
        out_specs=pl.BlockSpec((tm, d), tile),
        out_shape=jax.ShapeDtypeStruct((rows, d), F32),
        scratch_shapes=[pltpu.VMEM((tm, hq), F32)],
        compiler_params=_cparams("parallel", "arbitrary"),
        name="att_prompt",
    )(sinks, q, k, v, k, v, gate, x2, w_out.astype(BF16), ln_g.reshape(1, d), ln_b.reshape(1, d))


def _att_sample_kernel(sink_ref, q_ref, kn_ref, vn_ref, ckt_ref, cvt_ref, o_ref, kwt_ref, vwt_ref, *, nb, t):
    m = ATT_GROUP * t
    grp = lax.broadcasted_iota(jnp.int32, (m, 1), 0) // t
    from_prev, dist = _fold_masks(lax.broadcasted_iota(jnp.int32, (m, WINDOW), 0) % t)
    hd = ATT_HEAD_DIM
    pad = jnp.zeros((WINDOW - t, hd), F32)
    items = [(kvh, b) for kvh in range(ATT_KV_HEADS) for b in range(nb)]
    idx = range(len(items))
    slopes, sinks = [], []
    for kvh in range(ATT_KV_HEADS):
        slope = jnp.zeros((m, 1), F32)
        sink = jnp.zeros((m, 1), F32)
        for g in range(ATT_GROUP):
            h = kvh * ATT_GROUP + g
            slope = jnp.where(grp == g, _alibi_slope(h), slope)
            sink = jnp.where(grp == g, sink_ref[h] * LOG2E, sink)
        slopes += [slope] * nb
        sinks += [sink] * nb
    qs = [jnp.concatenate([q_ref[b * t:(b + 1) * t, (kvh * ATT_GROUP + g) * hd:(kvh * ATT_GROUP + g + 1) * hd]
                           for g in range(ATT_GROUP)], axis=0).astype(BF16) for kvh, b in items]
    k_new = [kn_ref[b * t:(b + 1) * t, kvh * hd:(kvh + 1) * hd] for kvh, b in items]
    v_new = [vn_ref[b * t:(b + 1) * t, kvh * hd:(kvh + 1) * hd] for kvh, b in items]
    kt_old = [ckt_ref[b, kvh] for kvh, b in items]
    vt_old = [cvt_ref[b, kvh] for kvh, b in items]

    sp = [jnp.dot(qs[i], kt_old[i].astype(BF16), preferred_element_type=F32) for i in idx]
    sc = [lax.dot_general(qs[i], jnp.concatenate([k_new[i], pad], axis=0).astype(BF16), _NT,
                          preferred_element_type=F32) for i in idx]
    pp, pc, den = _softmax_folded(sp, sc, slopes, sinks, from_prev, dist, None)
    o = [(lax.dot_general(pp[i], vt_old[i].astype(BF16), _NT, preferred_element_type=F32)
          + jnp.dot(pc[i], jnp.concatenate([v_new[i], pad], axis=0).astype(BF16), preferred_element_type=F32))
         / den[i] for i in idx]
    for i, (kvh, b) in enumerate(items):
        for g in range(ATT_GROUP):
            h = kvh * ATT_GROUP + g
            o_ref[b * t:(b + 1) * t, h * hd:(h + 1) * hd] = o[i][g * t:(g + 1) * t]

    lane = lax.broadcasted_iota(jnp.int32, (hd, WINDOW), 1)
    for new, old, out_ref in ((k_new, kt_old, kwt_ref), (v_new, vt_old, vwt_ref)):
        new_t = [jnp.concatenate([pad, new[i]], axis=0).T for i in idx]
        for i, (kvh, b) in enumerate(items):
            out_ref[b, kvh] = jnp.where(lane >= WINDOW - t, new_t[i], pltpu.roll(old[i], WINDOW - t, axis=1))


def _att_sample(q, k, v, cache_kt, cache_vt, sinks, batch, t, nb):
    rows, hq = q.shape
    hkv = k.shape[1]
    row = lambda i: (i, 0)
    cache_spec = pl.BlockSpec((nb,) + cache_kt.shape[1:], lambda i: (i, 0, 0, 0))
    cache_shape = jax.ShapeDtypeStruct(cache_kt.shape, F32)
    return pl.pallas_call(
        functools.partial(_att_sample_kernel, nb=nb, t=t),
        grid=(batch // nb,),
        in_specs=[pl.BlockSpec(memory_space=pltpu.SMEM),
                  pl.BlockSpec((nb * t, hq), row), pl.BlockSpec((nb * t, hkv), row),
                  pl.BlockSpec((nb * t, hkv), row), cache_spec, cache_spec],
        out_specs=[pl.BlockSpec((nb * t, hq), row), cache_spec, cache_spec],
        out_shape=[jax.ShapeDtypeStruct((rows, hq), F32), cache_shape, cache_shape],
        compiler_params=_cparams("parallel"),
        name="att_sample",
    )(sinks, q, k, v, cache_kt, cache_vt)


ROW_TILE = 512
ATT_BLOCKS_PER_STEP = 4


def _row_tile(rows):
    return ROW_TILE if rows % ROW_TILE == 0 else rows


def kernel(x_prompt, x_sample, state_wkv, state_shift, cache_win_k, cache_win_v, ln_g, ln_b, rwkv_mu, rwkv_w_in, rwkv_w0, rwkv_w1, rwkv_w2, rwkv_a0, rwkv_a1, rwkv_a2, rwkv_k_k, rwkv_k_a, rwkv_r_k, rwkv_gn_g, rwkv_gn_b, rwkv_w_out, att_w_in, att_sinks, att_w_out):
    bp, tp, d = x_prompt.shape
    bs, ts, _ = x_sample.shape
    assert tp % (PROMPT_CHUNK * PROMPT_CHUNKS_PER_STEP) == 0 and tp % ROW_TILE == 0 and ROW_TILE % (ATT_BLOCKS_PER_STEP * WINDOW) == 0 and WKV_ROWS % ts == 0 and bs % (WKV_ROWS // ts) == 0
    assert cache_win_k.shape[2] == WINDOW
    heads = d // RWKV_HEAD
    hkv = ATT_KV_HEADS * ATT_HEAD_DIM

    j = 0
    vec = lambda t: t.reshape(1, d)
    pre_w = (rwkv_mu[j], rwkv_w_in[j], rwkv_w0[j], rwkv_w1[j], rwkv_w2[j], rwkv_a0[j], rwkv_a1[j], rwkv_a2[j])
    head_w = (vec(rwkv_k_k[j]), vec(rwkv_k_a[j]), vec(rwkv_r_k[j]), vec(rwkv_gn_g[j]), vec(rwkv_gn_b[j]))

    xp2 = x_prompt.reshape(bp * tp, d)
    xs2 = x_sample.reshape(bs * ts, d)

    r, k, v, g, lw, a = _rwkv_pre(x_prompt, jnp.zeros((bp, d), F32), *pre_w, tm=_row_tile(bp * tp))
    yp, wkv_p = _wkv_prompt(r, k, v, lw, a, *head_w, batch=bp, seq=tp)
    xp1, q_p, k_p, v_p, gate_p = _post(yp, g, xp2, rwkv_w_out[j], ln_g[0], ln_b[0], tm=_row_tile(bp * tp),
                                       project=(att_w_in[j], BF16))

    r, k, v, g, lw, a = _rwkv_pre(x_sample, state_shift[j], *pre_w, tm=_row_tile(bs * ts))
    ys, wkv_s = _wkv_sample(r, k, v, lw, a, *head_w, state_wkv[j], batch=bs, seq=ts)
    xs1, q_s, k_s, v_s, gate_s = _post(ys, g, xs2, rwkv_w_out[j], ln_g[0], ln_b[0], tm=_row_tile(bs * ts),
                                       project=(att_w_in[j], F32))

    y_prompt = _att_prompt(q_p, k_p, v_p, gate_p, xp1, att_sinks[j], att_w_out[j], ln_g[1], ln_b[1],
                           batch=bp, seq=tp, tm=ROW_TILE).reshape(bp, tp, d)
    win_shape = (bp, WINDOW, ATT_KV_HEADS, ATT_HEAD_DIM)
    win_k_p = k_p.reshape(bp, tp, hkv)[:, tp - WINDOW:].reshape(win_shape)
    win_v_p = v_p.reshape(bp, tp, hkv)[:, tp - WINDOW:].reshape(win_shape)

    to_t = lambda c: jnp.transpose(c, (0, 2, 3, 1))
    o, win_k_s, win_v_s = _att_sample(q_s, k_s, v_s, to_t(cache_win_k[j]), to_t(cache_win_v[j]), att_sinks[j],
                                      batch=bs, t=ts, nb=WKV_ROWS // ts)
    y_sample = _post(o, gate_s, xs1, att_w_out[j], ln_g[1], ln_b[1], tm=_row_tile(bs * ts)).reshape(bs, ts, d)
    win_k_s = jnp.transpose(win_k_s, (0, 3, 1, 2))
    win_v_s = jnp.transpose(win_v_s, (0, 3, 1, 2))

    return (y_prompt, y_sample,
            wkv_p[None], x_prompt[:, -1][None], win_k_p[None], win_v_p[None],
            wkv_s[None], x_sample[:, -1][None], win_k_s[None], win_v_s[None])
```

```python
import functools

import jax
import jax.numpy as jnp
from jax import lax
from jax.experimental import pallas as pl
from jax.experimental.pallas import tpu as pltpu

F32 = jnp.float32
BF16 = jnp.bfloat16

RWKV_HEAD = 64
ATT_HEAD_DIM = 64
ATT_KV_HEADS = 4
ATT_GROUP = 4
WINDOW = 128
ATT_SCALE = ATT_HEAD_DIM ** -0.5
LOG2E = 1.4426950408889634
NEG = -1e30
DEPTH = 2
ALPHA = (2 * DEPTH) ** 0.25
LN_EPS = 1e-5
GN_EPS = 64e-5

V7X_LANES = 128
V7X_SUBLANES = 8
V7X_VMEM_LIMIT_BYTES = 56 * 1024 * 1024

HEADS_PER_VREG = V7X_LANES // RWKV_HEAD
STACK_ROWS = 128
WKV_ROWS = STACK_ROWS // HEADS_PER_VREG
PROMPT_CHUNK = WKV_ROWS
PROMPT_CHUNKS_PER_STEP = 4
SAMPLE_GROUPS_PER_STEP = 2


def _cparams(*sem):
    return pltpu.CompilerParams(dimension_semantics=sem, vmem_limit_bytes=V7X_VMEM_LIMIT_BYTES)


def _const_spec(shape):
    nd = len(shape)
    return pl.BlockSpec(shape, lambda *_: (0,) * nd, pipeline_mode=pl.Buffered(1))


def _rwkv_pre_kernel(x_ref, first_ref, halo_ref, mu_ref, win_ref, w0_ref, w1_ref, w2_ref, a0_ref, a1_ref, a2_ref,
                     r_ref, k_ref, v_ref, g_ref, lw_ref, a_ref, *, tiles_per_seq):
    x3 = x_ref[...]
    nseq, rows, d = x3.shape
    first = first_ref[...]
    if tiles_per_seq > 1:
        first = jnp.where(pl.program_id(0) % tiles_per_seq == 0, first, halo_ref[:, -1:, :])
    t_idx = lax.broadcasted_iota(jnp.int32, (1, rows, 1), 1)
    x_prev = jnp.where(t_idx == 0, first, pltpu.roll(x3, 1, axis=1))
    x = x3.reshape(nseq * rows, d)
    xx = x_prev.reshape(nseq * rows, d) - x

    def mix(p):
        return (x + xx * mu_ref[p:p + 1, :]).astype(BF16)

    def proj(p):
        return jnp.dot(mix(p), win_ref[p], preferred_element_type=F32)

    r_ref[...] = proj(0)
    k_ref[...] = proj(1)
    v_ref[...] = proj(2)
    g_ref[...] = proj(3)
    hw = jnp.tanh(jnp.dot(mix(4), w1_ref[...], preferred_element_type=F32))
    wl = w0_ref[...] + jnp.dot(hw.astype(BF16), w2_ref[...], preferred_element_type=F32)
    z = -wl
    softplus = jnp.maximum(z, 0.0) + jnp.log(1.0 + jnp.exp(-jnp.abs(z)))
    lw_ref[...] = -jnp.exp(-softplus - 0.5)
    ha = jnp.dot(mix(5), a1_ref[...], preferred_element_type=F32)
    al = a0_ref[...] + jnp.dot(ha.astype(BF16), a2_ref[...], preferred_element_type=F32)
    a_ref[...] = 1.0 / (1.0 + jnp.exp(-al))


def _rwkv_pre(x3, first, mu, w_in, w0, w1, w2, a0, a1, a2, tm):
    nseq, seq, d = x3.shape
    rows = nseq * seq
    lora = w1.shape[1]
    if seq >= tm:
        tps = seq // tm
        x_spec = pl.BlockSpec((1, tm, d), lambda i: (i // tps, i % tps, 0))
        first_spec = pl.BlockSpec((1, 1, d), lambda i: (i // tps, 0, 0))
        halo_spec = pl.BlockSpec(
            (1, V7X_SUBLANES, d),
            lambda i: (i // tps, jnp.maximum((i % tps) * (tm // V7X_SUBLANES) - 1, 0), 0))
    else:
        tps = 1
        x_spec = pl.BlockSpec((tm // seq, seq, d), lambda i: (i, 0, 0))
        first_spec = pl.BlockSpec((tm // seq, 1, d), lambda i: (i, 0, 0))
        halo_spec = pl.BlockSpec((1, V7X_SUBLANES, d), lambda i: (0, 0, 0))
    row_spec = pl.BlockSpec((tm, d), lambda i: (i, 0))
    out = jax.ShapeDtypeStruct((rows, d), F32)
    return pl.pallas_call(
        functools.partial(_rwkv_pre_kernel, tiles_per_seq=tps),
        grid=(rows // tm,),
        in_specs=[x_spec, first_spec, halo_spec, _const_spec((6, d)), _const_spec((4, d, d)),
                  _const_spec((1, d)), _const_spec((d, lora)), _const_spec((lora, d)), _const_spec((1, d)),
                  _const_spec((d, lora)), _const_spec((lora, d))],
        out_specs=[row_spec] * 6,
        out_shape=[out] * 6,
        compiler_params=_cparams("parallel"),
        name="rwkv_pre",
    )(x3, first.reshape(nseq, 1, d), x3, mu, w_in.astype(BF16), w0.reshape(1, d), w1.astype(BF16),
      w2.astype(BF16), a0.reshape(1, d), a1.astype(BF16), a2.astype(BF16))


def _mm(a, b, dims):
    return lax.dot_general(a.astype(BF16), b.astype(BF16), (dims, ((), ())), preferred_element_type=F32)


def _mm_nn(a, b):
    return _mm(a, b, ((1,), (0,)))


def _mm_nt(a, b):
    return _mm(a, b, ((1,), (1,)))


def _mm_tn(a, b):
    return _mm(a, b, ((0,), (0,)))


def _wkv_prepare(r, k, v, lw, a, k_k, k_a, r_k, nb, c):
    chains = range(len(r))
    rows = nb * c
    n = STACK_ROWS
    lane = lax.broadcasted_iota(jnp.int32, (1, V7X_LANES), 1)
    head0 = lane < RWKV_HEAD
    mk = (jnp.where(head0, 1.0, 0.0).astype(F32), jnp.where(head0, 0.0, 1.0).astype(F32))
    col_head0 = (lane // c) % HEADS_PER_VREG == 0
    cm = (jnp.where(col_head0, 1.0, 0.0).astype(F32), jnp.where(col_head0, 0.0, 1.0).astype(F32))

    def head_sum(t):
        s0 = jnp.sum(t * mk[0], axis=-1, keepdims=True)
        s1 = jnp.sum(t * mk[1], axis=-1, keepdims=True)
        return jnp.where(head0, s0, s1)

    def stack(t, masks):
        if c % (2 * V7X_SUBLANES) == 0:
            t = t.astype(BF16)
            masks = [m.astype(BF16) for m in masks]
        parts = []
        for b in range(nb):
            tb = t[b * c:(b + 1) * c]
            parts += [tb * masks[0], tb * masks[1]]
        return jnp.concatenate(parts, axis=0).astype(BF16)

    ri = lax.broadcasted_iota(jnp.int32, (rows, rows), 0)
    ci = lax.broadcasted_iota(jnp.int32, (rows, rows), 1)
    tri = jnp.where((ri // c == ci // c) & (ri >= ci), 1.0, 0.0).astype(BF16)
    lw_hi = [lw[p].astype(BF16) for p in chains]
    cum = [_mm_nn(tri, lw_hi[p]) + _mm_nn(tri, lw[p] - lw_hi[p].astype(F32)) for p in chains]

    kkr = [k[p] * k_k[p] for p in chains]
    kk = [kkr[p] * lax.rsqrt(jnp.maximum(head_sum(kkr[p] * kkr[p]), 1e-24)) for p in chains]
    kf = [k[p] * (1.0 + (a[p] - 1.0) * k_a[p]) for p in chains]
    g_incl = [jnp.exp(cum[p]) for p in chains]
    g_inv = [jnp.exp(-cum[p]) for p in chains]
    at = [(-kk[p] * jnp.exp(cum[p] - lw[p])).astype(BF16) for p in chains]
    rt = [(r[p] * g_incl[p]).astype(BF16) for p in chains]
    bt = [(kk[p] * a[p] * g_inv[p]).astype(BF16) for p in chains]
    kt = [(kf[p] * g_inv[p]).astype(BF16) for p in chains]
    vb = [v[p].astype(BF16) for p in chains]
    xa = [jnp.concatenate([at[p], rt[p]], axis=0) for p in chains]
    yb = [jnp.concatenate([stack(bt[p], mk), stack(kt[p], mk)], axis=0) for p in chains]
    v_s = [stack(vb[p], mk) for p in chains]

    qi = lax.broadcasted_iota(jnp.int32, (rows, n), 0)
    qj = lax.broadcasted_iota(jnp.int32, (rows, n), 1)
    same = qi // c == qj // (HEADS_PER_VREG * c)
    strict = same & (qj % c < qi % c)
    incl = same & (qj % c <= qi % c)
    eye = jnp.where(same & (qj % c == qi % c), 1.0, 0.0).astype(F32)

    gm = [_mm_nt(xa[p], yb[p]) for p in chains]
    a_ab = [jnp.where(strict, gm[p][:rows, :n], 0.0) for p in chains]
    a_kr = [jnp.concatenate([jnp.where(strict, gm[p][:rows, n:], 0.0),
                             jnp.where(incl, gm[p][rows:, n:], 0.0)], axis=0).astype(BF16) for p in chains]
    a_rb = [jnp.where(incl, gm[p][rows:, :n], 0.0).astype(BF16) for p in chains]

    tinv = [eye + a_ab[p] for p in chains]
    pw = [a_ab[p] for p in chains]
    if c > 2:
        pw = [_mm_nn(pw[p], stack(pw[p], cm)) for p in chains]
        span = 4
        while span < c:
            sq = [_mm_nn(pw[p], jnp.concatenate([stack(pw[p], cm), stack(tinv[p], cm)], axis=1)) for p in chains]
            tinv = [tinv[p] + sq[p][:, n:] for p in chains]
            pw = [sq[p][:, :n] for p in chains]
            span *= 2
        tinv = [tinv[p] + _mm_nn(pw[p], stack(tinv[p], cm)) for p in chains]

    av = [_mm_nn(a_kr[p], v_s[p]) for p in chains]
    bonus = [head_sum(r[p] * kf[p] * r_k[p]) * v[p] for p in chains]
    g_last = [[g_incl[p][(b + 1) * c - 1:(b + 1) * c] for b in range(nb)] for p in chains]
    return dict(xa=xa, at=at, rt=rt, bt=bt, kt=kt, vb=vb, tinv=[t.astype(BF16) for t in tinv], av=av, a_rb=a_rb,
                bonus=bonus, g_last=g_last, head_sum=head_sum, stack=lambda t: stack(t, mk))


def _wkv_apply(prep, sel, states, gn_g, gn_b, nb, c):
    rows = nb * c
    xa, at, rt, bt, kt, vb = (prep[key] for key in ("xa", "at", "rt", "bt", "kt", "vb"))
    stack = prep["stack"]
    if nb == 1:
        ah = [_mm_nt(xa[p], states[i][0]) for i, p in enumerate(sel)]
    else:
        ah = []
        for i, p in enumerate(sel):
            parts = [_mm_nt(jnp.concatenate([at[p][b * c:(b + 1) * c], rt[p][b * c:(b + 1) * c]], axis=0), states[i][b])
                     for b in range(nb)]
            ah.append(jnp.concatenate([t[:c] for t in parts] + [t[c:] for t in parts], axis=0))

    u = [_mm_nn(prep["tinv"][p], stack(ah[i][:rows] + prep["av"][p][:rows])) for i, p in enumerate(sel)]
    y = [ah[i][rows:] + prep["av"][p][rows:] + _mm_nn(prep["a_rb"][p], stack(u[i])) for i, p in enumerate(sel)]

    vi = lax.broadcasted_iota(jnp.int32, (V7X_LANES, V7X_LANES), 0)
    ki = lax.broadcasted_iota(jnp.int32, (V7X_LANES, V7X_LANES), 1)
    same_head = vi // RWKV_HEAD == ki // RWKV_HEAD
    new_states = []
    for i, p in enumerate(sel):
        ub = u[i].astype(BF16)
        per_batch = []
        for b in range(nb):
            sl = slice(b * c, (b + 1) * c)
            upd = _mm_tn(jnp.concatenate([ub[sl], vb[p][sl]], axis=0), jnp.concatenate([bt[p][sl], kt[p][sl]], axis=0))
            per_batch.append((states[i][b] + jnp.where(same_head, upd, 0.0)) * prep["g_last"][p][b])
        new_states.append(per_batch)

    head_sum = prep["head_sum"]
    inv_n = 1.0 / RWKV_HEAD
    out = []
    for i, p in enumerate(sel):
        mean = head_sum(y[i]) * inv_n
        yc = y[i] - mean
        var = head_sum(yc * yc) * inv_n
        out.append(yc * lax.rsqrt(var + GN_EPS) * gn_g[i] + gn_b[i] + prep["bonus"][p])
    return out, new_states


def _pair_lanes(p):
    return slice(p * V7X_LANES, (p + 1) * V7X_LANES)


def _state_to_blockdiag(s0, s1):
    z = jnp.zeros_like(s0)
    return jnp.concatenate([jnp.concatenate([s0, z], axis=1), jnp.concatenate([z, s1], axis=1)], axis=0)


def _wkv_prompt_kernel(r_ref, k_ref, v_ref, lw_ref, a_ref, kk_ref, ka_ref, rk_ref, gg_ref, gb_ref,
                       y_ref, sfin_ref, s_scr, *, n_batch, n_pairs, c, n_sub):
    ci = pl.program_id(0)

    @pl.when(ci == 0)
    def _():
        s_scr[...] = jnp.zeros_like(s_scr)

    seq_pairs = [(b, p) for b in range(n_batch) for p in range(n_pairs)]
    chains = [(j, b, p) for j in range(n_sub) for b, p in seq_pairs]
    data = [[ref[b, j * c:(j + 1) * c, _pair_lanes(p)] for j, b, p in chains]
            for ref in (r_ref, k_ref, v_ref, lw_ref, a_ref)]
    prm = [[ref[:, _pair_lanes(p)] for _, _, p in chains] for ref in (kk_ref, ka_ref, rk_ref)]
    prep = _wkv_prepare(*data, *prm, 1, c)
    gn = [[ref[:, _pair_lanes(p)] for _, p in seq_pairs] for ref in (gg_ref, gb_ref)]
    states = [[s_scr[i]] for i in range(len(seq_pairs))]
    for j in range(n_sub):
        sel = [j * len(seq_pairs) + i for i in range(len(seq_pairs))]
        yn, states = _wkv_apply(prep, sel, states, *gn, 1, c)
        for i, (b, p) in enumerate(seq_pairs):
            y_ref[b, j * c:(j + 1) * c, _pair_lanes(p)] = yn[i]
    for i in range(len(seq_pairs)):
        s_scr[i] = states[i][0]

    @pl.when(ci == pl.num_programs(0) - 1)
    def _():
        for i, (b, p) in enumerate(seq_pairs):
            s = s_scr[i]
            sfin_ref[b, 2 * p] = s[:RWKV_HEAD, :RWKV_HEAD]
            sfin_ref[b, 2 * p + 1] = s[RWKV_HEAD:, RWKV_HEAD:]


def _wkv_sample_kernel(r_ref, k_ref, v_ref, lw_ref, a_ref, kk_ref, ka_ref, rk_ref, gg_ref, gb_ref,
                       s0_ref, y_ref, sfin_ref, *, n_groups, n_pairs, nb, c):
    rows = nb * c
    chains = [(g, p) for g in range(n_groups) for p in range(n_pairs)]
    data = [[ref[g * rows:(g + 1) * rows, _pair_lanes(p)] for g, p in chains]
            for ref in (r_ref, k_ref, v_ref, lw_ref, a_ref)]
    prm = [[ref[:, _pair_lanes(p)] for _, p in chains] for ref in (kk_ref, ka_ref, rk_ref, gg_ref, gb_ref)]
    prep = _wkv_prepare(*data, *prm[:3], nb, c)
    states = [[_state_to_blockdiag(s0_ref[g * nb + b, 2 * p], s0_ref[g * nb + b, 2 * p + 1]) for b in range(nb)]
              for g, p in chains]
    yn, new = _wkv_apply(prep, list(range(len(chains))), states, *prm[3:], nb, c)
    for i, (g, p) in enumerate(chains):
        y_ref[g * rows:(g + 1) * rows, _pair_lanes(p)] = yn[i]
        for b in range(nb):
            sfin_ref[g * nb + b, 2 * p] = new[i][b][:RWKV_HEAD, :RWKV_HEAD]
            sfin_ref[g * nb + b, 2 * p + 1] = new[i][b][RWKV_HEAD:, RWKV_HEAD:]


def _wkv_prompt(r, k, v, lw, a, k_k, k_a, r_k, gn_g, gn_b, batch, seq):
    rows, d = r.shape
    heads = d // RWKV_HEAD
    n_pairs = heads // HEADS_PER_VREG
    c = PROMPT_CHUNK
    n_sub = PROMPT_CHUNKS_PER_STEP
    row_spec = pl.BlockSpec((batch, n_sub * c, d), lambda i: (0, i, 0))
    vec = _const_spec((1, d))
    kern = functools.partial(_wkv_prompt_kernel, n_batch=batch, n_pairs=n_pairs, c=c, n_sub=n_sub)
    seq3 = lambda t: t.reshape(batch, seq, d)
    y, s_fin = pl.pallas_call(
        kern,
        grid=(seq // (n_sub * c),),
        in_specs=[row_spec] * 5 + [vec] * 5,
        out_specs=[row_spec, pl.BlockSpec((batch, heads, RWKV_HEAD, RWKV_HEAD), lambda i: (0, 0, 0, 0))],
        out_shape=[jax.ShapeDtypeStruct((batch, seq, d), F32),
                   jax.ShapeDtypeStruct((batch, heads, RWKV_HEAD, RWKV_HEAD), F32)],
        scratch_shapes=[pltpu.VMEM((batch * n_pairs, V7X_LANES, V7X_LANES), F32)],
        compiler_params=_cparams("arbitrary"),
        name="wkv_prompt",
    )(seq3(r), seq3(k), seq3(v), seq3(lw), seq3(a), k_k, k_a, r_k, gn_g, gn_b)
    return y.reshape(rows, d), s_fin


def _wkv_sample(r, k, v, lw, a, k_k, k_a, r_k, gn_g, gn_b, state, batch, seq):
    rows, d = r.shape
    heads = d // RWKV_HEAD
    nb = WKV_ROWS // seq
    n_groups = SAMPLE_GROUPS_PER_STEP
    row_spec = pl.BlockSpec((n_groups * WKV_ROWS, d), lambda i: (i, 0))
    st_spec = pl.BlockSpec((n_groups * nb, heads, RWKV_HEAD, RWKV_HEAD), lambda i: (i, 0, 0, 0))
    vec = _const_spec((1, d))
    kern = functools.partial(_wkv_sample_kernel, n_groups=n_groups, n_pairs=heads // HEADS_PER_VREG, nb=nb, c=seq)
    return pl.pallas_call(
        kern,
        grid=(batch // (n_groups * nb),),
        in_specs=[row_spec] * 5 + [vec] * 5 + [st_spec],
        out_specs=[row_spec, st_spec],
        out_shape=[jax.ShapeDtypeStruct((rows, d), F32),
                   jax.ShapeDtypeStruct((batch, heads, RWKV_HEAD, RWKV_HEAD), F32)],
        compiler_params=_cparams("parallel"),
        name="wkv_sample",
    )(r, k, v, lw, a, k_k, k_a, r_k, gn_g, gn_b, state)


def _branch_out(y_ref, g_ref, x_ref, w_ref, lng_ref, lnb_ref):
    g = g_ref[...]
    h = (y_ref[...] * (g / (1.0 + jnp.exp(-g)))).astype(BF16)
    z = ALPHA * x_ref[...] + jnp.dot(h, w_ref[...], preferred_element_type=F32)
    mu = jnp.mean(z, axis=-1, keepdims=True)
    zc = z - mu
    var = jnp.mean(zc * zc, axis=-1, keepdims=True)
    return zc * lax.rsqrt(var + LN_EPS) * lng_ref[...] + lnb_ref[...]


def _post_kernel(y_ref, g_ref, x_ref, w_ref, lng_ref, lnb_ref, o_ref):
    o_ref[...] = _branch_out(y_ref, g_ref, x_ref, w_ref, lng_ref, lnb_ref)


def _post_project_kernel(y_ref, g_ref, x_ref, w_ref, lng_ref, lnb_ref, win_ref,
                         o_ref, q_ref, k_ref, v_ref, gate_ref, *, hq, hkv):
    x1 = _branch_out(y_ref, g_ref, x_ref, w_ref, lng_ref, lnb_ref)
    o_ref[...] = x1
    z = jnp.dot(x1.astype(BF16), win_ref[...], preferred_element_type=F32)
    q_ref[...] = (z[:, :hq] * (ATT_SCALE * LOG2E)).astype(q_ref.dtype)
    k_ref[...] = z[:, hq:hq + hkv]
    v_ref[...] = z[:, hq + hkv:hq + 2 * hkv]
    gate_ref[...] = z[:, hq + 2 * hkv:]


def _post(y, g, x2, w_out, ln_g, ln_b, tm, project=None):
    rows, d = x2.shape
    din = y.shape[1]
    in_spec = pl.BlockSpec((tm, din), lambda i: (i, 0))
    row_spec = pl.BlockSpec((tm, d), lambda i: (i, 0))
    in_specs = [in_spec, in_spec, row_spec, _const_spec((din, d)), _const_spec((1, d)), _const_spec((1, d))]
    args = (y, g, x2, w_out.astype(BF16), ln_g.reshape(1, d), ln_b.reshape(1, d))
    x_shape = jax.ShapeDtypeStruct((rows, d), F32)
    if project is None:
        return pl.pallas_call(
            _post_kernel, grid=(rows // tm,), in_specs=in_specs, out_specs=row_spec, out_shape=x_shape,
            compiler_params=_cparams("parallel"), name="branch_post",
        )(*args)
    w_in, q_dtype = project
    hq = ATT_KV_HEADS * ATT_GROUP * ATT_HEAD_DIM
    hkv = ATT_KV_HEADS * ATT_HEAD_DIM
    wide = pl.BlockSpec((tm, hq), lambda i: (i, 0))
    narrow = pl.BlockSpec((tm, hkv), lambda i: (i, 0))
    return pl.pallas_call(
        functools.partial(_post_project_kernel, hq=hq, hkv=hkv),
        grid=(rows // tm,),
        in_specs=in_specs + [_const_spec(w_in.shape)],
        out_specs=[row_spec, wide, narrow, narrow, wide],
        out_shape=[x_shape, jax.ShapeDtypeStruct((rows, hq), q_dtype), jax.ShapeDtypeStruct((rows, hkv), F32),
                   jax.ShapeDtypeStruct((rows, hkv), F32), jax.ShapeDtypeStruct((rows, hq), F32)],
        compiler_params=_cparams("parallel"),
        name="branch_post_project",
    )(*args, w_in.astype(BF16))


def _alibi_slope(h):
    n_heads = ATT_KV_HEADS * ATT_GROUP
    return LOG2E * 2.0 ** (-8.0 * (h + 1) / n_heads)


def _fold_masks(q_time):
    col = lax.broadcasted_iota(jnp.int32, q_time.shape, 1)
    from_prev = col > q_time
    dist = jnp.where(from_prev, q_time + WINDOW - col, q_time - col).astype(F32)
    return from_prev, dist


def _softmax_folded(sp, sc, slopes, sinks, from_prev, dist, dead):
    idx = range(len(sp))
    s = [jnp.where(from_prev, sp[i], sc[i]) - slopes[i] * dist for i in idx]
    if dead is not None:
        s = [jnp.where(dead, NEG, s[i]) for i in idx]
    m = [jnp.maximum(jnp.max(s[i], axis=-1, keepdims=True), sinks[i]) for i in idx]
    p = [jnp.exp2(s[i] - m[i]) for i in idx]
    den = [jnp.sum(p[i], axis=-1, keepdims=True) + jnp.exp2(sinks[i] - m[i]) for i in idx]
    pp = [jnp.where(from_prev, p[i], 0.0).astype(BF16) for i in idx]
    pc = [jnp.where(from_prev, 0.0, p[i]).astype(BF16) for i in idx]
    return pp, pc, den


_NT = (((1,), (1,)), ((), ()))


def _att_prompt_kernel(sink_ref, q_ref, kc_ref, vc_ref, kp_ref, vp_ref, gate_ref, x_ref, w_ref, lng_ref, lnb_ref,
                       y_ref, o_scr, *, n_blk, n_acc):
    step = pl.program_id(1)
    from_prev, dist = _fold_masks(lax.broadcasted_iota(jnp.int32, (WINDOW, WINDOW), 0))
    hd = ATT_HEAD_DIM
    heads = range(ATT_KV_HEADS * ATT_GROUP)
    slopes = [_alibi_slope(h) for h in heads]
    sinks = [sink_ref[h] * LOG2E for h in heads]
    row0 = pl.multiple_of((step % n_acc) * (n_blk * WINDOW), n_blk * WINDOW)
    items = [(blk, h) for blk in range(n_blk) for h in heads]
    kcat, vcat = [], []
    for blk in range(n_blk):
        rows = slice(blk * WINDOW, (blk + 1) * WINDOW)
        prev_of = (lambda ref_p, ref_c: ref_p) if blk == 0 else \
            (lambda ref_p, ref_c, blk=blk: ref_c.at[(blk - 1) * WINDOW:blk * WINDOW])
        kc, vc = [[jnp.concatenate([prev_of(prev, cur)[:, kvh * hd:(kvh + 1) * hd],
                                    cur[rows, kvh * hd:(kvh + 1) * hd]], axis=0).astype(BF16)
                   for kvh in range(ATT_KV_HEADS)] for prev, cur in ((kp_ref, kc_ref), (vp_ref, vc_ref))]
        kcat.append(kc)
        vcat.append(vc)
    s2 = [lax.dot_general(q_ref[blk * WINDOW:(blk + 1) * WINDOW, h * hd:(h + 1) * hd], kcat[blk][h // ATT_GROUP], _NT,
                          preferred_element_type=F32) for blk, h in items]
    n_h = len(heads)
    groups = [(range(n_h), from_prev & (step == 0))] + ([(range(n_h, len(items)), None)] if n_blk > 1 else [])
    parts = [_softmax_folded([s2[i][:, :WINDOW] for i in sel], [s2[i][:, WINDOW:] for i in sel],
                             [slopes[items[i][1]] for i in sel], [sinks[items[i][1]] for i in sel],
                             from_prev, dist, dead) for sel, dead in groups]
    pp, pc, den = [sum((list(part[j]) for part in parts), []) for j in range(3)]
    for i, (blk, h) in enumerate(items):
        o = jnp.dot(jnp.concatenate([pp[i], pc[i]], axis=1), vcat[blk][h // ATT_GROUP], preferred_element_type=F32)
        o_scr[pl.ds(row0 + blk * WINDOW, WINDOW), h * hd:(h + 1) * hd] = o / den[i]

    @pl.when((step % n_acc == n_acc - 1) & (step >= 0))
    def _():
        y_ref[...] = _branch_out(o_scr, gate_ref, x_ref, w_ref, lng_ref, lnb_ref)


def _att_prompt(q, k, v, gate, x2, sinks, w_out, ln_g, ln_b, batch, seq, tm):
    rows, hq = q.shape
    hkv = k.shape[1]
    d = x2.shape[1]
    n_blk = ATT_BLOCKS_PER_STEP
    steps = seq // (n_blk * WINDOW)
    n_acc = tm // (n_blk * WINDOW)
    cur = lambda b, n: (b * steps + n, 0)
    prev = lambda b, n: ((b * steps + n) * n_blk - jnp.minimum(n, 1), 0)
    tile = lambda b, n: ((b * steps + n) // n_acc, 0)
    return pl.pallas_call(
        functools.partial(_att_prompt_kernel, n_blk=n_blk, n_acc=n_acc),
        grid=(batch, steps),
        in_specs=[pl.BlockSpec(memory_space=pltpu.SMEM),
                  pl.BlockSpec((n_blk * WINDOW, hq), cur),
                  pl.BlockSpec((n_blk * WINDOW, hkv), cur), pl.BlockSpec((n_blk * WINDOW, hkv), cur),
                  pl.BlockSpec((WINDOW, hkv), prev), pl.BlockSpec((WINDOW, hkv), prev),
                  pl.BlockSpec((tm, hq), tile), pl.BlockSpec((tm, d), tile),
                  _const_spec((hq, d)), _const_spec((1, d)), _const_spec((1, d))],
        out_specs=pl.BlockSpec((tm, d), tile),
        out_shape=jax.ShapeDtypeStruct((rows, d), F32),
        scratch_shapes=[pltpu.VMEM((tm, hq), F32)],
        compiler_params=_cparams("parallel", "arbitrary"),
        name="att_prompt",
    )(sinks, q, k, v, k, v, gate, x2, w_out.astype(BF16), ln_g.reshape(1, d), ln_b.reshape(1, d))


def _att_sample_kernel(sink_ref, q_ref, kn_ref, vn_ref, ckt_ref, cvt_ref, o_ref, kwt_ref, vwt_ref, *, nb, t):
    m = ATT_GROUP * t
    grp = lax.broadcasted_iota(jnp.int32, (m, 1), 0) // t
    from_prev, dist = _fold_masks(lax.broadcasted_iota(jnp.int32, (m, WINDOW), 0) % t)
    hd = ATT_HEAD_DIM
    pad = jnp.zeros((WINDOW - t, hd), F32)
    items = [(kvh, b) for kvh in range(ATT_KV_HEADS) for b in range(nb)]
    idx = range(len(items))
    slopes, sinks = [], []
    for kvh in range(ATT_KV_HEADS):
        slope = jnp.zeros((m, 1), F32)
        sink = jnp.zeros((m, 1), F32)
        for g in range(ATT_GROUP):
            h = kvh * ATT_GROUP + g
            slope = jnp.where(grp == g, _alibi_slope(h), slope)
            sink = jnp.where(grp == g, sink_ref[h] * LOG2E, sink)
        slopes += [slope] * nb
        sinks += [sink] * nb
    qs = [jnp.concatenate([q_ref[b * t:(b + 1) * t, (kvh * ATT_GROUP + g) * hd:(kvh * ATT_GROUP + g + 1) * hd]
                           for g in range(ATT_GROUP)], axis=0).astype(BF16) for kvh, b in items]
    k_new = [kn_ref[b * t:(b + 1) * t, kvh * hd:(kvh + 1) * hd] for kvh, b in items]
    v_new = [vn_ref[b * t:(b + 1) * t, kvh * hd:(kvh + 1) * hd] for kvh, b in items]
    kt_old = [ckt_ref[b, kvh] for kvh, b in items]
    vt_old = [cvt_ref[b, kvh] for kvh, b in items]

    sp = [jnp.dot(qs[i], kt_old[i].astype(BF16), preferred_element_type=F32) for i in idx]
    sc = [lax.dot_general(qs[i], jnp.concatenate([k_new[i], pad], axis=0).astype(BF16), _NT,
                          preferred_element_type=F32) for i in idx]
    pp, pc, den = _softmax_folded(sp, sc, slopes, sinks, from_prev, dist, None)
    o = [(lax.dot_general(pp[i], vt_old[i].astype(BF16), _NT, preferred_element_type=F32)
          + jnp.dot(pc[i], jnp.concatenate([v_new[i], pad], axis=0).astype(BF16), preferred_element_type=F32))
         / den[i] for i in idx]
    for i, (kvh, b) in enumerate(items):
        for g in range(ATT_GROUP):
            h = kvh * ATT_GROUP + g
            o_ref[b * t:(b + 1) * t, h * hd:(h + 1) * hd] = o[i][g * t:(g + 1) * t]

    lane = lax.broadcasted_iota(jnp.int32, (hd, WINDOW), 1)
    for new, old, out_ref in ((k_new, kt_old, kwt_ref), (v_new, vt_old, vwt_ref)):
        new_t = [jnp.concatenate([pad, new[i]], axis=0).T for i in idx]
        for i, (kvh, b) in enumerate(items):
            out_ref[b, kvh] = jnp.where(lane >= WINDOW - t, new_t[i], pltpu.roll(old[i], WINDOW - t, axis=1))


def _att_sample(q, k, v, cache_kt, cache_vt, sinks, batch, t, nb):
    rows, hq = q.shape
    hkv = k.shape[1]
    row = lambda i: (i, 0)
    cache_spec = pl.BlockSpec((nb,) + cache_kt.shape[1:], lambda i: (i, 0, 0, 0))
    cache_shape = jax.ShapeDtypeStruct(cache_kt.shape, F32)
    return pl.pallas_call(
        functools.partial(_att_sample_kernel, nb=nb, t=t),
        grid=(batch // nb,),
        in_specs=[pl.BlockSpec(memory_space=pltpu.SMEM),
                  pl.BlockSpec((nb * t, hq), row), pl.BlockSpec((nb * t, hkv), row),
                  pl.BlockSpec((nb * t, hkv), row), cache_spec, cache_spec],
        out_specs=[pl.BlockSpec((nb * t, hq), row), cache_spec, cache_spec],
        out_shape=[jax.ShapeDtypeStruct((rows, hq), F32), cache_shape, cache_shape],
        compiler_params=_cparams("parallel"),
        name="att_sample",
    )(sinks, q, k, v, cache_kt, cache_vt)


ROW_TILE = 512
ATT_BLOCKS_PER_STEP = 4


def _row_tile(rows):
    return ROW_TILE if rows % ROW_TILE == 0 else rows


def kernel(x_prompt, x_sample, state_wkv, state_shift, cache_win_k, cache_win_v, ln_g, ln_b, rwkv_mu, rwkv_w_in, rwkv_w0, rwkv_w1, rwkv_w2, rwkv_a0, rwkv_a1, rwkv_a2, rwkv_k_k, rwkv_k_a, rwkv_r_k, rwkv_gn_g, rwkv_gn_b, rwkv_w_out, att_w_in, att_sinks, att_w_out):
    bp, tp, d = x_prompt.shape
    bs, ts, _ = x_sample.shape
    assert tp % (PROMPT_CHUNK * PROMPT_CHUNKS_PER_STEP) == 0 and tp % ROW_TILE == 0 and ROW_TILE % (ATT_BLOCKS_PER_STEP * WINDOW) == 0 and WKV_ROWS % ts == 0 and bs % (SAMPLE_GROUPS_PER_STEP * WKV_ROWS // ts) == 0
    assert cache_win_k.shape[2] == WINDOW
    heads = d // RWKV_HEAD
    hkv = ATT_KV_HEADS * ATT_HEAD_DIM

    j = 0
    vec = lambda t: t.reshape(1, d)
    pre_w = (rwkv_mu[j], rwkv_w_in[j], rwkv_w0[j], rwkv_w1[j], rwkv_w2[j], rwkv_a0[j], rwkv_a1[j], rwkv_a2[j])
    head_w = (vec(rwkv_k_k[j]), vec(rwkv_k_a[j]), vec(rwkv_r_k[j]), vec(rwkv_gn_g[j]), vec(rwkv_gn_b[j]))

    xp2 = x_prompt.reshape(bp * tp, d)
    xs2 = x_sample.reshape(bs * ts, d)

    r, k, v, g, lw, a = _rwkv_pre(x_prompt, jnp.zeros((bp, d), F32), *pre_w, tm=_row_tile(bp * tp))
    yp, wkv_p = _wkv_prompt(r, k, v, lw, a, *head_w, batch=bp, seq=tp)
    xp1, q_p, k_p, v_p, gate_p = _post(yp, g, xp2, rwkv_w_out[j], ln_g[0], ln_b[0], tm=_row_tile(bp * tp),
                                       project=(att_w_in[j], BF16))

    r, k, v, g, lw, a = _rwkv_pre(x_sample, state_shift[j], *pre_w, tm=_row_tile(bs * ts))
    ys, wkv_s = _wkv_sample(r, k, v, lw, a, *head_w, state_wkv[j], batch=bs, seq=ts)
    xs1, q_s, k_s, v_s, gate_s = _post(ys, g, xs2, rwkv_w_out[j], ln_g[0], ln_b[0], tm=_row_tile(bs * ts),
                                       project=(att_w_in[j], F32))

    y_prompt = _att_prompt(q_p, k_p, v_p, gate_p, xp1, att_sinks[j], att_w_out[j], ln_g[1], ln_b[1],
                           batch=bp, seq=tp, tm=ROW_TILE).reshape(bp, tp, d)
    win_shape = (bp, WINDOW, ATT_KV_HEADS, ATT_HEAD_DIM)
    win_k_p = k_p.reshape(bp, tp, hkv)[:, tp - WINDOW:].reshape(win_shape)
    win_v_p = v_p.reshape(bp, tp, hkv)[:, tp - WINDOW:].reshape(win_shape)

    to_t = lambda c: jnp.transpose(c, (0, 2, 3, 1))
    o, win_k_s, win_v_s = _att_sample(q_s, k_s, v_s, to_t(cache_win_k[j]), to_t(cache_win_v[j]), att_sinks[j],
                                      batch=bs, t=ts, nb=WKV_ROWS // ts)
    y_sample = _post(o, gate_s, xs1, att_w_out[j], ln_g[1], ln_b[1], tm=_row_tile(bs * ts)).reshape(bs, ts, d)
    win_k_s = jnp.transpose(win_k_s, (0, 3, 1, 2))
    win_v_s = jnp.transpose(win_v_s, (0, 3, 1, 2))

    return (y_prompt, y_sample,
            wkv_p[None], x_prompt[:, -1][None], win_k_p[None], win_v_p[None],
            wkv_s[None], x_sample[:, -1][None], win_k_s[None], win_v_s[None])
```

```python
import functools

import jax
import jax.numpy as jnp
from jax import lax
from jax.experimental import pallas as pl
from jax.experimental.pallas import tpu as pltpu

F32 = jnp.float32
BF16 = jnp.bfloat16

RWKV_HEAD = 64
ATT_HEAD_DIM = 64
ATT_KV_HEADS = 4
ATT_GROUP = 4
WINDOW = 128
ATT_SCALE = ATT_HEAD_DIM ** -0.5
LOG2E = 1.4426950408889634
NEG = -1e30
DEPTH = 2
ALPHA = (2 * DEPTH) ** 0.25
LN_EPS = 1e-5
GN_EPS = 64e-5

V7X_LANES = 128
V7X_SUBLANES = 8
V7X_VMEM_LIMIT_BYTES = 56 * 1024 * 1024

HEADS_PER_VREG = V7X_LANES // RWKV_HEAD
STACK_ROWS = 128
WKV_ROWS = STACK_ROWS // HEADS_PER_VREG
PROMPT_CHUNK = WKV_ROWS
PROMPT_CHUNKS_PER_STEP = 4
SAMPLE_GROUPS_PER_STEP = 2


def _cparams(*sem):
    return pltpu.CompilerParams(dimension_semantics=sem, vmem_limit_bytes=V7X_VMEM_LIMIT_BYTES)


def _const_spec(shape):
    nd = len(shape)
    return pl.BlockSpec(shape, lambda *_: (0,) * nd, pipeline_mode=pl.Buffered(1))


def _rwkv_pre_kernel(x_ref, first_ref, halo_ref, mu_ref, win_ref, w0_ref, w1_ref, w2_ref, a0_ref, a1_ref, a2_ref,
                     r_ref, k_ref, v_ref, g_ref, lw_ref, a_ref, *, tiles_per_seq):
    x3 = x_ref[...]
    nseq, rows, d = x3.shape
    first = first_ref[...]
    if tiles_per_seq > 1:
        first = jnp.where(pl.program_id(0) % tiles_per_seq == 0, first, halo_ref[:, -1:, :])
    t_idx = lax.broadcasted_iota(jnp.int32, (1, rows, 1), 1)
    x_prev = jnp.where(t_idx == 0, first, pltpu.roll(x3, 1, axis=1))
    x = x3.reshape(nseq * rows, d)
    xx = x_prev.reshape(nseq * rows, d) - x

    def mix(p):
        return (x + xx * mu_ref[p:p + 1, :]).astype(BF16)

    def proj(p):
        return jnp.dot(mix(p), win_ref[p], preferred_element_type=F32)

    hw = jnp.tanh(jnp.dot(mix(4), w1_ref[...], preferred_element_type=F32))
    ha = jnp.dot(mix(5), a1_ref[...], preferred_element_type=F32)
    wl = w0_ref[...] + jnp.dot(hw.astype(BF16), w2_ref[...], preferred_element_type=F32)
    al = a0_ref[...] + jnp.dot(ha.astype(BF16), a2_ref[...], preferred_element_type=F32)
    z = -wl
    softplus = jnp.maximum(z, 0.0) + jnp.log(1.0 + jnp.exp(-jnp.abs(z)))
    lw_ref[...] = -jnp.exp(-softplus - 0.5)
    a_ref[...] = 1.0 / (1.0 + jnp.exp(-al))
    r_ref[...] = proj(0)
    k_ref[...] = proj(1)
    v_ref[...] = proj(2)
    g_ref[...] = proj(3)


def _rwkv_pre(x3, first, mu, w_in, w0, w1, w2, a0, a1, a2, tm):
    nseq, seq, d = x3.shape
    rows = nseq * seq
    lora = w1.shape[1]
    if seq >= tm:
        tps = seq // tm
        x_spec = pl.BlockSpec((1, tm, d), lambda i: (i // tps, i % tps, 0))
        first_spec = pl.BlockSpec((1, 1, d), lambda i: (i // tps, 0, 0))
        halo_spec = pl.BlockSpec(
            (1, V7X_SUBLANES, d),
            lambda i: (i // tps, jnp.maximum((i % tps) * (tm // V7X_SUBLANES) - 1, 0), 0))
    else:
        tps = 1
        x_spec = pl.BlockSpec((tm // seq, seq, d), lambda i: (i, 0, 0))
        first_spec = pl.BlockSpec((tm // seq, 1, d), lambda i: (i, 0, 0))
        halo_spec = pl.BlockSpec((1, V7X_SUBLANES, d), lambda i: (0, 0, 0))
    row_spec = pl.BlockSpec((tm, d), lambda i: (i, 0))
    out = jax.ShapeDtypeStruct((rows, d), F32)
    return pl.pallas_call(
        functools.partial(_rwkv_pre_kernel, tiles_per_seq=tps),
        grid=(rows // tm,),
        in_specs=[x_spec, first_spec, halo_spec, _const_spec((6, d)), _const_spec((4, d, d)),
                  _const_spec((1, d)), _const_spec((d, lora)), _const_spec((lora, d)), _const_spec((1, d)),
                  _const_spec((d, lora)), _const_spec((lora, d))],
        out_specs=[row_spec] * 6,
        out_shape=[out] * 6,
        compiler_params=_cparams("parallel"),
        name="rwkv_pre",
    )(x3, first.reshape(nseq, 1, d), x3, mu, w_in.astype(BF16), w0.reshape(1, d), w1.astype(BF16),
      w2.astype(BF16), a0.reshape(1, d), a1.astype(BF16), a2.astype(BF16))


def _mm(a, b, dims):
    return lax.dot_general(a.astype(BF16), b.astype(BF16), (dims, ((), ())), preferred_element_type=F32)


def _mm_nn(a, b):
    return _mm(a, b, ((1,), (0,)))


def _mm_nt(a, b):
    return _mm(a, b, ((1,), (1,)))


def _mm_tn(a, b):
    return _mm(a, b, ((0,), (0,)))


def _wkv_prepare(r, k, v, lw, a, k_k, k_a, r_k, nb, c):
    chains = range(len(r))
    rows = nb * c
    n = STACK_ROWS
    lane = lax.broadcasted_iota(jnp.int32, (1, V7X_LANES), 1)
    head0 = lane < RWKV_HEAD
    mk = (jnp.where(head0, 1.0, 0.0).astype(F32), jnp.where(head0, 0.0, 1.0).astype(F32))
    col_head0 = (lane // c) % HEADS_PER_VREG == 0
    cm = (jnp.where(col_head0, 1.0, 0.0).astype(F32), jnp.where(col_head0, 0.0, 1.0).astype(F32))

    def head_sum(t):
        s0 = jnp.sum(t * mk[0], axis=-1, keepdims=True)
        s1 = jnp.sum(t * mk[1], axis=-1, keepdims=True)
        return jnp.where(head0, s0, s1)

    def stack(t, masks):
        if c % (2 * V7X_SUBLANES) == 0:
            t = t.astype(BF16)
            masks = [m.astype(BF16) for m in masks]
        parts = []
        for b in range(nb):
            tb = t[b * c:(b + 1) * c]
            parts += [tb * masks[0], tb * masks[1]]
        return jnp.concatenate(parts, axis=0).astype(BF16)

    ri = lax.broadcasted_iota(jnp.int32, (rows, rows), 0)
    ci = lax.broadcasted_iota(jnp.int32, (rows, rows), 1)
    tri = jnp.where((ri // c == ci // c) & (ri >= ci), 1.0, 0.0).astype(BF16)
    lw_hi = [lw[p].astype(BF16) for p in chains]
    cum = [_mm_nn(tri, lw_hi[p]) + _mm_nn(tri, lw[p] - lw_hi[p].astype(F32)) for p in chains]

    kkr = [k[p] * k_k[p] for p in chains]
    kk = [kkr[p] * lax.rsqrt(jnp.maximum(head_sum(kkr[p] * kkr[p]), 1e-24)) for p in chains]
    kf = [k[p] * (1.0 + (a[p] - 1.0) * k_a[p]) for p in chains]
    g_incl = [jnp.exp(cum[p]) for p in chains]
    g_inv = [jnp.exp(-cum[p]) for p in chains]
    at = [(-kk[p] * jnp.exp(cum[p] - lw[p])).astype(BF16) for p in chains]
    rt = [(r[p] * g_incl[p]).astype(BF16) for p in chains]
    bt = [(kk[p] * a[p] * g_inv[p]).astype(BF16) for p in chains]
    kt = [(kf[p] * g_inv[p]).astype(BF16) for p in chains]
    vb = [v[p].astype(BF16) for p in chains]
    xa = [jnp.concatenate([at[p], rt[p]], axis=0) for p in chains]
    yb = [jnp.concatenate([stack(bt[p], mk), stack(kt[p], mk)], axis=0) for p in chains]
    v_s = [stack(vb[p], mk) for p in chains]

    qi = lax.broadcasted_iota(jnp.int32, (rows, n), 0)
    qj = lax.broadcasted_iota(jnp.int32, (rows, n), 1)
    same = qi // c == qj // (HEADS_PER_VREG * c)
    strict = same & (qj % c < qi % c)
    incl = same & (qj % c <= qi % c)
    eye = jnp.where(same & (qj % c == qi % c), 1.0, 0.0).astype(F32)

    gm = [_mm_nt(xa[p], yb[p]) for p in chains]
    a_ab = [jnp.where(strict, gm[p][:rows, :n], 0.0) for p in chains]
    a_kr = [jnp.concatenate([jnp.where(strict, gm[p][:rows, n:], 0.0),
                             jnp.where(incl, gm[p][rows:, n:], 0.0)], axis=0).astype(BF16) for p in chains]
    a_rb = [jnp.where(incl, gm[p][rows:, :n], 0.0).astype(BF16) for p in chains]

    tinv = [eye + a_ab[p] for p in chains]
    pw = [a_ab[p] for p in chains]
    if c > 2:
        pw = [_mm_nn(pw[p], stack(pw[p], cm)) for p in chains]
        span = 4
        while span < c:
            sq = [_mm_nn(pw[p], jnp.concatenate([stack(pw[p], cm), stack(tinv[p], cm)], axis=1)) for p in chains]
            tinv = [tinv[p] + sq[p][:, n:] for p in chains]
            pw = [sq[p][:, :n] for p in chains]
            span *= 2
        tinv = [tinv[p] + _mm_nn(pw[p], stack(tinv[p], cm)) for p in chains]

    av = [_mm_nn(a_kr[p], v_s[p]) for p in chains]
    bonus = [head_sum(r[p] * kf[p] * r_k[p]) * v[p] for p in chains]
    g_last = [[g_incl[p][(b + 1) * c - 1:(b + 1) * c] for b in range(nb)] for p in chains]
    return dict(xa=xa, at=at, rt=rt, bt=bt, kt=kt, vb=vb, tinv=[t.astype(BF16) for t in tinv], av=av, a_rb=a_rb,
                bonus=bonus, g_last=g_last, head_sum=head_sum, stack=lambda t: stack(t, mk))


def _wkv_apply(prep, sel, states, gn_g, gn_b, nb, c):
    rows = nb * c
    xa, at, rt, bt, kt, vb = (prep[key] for key in ("xa", "at", "rt", "bt", "kt", "vb"))
    stack = prep["stack"]
    if nb == 1:
        ah = [_mm_nt(xa[p], states[i][0]) for i, p in enumerate(sel)]
    else:
        ah = []
        for i, p in enumerate(sel):
            parts = [_mm_nt(jnp.concatenate([at[p][b * c:(b + 1) * c], rt[p][b * c:(b + 1) * c]], axis=0), states[i][b])
                     for b in range(nb)]
            ah.append(jnp.concatenate([t[:c] for t in parts] + [t[c:] for t in parts], axis=0))

    u = [_mm_nn(prep["tinv"][p], stack(ah[i][:rows] + prep["av"][p][:rows])) for i, p in enumerate(sel)]
    y = [ah[i][rows:] + prep["av"][p][rows:] + _mm_nn(prep["a_rb"][p], stack(u[i])) for i, p in enumerate(sel)]

    vi = lax.broadcasted_iota(jnp.int32, (V7X_LANES, V7X_LANES), 0)
    ki = lax.broadcasted_iota(jnp.int32, (V7X_LANES, V7X_LANES), 1)
    same_head = vi // RWKV_HEAD == ki // RWKV_HEAD
    new_states = []
    for i, p in enumerate(sel):
        ub = u[i].astype(BF16)
        per_batch = []
        for b in range(nb):
            sl = slice(b * c, (b + 1) * c)
            upd = _mm_tn(jnp.concatenate([ub[sl], vb[p][sl]], axis=0), jnp.concatenate([bt[p][sl], kt[p][sl]], axis=0))
            per_batch.append((states[i][b] + jnp.where(same_head, upd, 0.0)) * prep["g_last"][p][b])
        new_states.append(per_batch)

    head_sum = prep["head_sum"]
    inv_n = 1.0 / RWKV_HEAD
    out = []
    for i, p in enumerate(sel):
        mean = head_sum(y[i]) * inv_n
        yc = y[i] - mean
        var = head_sum(yc * yc) * inv_n
        out.append(yc * lax.rsqrt(var + GN_EPS) * gn_g[i] + gn_b[i] + prep["bonus"][p])
    return out, new_states


def _pair_lanes(p):
    return slice(p * V7X_LANES, (p + 1) * V7X_LANES)


def _state_to_blockdiag(s0, s1):
    z = jnp.zeros_like(s0)
    return jnp.concatenate([jnp.concatenate([s0, z], axis=1), jnp.concatenate([z, s1], axis=1)], axis=0)


def _wkv_prompt_kernel(r_ref, k_ref, v_ref, lw_ref, a_ref, kk_ref, ka_ref, rk_ref, gg_ref, gb_ref,
                       y_ref, sfin_ref, s_scr, *, n_batch, n_pairs, c, n_sub):
    ci = pl.program_id(0)

    @pl.when(ci == 0)
    def _():
        s_scr[...] = jnp.zeros_like(s_scr)

    seq_pairs = [(b, p) for b in range(n_batch) for p in range(n_pairs)]
    chains = [(j, b, p) for j in range(n_sub) for b, p in seq_pairs]
    data = [[ref[b, j * c:(j + 1) * c, _pair_lanes(p)] for j, b, p in chains]
            for ref in (r_ref, k_ref, v_ref, lw_ref, a_ref)]
    prm = [[ref[:, _pair_lanes(p)] for _, _, p in chains] for ref in (kk_ref, ka_ref, rk_ref)]
    prep = _wkv_prepare(*data, *prm, 1, c)
    gn = [[ref[:, _pair_lanes(p)] for _, p in seq_pairs] for ref in (gg_ref, gb_ref)]
    states = [[s_scr[i]] for i in range(len(seq_pairs))]
    for j in range(n_sub):
        sel = [j * len(seq_pairs) + i for i in range(len(seq_pairs))]
        yn, states = _wkv_apply(prep, sel, states, *gn, 1, c)
        for i, (b, p) in enumerate(seq_pairs):
            y_ref[b, j * c:(j + 1) * c, _pair_lanes(p)] = yn[i]
    for i in range(len(seq_pairs)):
        s_scr[i] = states[i][0]

    @pl.when(ci == pl.num_programs(0) - 1)
    def _():
        for i, (b, p) in enumerate(seq_pairs):
            s = s_scr[i]
            sfin_ref[b, 2 * p] = s[:RWKV_HEAD, :RWKV_HEAD]
            sfin_ref[b, 2 * p + 1] = s[RWKV_HEAD:, RWKV_HEAD:]


def _wkv_sample_kernel(r_ref, k_ref, v_ref, lw_ref, a_ref, kk_ref, ka_ref, rk_ref, gg_ref, gb_ref,
                       s0_ref, y_ref, sfin_ref, *, n_groups, n_pairs, nb, c):
    rows = nb * c
    chains = [(g, p) for g in range(n_groups) for p in range(n_pairs)]
    data = [[ref[g * rows:(g + 1) * rows, _pair_lanes(p)] for g, p in chains]
            for ref in (r_ref, k_ref, v_ref, lw_ref, a_ref)]
    prm = [[ref[:, _pair_lanes(p)] for _, p in chains] for ref in (kk_ref, ka_ref, rk_ref, gg_ref, gb_ref)]
    prep = _wkv_prepare(*data, *prm[:3], nb, c)
    states = [[_state_to_blockdiag(s0_ref[g * nb + b, 2 * p], s0_ref[g * nb + b, 2 * p + 1]) for b in range(nb)]
              for g, p in chains]
    yn, new = _wkv_apply(prep, list(range(len(chains))), states, *prm[3:], nb, c)
    for i, (g, p) in enumerate(chains):
        y_ref[g * rows:(g + 1) * rows, _pair_lanes(p)] = yn[i]
        for b in range(nb):
            sfin_ref[g * nb + b, 2 * p] = new[i][b][:RWKV_HEAD, :RWKV_HEAD]
            sfin_ref[g * nb + b, 2 * p + 1] = new[i][b][RWKV_HEAD:, RWKV_HEAD:]


def _wkv_prompt(r, k, v, lw, a, k_k, k_a, r_k, gn_g, gn_b, batch, seq):
    rows, d = r.shape
    heads = d // RWKV_HEAD
    n_pairs = heads // HEADS_PER_VREG
    c = PROMPT_CHUNK
    n_sub = PROMPT_CHUNKS_PER_STEP
    row_spec = pl.BlockSpec((batch, n_sub * c, d), lambda i: (0, i, 0))
    vec = _const_spec((1, d))
    kern = functools.partial(_wkv_prompt_kernel, n_batch=batch, n_pairs=n_pairs, c=c, n_sub=n_sub)
    seq3 = lambda t: t.reshape(batch, seq, d)
    y, s_fin = pl.pallas_call(
        kern,
        grid=(seq // (n_sub * c),),
        in_specs=[row_spec] * 5 + [vec] * 5,
        out_specs=[row_spec, pl.BlockSpec((batch, heads, RWKV_HEAD, RWKV_HEAD), lambda i: (0, 0, 0, 0))],
        out_shape=[jax.ShapeDtypeStruct((batch, seq, d), F32),
                   jax.ShapeDtypeStruct((batch, heads, RWKV_HEAD, RWKV_HEAD), F32)],
        scratch_shapes=[pltpu.VMEM((batch * n_pairs, V7X_LANES, V7X_LANES), F32)],
        compiler_params=_cparams("arbitrary"),
        name="wkv_prompt",
    )(seq3(r), seq3(k), seq3(v), seq3(lw), seq3(a), k_k, k_a, r_k, gn_g, gn_b)
    return y.reshape(rows, d), s_fin


def _wkv_sample(r, k, v, lw, a, k_k, k_a, r_k, gn_g, gn_b, state, batch, seq):
    rows, d = r.shape
    heads = d // RWKV_HEAD
    nb = WKV_ROWS // seq
    n_groups = SAMPLE_GROUPS_PER_STEP
    row_spec = pl.BlockSpec((n_groups * WKV_ROWS, d), lambda i: (i, 0))
    st_spec = pl.BlockSpec((n_groups * nb, heads, RWKV_HEAD, RWKV_HEAD), lambda i: (i, 0, 0, 0))
    vec = _const_spec((1, d))
    kern = functools.partial(_wkv_sample_kernel, n_groups=n_groups, n_pairs=heads // HEADS_PER_VREG, nb=nb, c=seq)
    return pl.pallas_call(
        kern,
        grid=(batch // (n_groups * nb),),
        in_specs=[row_spec] * 5 + [vec] * 5 + [st_spec],
        out_specs=[row_spec, st_spec],
        out_shape=[jax.ShapeDtypeStruct((rows, d), F32),
                   jax.ShapeDtypeStruct((batch, heads, RWKV_HEAD, RWKV_HEAD), F32)],
        compiler_params=_cparams("parallel"),
        name="wkv_sample",
    )(r, k, v, lw, a, k_k, k_a, r_k, gn_g, gn_b, state)


def _branch_out(y_ref, g_ref, x_ref, w_ref, lng_ref, lnb_ref):
    g = g_ref[...]
    h = (y_ref[...] * (g / (1.0 + jnp.exp(-g)))).astype(BF16)
    z = ALPHA * x_ref[...] + jnp.dot(h, w_ref[...], preferred_element_type=F32)
    mu = jnp.mean(z, axis=-1, keepdims=True)
    zc = z - mu
    var = jnp.mean(zc * zc, axis=-1, keepdims=True)
    return zc * lax.rsqrt(var + LN_EPS) * lng_ref[...] + lnb_ref[...]


def _post_kernel(y_ref, g_ref, x_ref, w_ref, lng_ref, lnb_ref, o_ref):
    o_ref[...] = _branch_out(y_ref, g_ref, x_ref, w_ref, lng_ref, lnb_ref)


def _post_project_kernel(y_ref, g_ref, x_ref, w_ref, lng_ref, lnb_ref, win_ref,
                         o_ref, q_ref, k_ref, v_ref, gate_ref, *, hq, hkv):
    x1 = _branch_out(y_ref, g_ref, x_ref, w_ref, lng_ref, lnb_ref)
    o_ref[...] = x1
    z = jnp.dot(x1.astype(BF16), win_ref[...], preferred_element_type=F32)
    q_ref[...] = (z[:, :hq] * (ATT_SCALE * LOG2E)).astype(q_ref.dtype)
    k_ref[...] = z[:, hq:hq + hkv]
    v_ref[...] = z[:, hq + hkv:hq + 2 * hkv]
    gate_ref[...] = z[:, hq + 2 * hkv:]


def _post(y, g, x2, w_out, ln_g, ln_b, tm, project=None):
    rows, d = x2.shape
    din = y.shape[1]
    in_spec = pl.BlockSpec((tm, din), lambda i: (i, 0))
    row_spec = pl.BlockSpec((tm, d), lambda i: (i, 0))
    in_specs = [in_spec, in_spec, row_spec, _const_spec((din, d)), _const_spec((1, d)), _const_spec((1, d))]
    args = (y, g, x2, w_out.astype(BF16), ln_g.reshape(1, d), ln_b.reshape(1, d))
    x_shape = jax.ShapeDtypeStruct((rows, d), F32)
    if project is None:
        return pl.pallas_call(
            _post_kernel, grid=(rows // tm,), in_specs=in_specs, out_specs=row_spec, out_shape=x_shape,
            compiler_params=_cparams("parallel"), name="branch_post",
        )(*args)
    w_in, q_dtype = project
    hq = ATT_KV_HEADS * ATT_GROUP * ATT_HEAD_DIM
    hkv = ATT_KV_HEADS * ATT_HEAD_DIM
    wide = pl.BlockSpec((tm, hq), lambda i: (i, 0))
    narrow = pl.BlockSpec((tm, hkv), lambda i: (i, 0))
    return pl.pallas_call(
        functools.partial(_post_project_kernel, hq=hq, hkv=hkv),
        grid=(rows // tm,),
        in_specs=in_specs + [_const_spec(w_in.shape)],
        out_specs=[row_spec, wide, narrow, narrow, wide],
        out_shape=[x_shape, jax.ShapeDtypeStruct((rows, hq), q_dtype), jax.ShapeDtypeStruct((rows, hkv), F32),
                   jax.ShapeDtypeStruct((rows, hkv), F32), jax.ShapeDtypeStruct((rows, hq), F32)],
        compiler_params=_cparams("parallel"),
        name="branch_post_project",
    )(*args, w_in.astype(BF16))


def _alibi_slope(h):
    n_heads = ATT_KV_HEADS * ATT_GROUP
    return LOG2E * 2.0 ** (-8.0 * (h + 1) / n_heads)


def _fold_masks(q_time):
    col = lax.broadcasted_iota(jnp.int32, q_time.shape, 1)
    from_prev = col > q_time
    dist = jnp.where(from_prev, q_time + WINDOW - col, q_time - col).astype(F32)
    return from_prev, dist


def _softmax_folded(sp, sc, biases, sinks, from_prev, dead):
    idx = range(len(sp))
    s = [jnp.where(from_prev, sp[i], sc[i]) - biases[i] for i in idx]
    if dead is not None:
        s = [jnp.where(dead, NEG, s[i]) for i in idx]
    m = [jnp.maximum(jnp.max(s[i], axis=-1, keepdims=True), sinks[i]) for i in idx]
    p = [jnp.exp2(s[i] - m[i]) for i in idx]
    den = [jnp.sum(p[i], axis=-1, keepdims=True) + jnp.exp2(sinks[i] - m[i]) for i in idx]
    pp = [jnp.where(from_prev, p[i], 0.0).astype(BF16) for i in idx]
    pc = [jnp.where(from_prev, 0.0, p[i]).astype(BF16) for i in idx]
    return pp, pc, den


_NT = (((1,), (1,)), ((), ()))


def _att_prompt_kernel(sink_ref, q_ref, kc_ref, vc_ref, kp_ref, vp_ref, gate_ref, x_ref, w_ref, lng_ref, lnb_ref,
                       y_ref, o_scr, *, n_blk, n_acc):
    step = pl.program_id(1)
    from_prev, dist = _fold_masks(lax.broadcasted_iota(jnp.int32, (WINDOW, WINDOW), 0))
    hd = ATT_HEAD_DIM
    heads = range(ATT_KV_HEADS * ATT_GROUP)
    biases = [_alibi_slope(h) * dist for h in heads]
    sinks = [sink_ref[h] * LOG2E for h in heads]
    row0 = pl.multiple_of((step % n_acc) * (n_blk * WINDOW), n_blk * WINDOW)
    items = [(blk, h) for blk in range(n_blk) for h in heads]
    kcat, vcat = [], []
    for blk in range(n_blk):
        rows = slice(blk * WINDOW, (blk + 1) * WINDOW)
        prev_of = (lambda ref_p, ref_c: ref_p) if blk == 0 else \
            (lambda ref_p, ref_c, blk=blk: ref_c.at[(blk - 1) * WINDOW:blk * WINDOW])
        kc, vc = [[jnp.concatenate([prev_of(prev, cur)[:, kvh * hd:(kvh + 1) * hd],
                                    cur[rows, kvh * hd:(kvh + 1) * hd]], axis=0).astype(BF16)
                   for kvh in range(ATT_KV_HEADS)] for prev, cur in ((kp_ref, kc_ref), (vp_ref, vc_ref))]
        kcat.append(kc)
        vcat.append(vc)
    s2 = [lax.dot_general(q_ref[blk * WINDOW:(blk + 1) * WINDOW, h * hd:(h + 1) * hd], kcat[blk][h // ATT_GROUP], _NT,
                          preferred_element_type=F32) for blk, h in items]
    n_h = len(heads)
    groups = [(range(n_h), from_prev & (step == 0))] + ([(range(n_h, len(items)), None)] if n_blk > 1 else [])
    parts = [_softmax_folded([s2[i][:, :WINDOW] for i in sel], [s2[i][:, WINDOW:] for i in sel],
                             [biases[items[i][1]] for i in sel], [sinks[items[i][1]] for i in sel],
                             from_prev, dead) for sel, dead in groups]
    pp, pc, den = [sum((list(part[j]) for part in parts), []) for j in range(3)]
    for i, (blk, h) in enumerate(items):
        o = jnp.dot(jnp.concatenate([pp[i], pc[i]], axis=1), vcat[blk][h // ATT_GROUP], preferred_element_type=F32)
        o_scr[pl.ds(row0 + blk * WINDOW, WINDOW), h * hd:(h + 1) * hd] = o / den[i]

    @pl.when((step % n_acc == n_acc - 1) & (step >= 0))
    def _():
        y_ref[...] = _branch_out(o_scr, gate_ref, x_ref, w_ref, lng_ref, lnb_ref)


def _att_prompt(q, k, v, gate, x2, sinks, w_out, ln_g, ln_b, batch, seq, tm):
    rows, hq = q.shape
    hkv = k.shape[1]
    d = x2.shape[1]
    n_blk = ATT_BLOCKS_PER_STEP
    steps = seq // (n_blk * WINDOW)
    n_acc = tm // (n_blk * WINDOW)
    cur = lambda b, n: (b * steps + n, 0)
    prev = lambda b, n: ((b * steps + n) * n_blk - jnp.minimum(n, 1), 0)
    tile = lambda b, n: ((b * steps + n) // n_acc, 0)
    return pl.pallas_call(
        functools.partial(_att_prompt_kernel, n_blk=n_blk, n_acc=n_acc),
        grid=(batch, steps),
        in_specs=[pl.BlockSpec(memory_space=pltpu.SMEM),
                  pl.BlockSpec((n_blk * WINDOW, hq), cur),
                  pl.BlockSpec((n_blk * WINDOW, hkv), cur), pl.BlockSpec((n_blk * WINDOW, hkv), cur),
                  pl.BlockSpec((WINDOW, hkv), prev), pl.BlockSpec((WINDOW, hkv), prev),
                  pl.BlockSpec((tm, hq), tile), pl.BlockSpec((tm, d), tile),
                  _const_spec((hq, d)), _const_spec((1, d)), _const_spec((1, d))],
        out_specs=pl.BlockSpec((tm, d), tile),
        out_shape=jax.ShapeDtypeStruct((rows, d), F32),
        scratch_shapes=[pltpu.VMEM((tm, hq), F32)],
        compiler_params=_cparams("parallel", "arbitrary"),
        name="att_prompt",
    )(sinks, q, k, v, k, v, gate, x2, w_out.astype(BF16), ln_g.reshape(1, d), ln_b.reshape(1, d))


def _att_sample_kernel(sink_ref, q_ref, kn_ref, vn_ref, ckt_ref, cvt_ref, o_ref, kwt_ref, vwt_ref, *, nb, t):
    m = ATT_GROUP * t
    grp = lax.broadcasted_iota(jnp.int32, (m, 1), 0) // t
    from_prev, dist = _fold_masks(lax.broadcasted_iota(jnp.int32, (m, WINDOW), 0) % t)
    hd = ATT_HEAD_DIM
    pad = jnp.zeros((WINDOW - t, hd), F32)
    items = [(kvh, b) for kvh in range(ATT_KV_HEADS) for b in range(nb)]
    idx = range(len(items))
    biases, sinks = [], []
    for kvh in range(ATT_KV_HEADS):
        slope = jnp.zeros((m, 1), F32)
        sink = jnp.zeros((m, 1), F32)
        for g in range(ATT_GROUP):
            h = kvh * ATT_GROUP + g
            slope = jnp.where(grp == g, _alibi_slope(h), slope)
            sink = jnp.where(grp == g, sink_ref[h] * LOG2E, sink)
        biases += [slope * dist] * nb
        sinks += [sink] * nb
    qs = [jnp.concatenate([q_ref[b * t:(b + 1) * t, (kvh * ATT_GROUP + g) * hd:(kvh * ATT_GROUP + g + 1) * hd]
                           for g in range(ATT_GROUP)], axis=0).astype(BF16) for kvh, b in items]
    k_new = [kn_ref[b * t:(b + 1) * t, kvh * hd:(kvh + 1) * hd] for kvh, b in items]
    v_new = [vn_ref[b * t:(b + 1) * t, kvh * hd:(kvh + 1) * hd] for kvh, b in items]
    kt_old = [ckt_ref[b, kvh] for kvh, b in items]
    vt_old = [cvt_ref[b, kvh] for kvh, b in items]

    sp = [jnp.dot(qs[i], kt_old[i].astype(BF16), preferred_element_type=F32) for i in idx]
    sc = [lax.dot_general(qs[i], jnp.concatenate([k_new[i], pad], axis=0).astype(BF16), _NT,
                          preferred_element_type=F32) for i in idx]
    pp, pc, den = _softmax_folded(sp, sc, biases, sinks, from_prev, None)
    o = [(lax.dot_general(pp[i], vt_old[i].astype(BF16), _NT, preferred_element_type=F32)
          + jnp.dot(pc[i], jnp.concatenate([v_new[i], pad], axis=0).astype(BF16), preferred_element_type=F32))
         / den[i] for i in idx]
    for i, (kvh, b) in enumerate(items):
        for g in range(ATT_GROUP):
            h = kvh * ATT_GROUP + g
            o_ref[b * t:(b + 1) * t, h * hd:(h + 1) * hd] = o[i][g * t:(g + 1) * t]

    lane = lax.broadcasted_iota(jnp.int32, (hd, WINDOW), 1)
    for new, old, out_ref in ((k_new, kt_old, kwt_ref), (v_new, vt_old, vwt_ref)):
        new_t = [jnp.concatenate([pad, new[i]], axis=0).T for i in idx]
        for i, (kvh, b) in enumerate(items):
            out_ref[b, kvh] = jnp.where(lane >= WINDOW - t, new_t[i], pltpu.roll(old[i], WINDOW - t, axis=1))


def _att_sample(q, k, v, cache_kt, cache_vt, sinks, batch, t, nb):
    rows, hq = q.shape
    hkv = k.shape[1]
    row = lambda i: (i, 0)
    cache_spec = pl.BlockSpec((nb,) + cache_kt.shape[1:], lambda i: (i, 0, 0, 0))
    cache_shape = jax.ShapeDtypeStruct(cache_kt.shape, F32)
    return pl.pallas_call(
        functools.partial(_att_sample_kernel, nb=nb, t=t),
        grid=(batch // nb,),
        in_specs=[pl.BlockSpec(memory_space=pltpu.SMEM),
                  pl.BlockSpec((nb * t, hq), row), pl.BlockSpec((nb * t, hkv), row),
                  pl.BlockSpec((nb * t, hkv), row), cache_spec, cache_spec],
        out_specs=[pl.BlockSpec((nb * t, hq), row), cache_spec, cache_spec],
        out_shape=[jax.ShapeDtypeStruct((rows, hq), F32), cache_shape, cache_shape],
        compiler_params=_cparams("parallel"),
        name="att_sample",
    )(sinks, q, k, v, cache_kt, cache_vt)


ROW_TILE = 512
ATT_BLOCKS_PER_STEP = 4


def _row_tile(rows):
    return ROW_TILE if rows % ROW_TILE == 0 else rows


def kernel(x_prompt, x_sample, state_wkv, state_shift, cache_win_k, cache_win_v, ln_g, ln_b, rwkv_mu, rwkv_w_in, rwkv_w0, rwkv_w1, rwkv_w2, rwkv_a0, rwkv_a1, rwkv_a2, rwkv_k_k, rwkv_k_a, rwkv_r_k, rwkv_gn_g, rwkv_gn_b, rwkv_w_out, att_w_in, att_sinks, att_w_out):
    bp, tp, d = x_prompt.shape
    bs, ts, _ = x_sample.shape
    assert tp % (PROMPT_CHUNK * PROMPT_CHUNKS_PER_STEP) == 0 and tp % ROW_TILE == 0 and ROW_TILE % (ATT_BLOCKS_PER_STEP * WINDOW) == 0 and WKV_ROWS % ts == 0 and bs % (SAMPLE_GROUPS_PER_STEP * WKV_ROWS // ts) == 0
    assert cache_win_k.shape[2] == WINDOW
    heads = d // RWKV_HEAD
    hkv = ATT_KV_HEADS * ATT_HEAD_DIM

    j = 0
    vec = lambda t: t.reshape(1, d)
    pre_w = (rwkv_mu[j], rwkv_w_in[j], rwkv_w0[j], rwkv_w1[j], rwkv_w2[j], rwkv_a0[j], rwkv_a1[j], rwkv_a2[j])
    head_w = (vec(rwkv_k_k[j]), vec(rwkv_k_a[j]), vec(rwkv_r_k[j]), vec(rwkv_gn_g[j]), vec(rwkv_gn_b[j]))

    xp2 = x_prompt.reshape(bp * tp, d)
    xs2 = x_sample.reshape(bs * ts, d)

    r, k, v, g, lw, a = _rwkv_pre(x_prompt, jnp.zeros((bp, d), F32), *pre_w, tm=_row_tile(bp * tp))
    yp, wkv_p = _wkv_prompt(r, k, v, lw, a, *head_w, batch=bp, seq=tp)
    xp1, q_p, k_p, v_p, gate_p = _post(yp, g, xp2, rwkv_w_out[j], ln_g[0], ln_b[0], tm=_row_tile(bp * tp),
                                       project=(att_w_in[j], BF16))

    r, k, v, g, lw, a = _rwkv_pre(x_sample, state_shift[j], *pre_w, tm=_row_tile(bs * ts))
    ys, wkv_s = _wkv_sample(r, k, v, lw, a, *head_w, state_wkv[j], batch=bs, seq=ts)
    xs1, q_s, k_s, v_s, gate_s = _post(ys, g, xs2, rwkv_w_out[j], ln_g[0], ln_b[0], tm=_row_tile(bs * ts),
                                       project=(att_w_in[j], F32))

    y_prompt = _att_prompt(q_p, k_p, v_p, gate_p, xp1, att_sinks[j], att_w_out[j], ln_g[1], ln_b[1],
                           batch=bp, seq=tp, tm=ROW_TILE).reshape(bp, tp, d)
    win_shape = (bp, WINDOW, ATT_KV_HEADS, ATT_HEAD_DIM)
    win_k_p = k_p.reshape(bp, tp, hkv)[:, tp - WINDOW:].reshape(win_shape)
    win_v_p = v_p.reshape(bp, tp, hkv)[:, tp - WINDOW:].reshape(win_shape)

    to_t = lambda c: jnp.transpose(c, (0, 2, 3, 1))
    o, win_k_s, win_v_s = _att_sample(q_s, k_s, v_s, to_t(cache_win_k[j]), to_t(cache_win_v[j]), att_sinks[j],
                                      batch=bs, t=ts, nb=WKV_ROWS // ts)
    y_sample = _post(o, gate_s, xs1, att_w_out[j], ln_g[1], ln_b[1], tm=_row_tile(bs * ts)).reshape(bs, ts, d)
    win_k_s = jnp.transpose(win_k_s, (0, 3, 1, 2))
    win_v_s = jnp.transpose(win_v_s, (0, 3, 1, 2))

    return (y_prompt, y_sample,
            wkv_p[None], x_prompt[:, -1][None], win_k_p[None], win_v_p[None],
            wkv_s[None], x_sample[:, -1][None], win_k_s[None], win_v_s[None])
```

```python
import functools

import jax
import jax.numpy as jnp
from jax import lax
from jax.experimental import pallas as pl
from jax.experimental.pallas import tpu as pltpu

F32 = jnp.float32
BF16 = jnp.bfloat16

RWKV_HEAD = 64
ATT_HEAD_DIM = 64
ATT_KV_HEADS = 4
ATT_GROUP = 4
WINDOW = 128
ATT_SCALE = ATT_HEAD_DIM ** -0.5
LOG2E = 1.4426950408889634
NEG = -1e30
DEPTH = 2
ALPHA = (2 * DEPTH) ** 0.25
LN_EPS = 1e-5
GN_EPS = 64e-5

V7X_LANES = 128
V7X_SUBLANES = 8
V7X_VMEM_LIMIT_BYTES = 56 * 1024 * 1024

HEADS_PER_VREG = V7X_LANES // RWKV_HEAD
STACK_ROWS = 128
WKV_ROWS = STACK_ROWS // HEADS_PER_VREG
PROMPT_CHUNK = WKV_ROWS
PROMPT_CHUNKS_PER_STEP = 4
SAMPLE_GROUPS_PER_STEP = 2


def _cparams(*sem):
    return pltpu.CompilerParams(dimension_semantics=sem, vmem_limit_bytes=V7X_VMEM_LIMIT_BYTES)


def _const_spec(shape):
    nd = len(shape)
    return pl.BlockSpec(shape, lambda *_: (0,) * nd, pipeline_mode=pl.Buffered(1))


def _rwkv_pre_kernel(x_ref, first_ref, halo_ref, mu_ref, win_ref, w0_ref, w1_ref, w2_ref, a0_ref, a1_ref, a2_ref,
                     r_ref, k_ref, v_ref, g_ref, lw_ref, a_ref, *, tiles_per_seq):
    x3 = x_ref[...]
    nseq, rows, d = x3.shape
    first = first_ref[...]
    if tiles_per_seq > 1:
        first = jnp.where(pl.program_id(0) % tiles_per_seq == 0, first, halo_ref[:, -1:, :])
    t_idx = lax.broadcasted_iota(jnp.int32, (1, rows, 1), 1)
    x_prev = jnp.where(t_idx == 0, first, pltpu.roll(x3, 1, axis=1))
    x = x3.reshape(nseq * rows, d)
    xx = x_prev.reshape(nseq * rows, d) - x

    def mix(p):
        return (x + xx * mu_ref[p:p + 1, :]).astype(BF16)

    def proj(p):
        return jnp.dot(mix(p), win_ref[p], preferred_element_type=F32)

    r_ref[...] = proj(0)
    k_ref[...] = proj(1)
    v_ref[...] = proj(2)
    g_ref[...] = proj(3)
    hw = jnp.tanh(jnp.dot(mix(4), w1_ref[...], preferred_element_type=F32))
    wl = w0_ref[...] + jnp.dot(hw.astype(BF16), w2_ref[...], preferred_element_type=F32)
    z = -wl
    softplus = jnp.maximum(z, 0.0) + jnp.log(1.0 + jnp.exp(-jnp.abs(z)))
    lw_ref[...] = -jnp.exp(-softplus - 0.5)
    ha = jnp.dot(mix(5), a1_ref[...], preferred_element_type=F32)
    al = a0_ref[...] + jnp.dot(ha.astype(BF16), a2_ref[...], preferred_element_type=F32)
    a_ref[...] = 1.0 / (1.0 + jnp.exp(-al))


def _rwkv_pre(x3, first, mu, w_in, w0, w1, w2, a0, a1, a2, tm):
    nseq, seq, d = x3.shape
    rows = nseq * seq
    lora = w1.shape[1]
    if seq >= tm:
        tps = seq // tm
        x_spec = pl.BlockSpec((1, tm, d), lambda i: (i // tps, i % tps, 0))
        first_spec = pl.BlockSpec((1, 1, d), lambda i: (i // tps, 0, 0))
        halo_spec = pl.BlockSpec(
            (1, V7X_SUBLANES, d),
            lambda i: (i // tps, jnp.maximum((i % tps) * (tm // V7X_SUBLANES) - 1, 0), 0))
    else:
        tps = 1
        x_spec = pl.BlockSpec((tm // seq, seq, d), lambda i: (i, 0, 0))
        first_spec = pl.BlockSpec((tm // seq, 1, d), lambda i: (i, 0, 0))
        halo_spec = pl.BlockSpec((1, V7X_SUBLANES, d), lambda i: (0, 0, 0))
    row_spec = pl.BlockSpec((tm, d), lambda i: (i, 0))
    out = jax.ShapeDtypeStruct((rows, d), F32)
    return pl.pallas_call(
        functools.partial(_rwkv_pre_kernel, tiles_per_seq=tps),
        grid=(rows // tm,),
        in_specs=[x_spec, first_spec, halo_spec, _const_spec((6, d)), _const_spec((4, d, d)),
                  _const_spec((1, d)), _const_spec((d, lora)), _const_spec((lora, d)), _const_spec((1, d)),
                  _const_spec((d, lora)), _const_spec((lora, d))],
        out_specs=[row_spec] * 6,
        out_shape=[out] * 6,
        compiler_params=_cparams("parallel"),
        name="rwkv_pre",
    )(x3, first.reshape(nseq, 1, d), x3, mu, w_in.astype(BF16), w0.reshape(1, d), w1.astype(BF16),
      w2.astype(BF16), a0.reshape(1, d), a1.astype(BF16), a2.astype(BF16))


def _mm(a, b, dims):
    return lax.dot_general(a.astype(BF16), b.astype(BF16), (dims, ((), ())), preferred_element_type=F32)


def _mm_nn(a, b):
    return _mm(a, b, ((1,), (0,)))


def _mm_nt(a, b):
    return _mm(a, b, ((1,), (1,)))


def _mm_tn(a, b):
    return _mm(a, b, ((0,), (0,)))


def _wkv_prepare(r, k, v, lw, a, k_k, k_a, r_k, nb, c):
    chains = range(len(r))
    rows = nb * c
    n = STACK_ROWS
    lane = lax.broadcasted_iota(jnp.int32, (1, V7X_LANES), 1)
    head0 = lane < RWKV_HEAD
    mk = (jnp.where(head0, 1.0, 0.0).astype(F32), jnp.where(head0, 0.0, 1.0).astype(F32))
    col_head0 = (lane // c) % HEADS_PER_VREG == 0
    cm = (jnp.where(col_head0, 1.0, 0.0).astype(F32), jnp.where(col_head0, 0.0, 1.0).astype(F32))

    def head_sum(t):
        s0 = jnp.sum(t * mk[0], axis=-1, keepdims=True)
        s1 = jnp.sum(t * mk[1], axis=-1, keepdims=True)
        return jnp.where(head0, s0, s1)

    def stack(t, masks):
        if c % (2 * V7X_SUBLANES) == 0:
            t = t.astype(BF16)
            masks = [m.astype(BF16) for m in masks]
        parts = []
        for b in range(nb):
            tb = t[b * c:(b + 1) * c]
            parts += [tb * masks[0], tb * masks[1]]
        return jnp.concatenate(parts, axis=0).astype(BF16)

    ri = lax.broadcasted_iota(jnp.int32, (rows, rows), 0)
    ci = lax.broadcasted_iota(jnp.int32, (rows, rows), 1)
    tri = jnp.where((ri // c == ci // c) & (ri >= ci), 1.0, 0.0).astype(BF16)
    lw_hi = [lw[p].astype(BF16) for p in chains]
    cum = [_mm_nn(tri, lw_hi[p]) + _mm_nn(tri, lw[p] - lw_hi[p].astype(F32)) for p in chains]

    kkr = [k[p] * k_k[p] for p in chains]
    kk = [kkr[p] * lax.rsqrt(jnp.maximum(head_sum(kkr[p] * kkr[p]), 1e-24)) for p in chains]
    kf = [k[p] * (1.0 + (a[p] - 1.0) * k_a[p]) for p in chains]
    g_incl = [jnp.exp(cum[p]) for p in chains]
    g_inv = [jnp.exp(-cum[p]) for p in chains]
    at = [(-kk[p] * jnp.exp(cum[p] - lw[p])).astype(BF16) for p in chains]
    rt = [(r[p] * g_incl[p]).astype(BF16) for p in chains]
    bt = [(kk[p] * a[p] * g_inv[p]).astype(BF16) for p in chains]
    kt = [(kf[p] * g_inv[p]).astype(BF16) for p in chains]
    vb = [v[p].astype(BF16) for p in chains]
    xa = [jnp.concatenate([at[p], rt[p]], axis=0) for p in chains]
    yb = [jnp.concatenate([stack(bt[p], mk), stack(kt[p], mk)], axis=0) for p in chains]
    v_s = [stack(vb[p], mk) for p in chains]

    qi = lax.broadcasted_iota(jnp.int32, (rows, n), 0)
    qj = lax.broadcasted_iota(jnp.int32, (rows, n), 1)
    same = qi // c == qj // (HEADS_PER_VREG * c)
    strict = same & (qj % c < qi % c)
    incl = same & (qj % c <= qi % c)
    eye = jnp.where(same & (qj % c == qi % c), 1.0, 0.0).astype(F32)

    gm = [_mm_nt(xa[p], yb[p]) for p in chains]
    a_ab = [jnp.where(strict, gm[p][:rows, :n], 0.0) for p in chains]
    a_kr = [jnp.concatenate([jnp.where(strict, gm[p][:rows, n:], 0.0),
                             jnp.where(incl, gm[p][rows:, n:], 0.0)], axis=0).astype(BF16) for p in chains]
    a_rb = [jnp.where(incl, gm[p][rows:, :n], 0.0).astype(BF16) for p in chains]

    tinv = [eye + a_ab[p] for p in chains]
    pw = [a_ab[p] for p in chains]
    if c > 2:
        pw = [_mm_nn(pw[p], stack(pw[p], cm)) for p in chains]
        span = 4
        while span < c:
            sq = [_mm_nn(pw[p], jnp.concatenate([stack(pw[p], cm), stack(tinv[p], cm)], axis=1)) for p in chains]
            tinv = [tinv[p] + sq[p][:, n:] for p in chains]
            pw = [sq[p][:, :n] for p in chains]
            span *= 2
        tinv = [tinv[p] + _mm_nn(pw[p], stack(tinv[p], cm)) for p in chains]

    av = [_mm_nn(a_kr[p], v_s[p]) for p in chains]
    bonus = [head_sum(r[p] * kf[p] * r_k[p]) * v[p] for p in chains]
    g_last = [[g_incl[p][(b + 1) * c - 1:(b + 1) * c] for b in range(nb)] for p in chains]
    return dict(xa=xa, at=at, rt=rt, bt=bt, kt=kt, vb=vb, tinv=[t.astype(BF16) for t in tinv], av=av, a_rb=a_rb,
                bonus=bonus, g_last=g_last, head_sum=head_sum, stack=lambda t: stack(t, mk))


def _wkv_apply(prep, sel, states, gn_g, gn_b, nb, c):
    rows = nb * c
    xa, at, rt, bt, kt, vb = (prep[key] for key in ("xa", "at", "rt", "bt", "kt", "vb"))
    stack = prep["stack"]
    if nb == 1:
        ah = [_mm_nt(xa[p], states[i][0]) for i, p in enumerate(sel)]
    else:
        ah = []
        for i, p in enumerate(sel):
            parts = [_mm_nt(jnp.concatenate([at[p][b * c:(b + 1) * c], rt[p][b * c:(b + 1) * c]], axis=0), states[i][b])
                     for b in range(nb)]
            ah.append(jnp.concatenate([t[:c] for t in parts] + [t[c:] for t in parts], axis=0))

    u = [_mm_nn(prep["tinv"][p], stack(ah[i][:rows] + prep["av"][p][:rows])) for i, p in enumerate(sel)]
    y = [ah[i][rows:] + prep["av"][p][rows:] + _mm_nn(prep["a_rb"][p], stack(u[i])) for i, p in enumerate(sel)]

    vi = lax.broadcasted_iota(jnp.int32, (V7X_LANES, V7X_LANES), 0)
    ki = lax.broadcasted_iota(jnp.int32, (V7X_LANES, V7X_LANES), 1)
    same_head = vi // RWKV_HEAD == ki // RWKV_HEAD
    new_states = []
    for i, p in enumerate(sel):
        ub = u[i].astype(BF16)
        per_batch = []
        for b in range(nb):
            sl = slice(b * c, (b + 1) * c)
            upd = _mm_tn(jnp.concatenate([ub[sl], vb[p][sl]], axis=0), jnp.concatenate([bt[p][sl], kt[p][sl]], axis=0))
            per_batch.append((states[i][b] + jnp.where(same_head, upd, 0.0)) * prep["g_last"][p][b])
        new_states.append(per_batch)

    head_sum = prep["head_sum"]
    inv_n = 1.0 / RWKV_HEAD
    out = []
    for i, p in enumerate(sel):
        mean = head_sum(y[i]) * inv_n
        yc = y[i] - mean
        var = head_sum(yc * yc) * inv_n
        out.append(yc * lax.rsqrt(var + GN_EPS) * gn_g[i] + gn_b[i] + prep["bonus"][p])
    return out, new_states


def _pair_lanes(p):
    return slice(p * V7X_LANES, (p + 1) * V7X_LANES)


def _state_to_blockdiag(s0, s1):
    z = jnp.zeros_like(s0)
    return jnp.concatenate([jnp.concatenate([s0, z], axis=1), jnp.concatenate([z, s1], axis=1)], axis=0)


def _wkv_prompt_kernel(r_ref, k_ref, v_ref, lw_ref, a_ref, kk_ref, ka_ref, rk_ref, gg_ref, gb_ref,
                       y_ref, sfin_ref, s_scr, *, n_batch, n_pairs, c, n_sub):
    ci = pl.program_id(0)

    @pl.when(ci == 0)
    def _():
        s_scr[...] = jnp.zeros_like(s_scr)

    seq_pairs = [(b, p) for b in range(n_batch) for p in range(n_pairs)]
    chains = [(j, b, p) for j in range(n_sub) for b, p in seq_pairs]
    data = [[ref[b, j * c:(j + 1) * c, _pair_lanes(p)] for j, b, p in chains]
            for ref in (r_ref, k_ref, v_ref, lw_ref, a_ref)]
    prm = [[ref[:, _pair_lanes(p)] for _, _, p in chains] for ref in (kk_ref, ka_ref, rk_ref)]
    prep = _wkv_prepare(*data, *prm, 1, c)
    gn = [[ref[:, _pair_lanes(p)] for _, p in seq_pairs] for ref in (gg_ref, gb_ref)]
    states = [[s_scr[i]] for i in range(len(seq_pairs))]
    for j in range(n_sub):
        sel = [j * len(seq_pairs) + i for i in range(len(seq_pairs))]
        yn, states = _wkv_apply(prep, sel, states, *gn, 1, c)
        for i, (b, p) in enumerate(seq_pairs):
            y_ref[b, j * c:(j + 1) * c, _pair_lanes(p)] = yn[i]
    for i in range(len(seq_pairs)):
        s_scr[i] = states[i][0]

    @pl.when(ci == pl.num_programs(0) - 1)
    def _():
        for i, (b, p) in enumerate(seq_pairs):
            s = s_scr[i]
            sfin_ref[b, 2 * p] = s[:RWKV_HEAD, :RWKV_HEAD]
            sfin_ref[b, 2 * p + 1] = s[RWKV_HEAD:, RWKV_HEAD:]


def _wkv_sample_kernel(r_ref, k_ref, v_ref, lw_ref, a_ref, kk_ref, ka_ref, rk_ref, gg_ref, gb_ref,
                       s0_ref, y_ref, sfin_ref, *, n_groups, n_pairs, nb, c):
    rows = nb * c
    chains = [(g, p) for g in range(n_groups) for p in range(n_pairs)]
    data = [[ref[g * rows:(g + 1) * rows, _pair_lanes(p)] for g, p in chains]
            for ref in (r_ref, k_ref, v_ref, lw_ref, a_ref)]
    prm = [[ref[:, _pair_lanes(p)] for _, p in chains] for ref in (kk_ref, ka_ref, rk_ref, gg_ref, gb_ref)]
    prep = _wkv_prepare(*data, *prm[:3], nb, c)
    states = [[_state_to_blockdiag(s0_ref[g * nb + b, 2 * p], s0_ref[g * nb + b, 2 * p + 1]) for b in range(nb)]
              for g, p in chains]
    yn, new = _wkv_apply(prep, list(range(len(chains))), states, *prm[3:], nb, c)
    for i, (g, p) in enumerate(chains):
        y_ref[g * rows:(g + 1) * rows, _pair_lanes(p)] = yn[i]
        for b in range(nb):
            sfin_ref[g * nb + b, 2 * p] = new[i][b][:RWKV_HEAD, :RWKV_HEAD]
            sfin_ref[g * nb + b, 2 * p + 1] = new[i][b][RWKV_HEAD:, RWKV_HEAD:]


def _wkv_prompt(r, k, v, lw, a, k_k, k_a, r_k, gn_g, gn_b, batch, seq):
    rows, d = r.shape
    heads = d // RWKV_HEAD
    n_pairs = heads // HEADS_PER_VREG
    c = PROMPT_CHUNK
    n_sub = PROMPT_CHUNKS_PER_STEP
    row_spec = pl.BlockSpec((batch, n_sub * c, d), lambda i: (0, i, 0))
    vec = _const_spec((1, d))
    kern = functools.partial(_wkv_prompt_kernel, n_batch=batch, n_pairs=n_pairs, c=c, n_sub=n_sub)
    seq3 = lambda t: t.reshape(batch, seq, d)
    y, s_fin = pl.pallas_call(
        kern,
        grid=(seq // (n_sub * c),),
        in_specs=[row_spec] * 5 + [vec] * 5,
        out_specs=[row_spec, pl.BlockSpec((batch, heads, RWKV_HEAD, RWKV_HEAD), lambda i: (0, 0, 0, 0))],
        out_shape=[jax.ShapeDtypeStruct((batch, seq, d), F32),
                   jax.ShapeDtypeStruct((batch, heads, RWKV_HEAD, RWKV_HEAD), F32)],
        scratch_shapes=[pltpu.VMEM((batch * n_pairs, V7X_LANES, V7X_LANES), F32)],
        compiler_params=_cparams("arbitrary"),
        name="wkv_prompt",
    )(seq3(r), seq3(k), seq3(v), seq3(lw), seq3(a), k_k, k_a, r_k, gn_g, gn_b)
    return y.reshape(rows, d), s_fin


def _wkv_sample(r, k, v, lw, a, k_k, k_a, r_k, gn_g, gn_b, state, batch, seq):
    rows, d = r.shape
    heads = d // RWKV_HEAD
    nb = WKV_ROWS // seq
    n_groups = SAMPLE_GROUPS_PER_STEP
    row_spec = pl.BlockSpec((n_groups * WKV_ROWS, d), lambda i: (i, 0))
    st_spec = pl.BlockSpec((n_groups * nb, heads, RWKV_HEAD, RWKV_HEAD), lambda i: (i, 0, 0, 0))
    vec = _const_spec((1, d))
    kern = functools.partial(_wkv_sample_kernel, n_groups=n_groups, n_pairs=heads // HEADS_PER_VREG, nb=nb, c=seq)
    return pl.pallas_call(
        kern,
        grid=(batch // (n_groups * nb),),
        in_specs=[row_spec] * 5 + [vec] * 5 + [st_spec],
        out_specs=[row_spec, st_spec],
        out_shape=[jax.ShapeDtypeStruct((rows, d), F32),
                   jax.ShapeDtypeStruct((batch, heads, RWKV_HEAD, RWKV_HEAD), F32)],
        compiler_params=_cparams("parallel"),
        name="wkv_sample",
    )(r, k, v, lw, a, k_k, k_a, r_k, gn_g, gn_b, state)


def _branch_out(y_ref, g_ref, x_ref, w_ref, lng_ref, lnb_ref):
    g = g_ref[...]
    h = (y_ref[...] * (g / (1.0 + jnp.exp(-g)))).astype(BF16)
    z = ALPHA * x_ref[...] + jnp.dot(h, w_ref[...], preferred_element_type=F32)
    mu = jnp.mean(z, axis=-1, keepdims=True)
    zc = z - mu
    var = jnp.mean(zc * zc, axis=-1, keepdims=True)
    return zc * lax.rsqrt(var + LN_EPS) * lng_ref[...] + lnb_ref[...]


def _post_kernel(y_ref, g_ref, x_ref, w_ref, lng_ref, lnb_ref, o_ref):
    o_ref[...] = _branch_out(y_ref, g_ref, x_ref, w_ref, lng_ref, lnb_ref)


def _post_project_kernel(y_ref, g_ref, x_ref, w_ref, lng_ref, lnb_ref, win_ref,
                         o_ref, q_ref, k_ref, v_ref, gate_ref, *, hq, hkv):
    rows = o_ref.shape[0]
    n_part = 2 if rows % (4 * V7X_SUBLANES) == 0 else 1
    parts = [slice(i * rows // n_part, (i + 1) * rows // n_part) for i in range(n_part)]
    x1 = [_branch_out(y_ref.at[s], g_ref.at[s], x_ref.at[s], w_ref, lng_ref, lnb_ref) for s in parts]
    zs = [jnp.dot(x1[i].astype(BF16), win_ref[...], preferred_element_type=F32) for i in range(n_part)]
    for s, xh, z in zip(parts, x1, zs):
        o_ref[s] = xh
        q_ref[s] = (z[:, :hq] * (ATT_SCALE * LOG2E)).astype(q_ref.dtype)
        k_ref[s] = z[:, hq:hq + hkv]
        v_ref[s] = z[:, hq + hkv:hq + 2 * hkv]
        gate_ref[s] = z[:, hq + 2 * hkv:]


def _post(y, g, x2, w_out, ln_g, ln_b, tm, project=None):
    rows, d = x2.shape
    din = y.shape[1]
    in_spec = pl.BlockSpec((tm, din), lambda i: (i, 0))
    row_spec = pl.BlockSpec((tm, d), lambda i: (i, 0))
    in_specs = [in_spec, in_spec, row_spec, _const_spec((din, d)), _const_spec((1, d)), _const_spec((1, d))]
    args = (y, g, x2, w_out.astype(BF16), ln_g.reshape(1, d), ln_b.reshape(1, d))
    x_shape = jax.ShapeDtypeStruct((rows, d), F32)
    if project is None:
        return pl.pallas_call(
            _post_kernel, grid=(rows // tm,), in_specs=in_specs, out_specs=row_spec, out_shape=x_shape,
            compiler_params=_cparams("parallel"), name="branch_post",
        )(*args)
    w_in, q_dtype = project
    hq = ATT_KV_HEADS * ATT_GROUP * ATT_HEAD_DIM
    hkv = ATT_KV_HEADS * ATT_HEAD_DIM
    wide = pl.BlockSpec((tm, hq), lambda i: (i, 0))
    narrow = pl.BlockSpec((tm, hkv), lambda i: (i, 0))
    return pl.pallas_call(
        functools.partial(_post_project_kernel, hq=hq, hkv=hkv),
        grid=(rows // tm,),
        in_specs=in_specs + [_const_spec(w_in.shape)],
        out_specs=[row_spec, wide, narrow, narrow, wide],
        out_shape=[x_shape, jax.ShapeDtypeStruct((rows, hq), q_dtype), jax.ShapeDtypeStruct((rows, hkv), F32),
                   jax.ShapeDtypeStruct((rows, hkv), F32), jax.ShapeDtypeStruct((rows, hq), F32)],
        compiler_params=_cparams("parallel"),
        name="branch_post_project",
    )(*args, w_in.astype(BF16))


def _alibi_slope(h):
    n_heads = ATT_KV_HEADS * ATT_GROUP
    return LOG2E * 2.0 ** (-8.0 * (h + 1) / n_heads)


def _fold_masks(q_time):
    col = lax.broadcasted_iota(jnp.int32, q_time.shape, 1)
    from_prev = col > q_time
    dist = jnp.where(from_prev, q_time + WINDOW - col, q_time - col).astype(F32)
    return from_prev, dist


def _softmax_folded(sp, sc, slopes, sinks, from_prev, dist, dead):
    idx = range(len(sp))
    s = [jnp.where(from_prev, sp[i], sc[i]) - slopes[i] * dist for i in idx]
    if dead is not None:
        s = [jnp.where(dead, NEG, s[i]) for i in idx]
    m = [jnp.maximum(jnp.max(s[i], axis=-1, keepdims=True), sinks[i]) for i in idx]
    p = [jnp.exp2(s[i] - m[i]) for i in idx]
    den = [jnp.sum(p[i], axis=-1, keepdims=True) + jnp.exp2(sinks[i] - m[i]) for i in idx]
    pp = [jnp.where(from_prev, p[i], 0.0).astype(BF16) for i in idx]
    pc = [jnp.where(from_prev, 0.0, p[i]).astype(BF16) for i in idx]
    return pp, pc, den


_NT = (((1,), (1,)), ((), ()))


def _att_prompt_kernel(sink_ref, q_ref, kc_ref, vc_ref, kp_ref, vp_ref, gate_ref, x_ref, w_ref, lng_ref, lnb_ref,
                       y_ref, o_scr, *, n_blk, n_acc):
    step = pl.program_id(1)
    from_prev, dist = _fold_masks(lax.broadcasted_iota(jnp.int32, (WINDOW, WINDOW), 0))
    hd = ATT_HEAD_DIM
    heads = range(ATT_KV_HEADS * ATT_GROUP)
    slopes = [_alibi_slope(h) for h in heads]
    sinks = [sink_ref[h] * LOG2E for h in heads]
    row0 = pl.multiple_of((step % n_acc) * (n_blk * WINDOW), n_blk * WINDOW)
    items = [(blk, h) for blk in range(n_blk) for h in heads]
    kcat, vcat = [], []
    for blk in range(n_blk):
        rows = slice(blk * WINDOW, (blk + 1) * WINDOW)
        prev_of = (lambda ref_p, ref_c: ref_p) if blk == 0 else \
            (lambda ref_p, ref_c, blk=blk: ref_c.at[(blk - 1) * WINDOW:blk * WINDOW])
        kc, vc = [[jnp.concatenate([prev_of(prev, cur)[:, kvh * hd:(kvh + 1) * hd],
                                    cur[rows, kvh * hd:(kvh + 1) * hd]], axis=0).astype(BF16)
                   for kvh in range(ATT_KV_HEADS)] for prev, cur in ((kp_ref, kc_ref), (vp_ref, vc_ref))]
        kcat.append(kc)
        vcat.append(vc)
    s2 = [lax.dot_general(q_ref[blk * WINDOW:(blk + 1) * WINDOW, h * hd:(h + 1) * hd], kcat[blk][h // ATT_GROUP], _NT,
                          preferred_element_type=F32) for blk, h in items]
    n_h = len(heads)
    groups = [(range(n_h), from_prev & (step == 0))] + ([(range(n_h, len(items)), None)] if n_blk > 1 else [])
    parts = [_softmax_folded([s2[i][:, :WINDOW] for i in sel], [s2[i][:, WINDOW:] for i in sel],
                             [slopes[items[i][1]] for i in sel], [sinks[items[i][1]] for i in sel],
                             from_prev, dist, dead) for sel, dead in groups]
    pp, pc, den = [sum((list(part[j]) for part in parts), []) for j in range(3)]
    for i, (blk, h) in enumerate(items):
        o = jnp.dot(jnp.concatenate([pp[i], pc[i]], axis=1), vcat[blk][h // ATT_GROUP], preferred_element_type=F32)
        o_scr[pl.ds(row0 + blk * WINDOW, WINDOW), h * hd:(h + 1) * hd] = o / den[i]

    @pl.when((step % n_acc == n_acc - 1) & (step >= 0))
    def _():
        y_ref[...] = _branch_out(o_scr, gate_ref, x_ref, w_ref, lng_ref, lnb_ref)


def _att_prompt(q, k, v, gate, x2, sinks, w_out, ln_g, ln_b, batch, seq, tm):
    rows, hq = q.shape
    hkv = k.shape[1]
    d = x2.shape[1]
    n_blk = ATT_BLOCKS_PER_STEP
    steps = seq // (n_blk * WINDOW)
    n_acc = tm // (n_blk * WINDOW)
    cur = lambda b, n: (b * steps + n, 0)
    prev = lambda b, n: ((b * steps + n) * n_blk - jnp.minimum(n, 1), 0)
    tile = lambda b, n: ((b * steps + n) // n_acc, 0)
    return pl.pallas_call(
        functools.partial(_att_prompt_kernel, n_blk=n_blk, n_acc=n_acc),
        grid=(batch, steps),
        in_specs=[pl.BlockSpec(memory_space=pltpu.SMEM),
                  pl.BlockSpec((n_blk * WINDOW, hq), cur),
                  pl.BlockSpec((n_blk * WINDOW, hkv), cur), pl.BlockSpec((n_blk * WINDOW, hkv), cur),
                  pl.BlockSpec((WINDOW, hkv), prev), pl.BlockSpec((WINDOW, hkv), prev),
                  pl.BlockSpec((tm, hq), tile), pl.BlockSpec((tm, d), tile),
                  _const_spec((hq, d)), _const_spec((1, d)), _const_spec((1, d))],
        out_specs=pl.BlockSpec((tm, d), tile),
        out_shape=jax.ShapeDtypeStruct((rows, d), F32),
        scratch_shapes=[pltpu.VMEM((tm, hq), F32)],
        compiler_params=_cparams("parallel", "arbitrary"),
        name="att_prompt",
    )(sinks, q, k, v, k, v, gate, x2, w_out.astype(BF16), ln_g.reshape(1, d), ln_b.reshape(1, d))


def _att_sample_kernel(sink_ref, q_ref, kn_ref, vn_ref, ckt_ref, cvt_ref, o_ref, kwt_ref, vwt_ref, *, nb, t):
    m = ATT_GROUP * t
    grp = lax.broadcasted_iota(jnp.int32, (m, 1), 0) // t
    from_prev, dist = _fold_masks(lax.broadcasted_iota(jnp.int32, (m, WINDOW), 0) % t)
    hd = ATT_HEAD_DIM
    pad = jnp.zeros((WINDOW - t, hd), F32)
    items = [(kvh, b) for kvh in range(ATT_KV_HEADS) for b in range(nb)]
    idx = range(len(items))
    slopes, sinks = [], []
    for kvh in range(ATT_KV_HEADS):
        slope = jnp.zeros((m, 1), F32)
        sink = jnp.zeros((m, 1), F32)
        for g in range(ATT_GROUP):
            h = kvh * ATT_GROUP + g
            slope = jnp.where(grp == g, _alibi_slope(h), slope)
            sink = jnp.where(grp == g, sink_ref[h] * LOG2E, sink)
        slopes += [slope] * nb
        sinks += [sink] * nb
    qs = [jnp.concatenate([q_ref[b * t:(b + 1) * t, (kvh * ATT_GROUP + g) * hd:(kvh * ATT_GROUP + g + 1) * hd]
                           for g in range(ATT_GROUP)], axis=0).astype(BF16) for kvh, b in items]
    k_new = [kn_ref[b * t:(b + 1) * t, kvh * hd:(kvh + 1) * hd] for kvh, b in items]
    v_new = [vn_ref[b * t:(b + 1) * t, kvh * hd:(kvh + 1) * hd] for kvh, b in items]
    kt_old = [ckt_ref[b, kvh] for kvh, b in items]
    vt_old = [cvt_ref[b, kvh] for kvh, b in items]

    sp = [jnp.dot(qs[i], kt_old[i].astype(BF16), preferred_element_type=F32) for i in idx]
    sc = [lax.dot_general(qs[i], jnp.concatenate([k_new[i], pad], axis=0).astype(BF16), _NT,
                          preferred_element_type=F32) for i in idx]
    pp, pc, den = _softmax_folded(sp, sc, slopes, sinks, from_prev, dist, None)
    o = [(lax.dot_general(pp[i], vt_old[i].astype(BF16), _NT, preferred_element_type=F32)
          + jnp.dot(pc[i], jnp.concatenate([v_new[i], pad], axis=0).astype(BF16), preferred_element_type=F32))
         / den[i] for i in idx]
    for i, (kvh, b) in enumerate(items):
        for g in range(ATT_GROUP):
            h = kvh * ATT_GROUP + g
            o_ref[b * t:(b + 1) * t, h * hd:(h + 1) * hd] = o[i][g * t:(g + 1) * t]

    lane = lax.broadcasted_iota(jnp.int32, (hd, WINDOW), 1)
    for new, old, out_ref in ((k_new, kt_old, kwt_ref), (v_new, vt_old, vwt_ref)):
        new_t = [jnp.concatenate([pad, new[i]], axis=0).T for i in idx]
        for i, (kvh, b) in enumerate(items):
            out_ref[b, kvh] = jnp.where(lane >= WINDOW - t, new_t[i], pltpu.roll(old[i], WINDOW - t, axis=1))


def _att_sample(q, k, v, cache_kt, cache_vt, sinks, batch, t, nb):
    rows, hq = q.shape
    hkv = k.shape[1]
    row = lambda i: (i, 0)
    cache_spec = pl.BlockSpec((nb,) + cache_kt.shape[1:], lambda i: (i, 0, 0, 0))
    cache_shape = jax.ShapeDtypeStruct(cache_kt.shape, F32)
    return pl.pallas_call(
        functools.partial(_att_sample_kernel, nb=nb, t=t),
        grid=(batch // nb,),
        in_specs=[pl.BlockSpec(memory_space=pltpu.SMEM),
                  pl.BlockSpec((nb * t, hq), row), pl.BlockSpec((nb * t, hkv), row),
                  pl.BlockSpec((nb * t, hkv), row), cache_spec, cache_spec],
        out_specs=[pl.BlockSpec((nb * t, hq), row), cache_spec, cache_spec],
        out_shape=[jax.ShapeDtypeStruct((rows, hq), F32), cache_shape, cache_shape],
        compiler_params=_cparams("parallel"),
        name="att_sample",
    )(sinks, q, k, v, cache_kt, cache_vt)


ROW_TILE = 512
ATT_BLOCKS_PER_STEP = 4


def _row_tile(rows):
    return ROW_TILE if rows % ROW_TILE == 0 else rows


def kernel(x_prompt, x_sample, state_wkv, state_shift, cache_win_k, cache_win_v, ln_g, ln_b, rwkv_mu, rwkv_w_in, rwkv_w0, rwkv_w1, rwkv_w2, rwkv_a0, rwkv_a1, rwkv_a2, rwkv_k_k, rwkv_k_a, rwkv_r_k, rwkv_gn_g, rwkv_gn_b, rwkv_w_out, att_w_in, att_sinks, att_w_out):
    bp, tp, d = x_prompt.shape
    bs, ts, _ = x_sample.shape
    assert tp % (PROMPT_CHUNK * PROMPT_CHUNKS_PER_STEP) == 0 and tp % ROW_TILE == 0 and ROW_TILE % (ATT_BLOCKS_PER_STEP * WINDOW) == 0 and WKV_ROWS % ts == 0 and bs % (SAMPLE_GROUPS_PER_STEP * WKV_ROWS // ts) == 0
    assert cache_win_k.shape[2] == WINDOW
    heads = d // RWKV_HEAD
    hkv = ATT_KV_HEADS * ATT_HEAD_DIM

    j = 0
    vec = lambda t: t.reshape(1, d)
    pre_w = (rwkv_mu[j], rwkv_w_in[j], rwkv_w0[j], rwkv_w1[j], rwkv_w2[j], rwkv_a0[j], rwkv_a1[j], rwkv_a2[j])
    head_w = (vec(rwkv_k_k[j]), vec(rwkv_k_a[j]), vec(rwkv_r_k[j]), vec(rwkv_gn_g[j]), vec(rwkv_gn_b[j]))

    xp2 = x_prompt.reshape(bp * tp, d)
    xs2 = x_sample.reshape(bs * ts, d)

    r, k, v, g, lw, a = _rwkv_pre(x_prompt, jnp.zeros((bp, d), F32), *pre_w, tm=_row_tile(bp * tp))
    yp, wkv_p = _wkv_prompt(r, k, v, lw, a, *head_w, batch=bp, seq=tp)
    xp1, q_p, k_p, v_p, gate_p = _post(yp, g, xp2, rwkv_w_out[j], ln_g[0], ln_b[0], tm=_row_tile(bp * tp),
                                       project=(att_w_in[j], BF16))

    r, k, v, g, lw, a = _rwkv_pre(x_sample, state_shift[j], *pre_w, tm=_row_tile(bs * ts))
    ys, wkv_s = _wkv_sample(r, k, v, lw, a, *head_w, state_wkv[j], batch=bs, seq=ts)
    xs1, q_s, k_s, v_s, gate_s = _post(ys, g, xs2, rwkv_w_out[j], ln_g[0], ln_b[0], tm=_row_tile(bs * ts),
                                       project=(att_w_in[j], F32))

    y_prompt = _att_prompt(q_p, k_p, v_p, gate_p, xp1, att_sinks[j], att_w_out[j], ln_g[1], ln_b[1],
                           batch=bp, seq=tp, tm=ROW_TILE).reshape(bp, tp, d)
    win_shape = (bp, WINDOW, ATT_KV_HEADS, ATT_HEAD_DIM)
    win_k_p = k_p.reshape(bp, tp, hkv)[:, tp - WINDOW:].reshape(win_shape)
    win_v_p = v_p.reshape(bp, tp, hkv)[:, tp - WINDOW:].reshape(win_shape)

    to_t = lambda c: jnp.transpose(c, (0, 2, 3, 1))
    o, win_k_s, win_v_s = _att_sample(q_s, k_s, v_s, to_t(cache_win_k[j]), to_t(cache_win_v[j]), att_sinks[j],
                                      batch=bs, t=ts, nb=WKV_ROWS // ts)
    y_sample = _post(o, gate_s, xs1, att_w_out[j], ln_g[1], ln_b[1], tm=_row_tile(bs * ts)).reshape(bs, ts, d)
    win_k_s = jnp.transpose(win_k_s, (0, 3, 1, 2))
    win_v_s = jnp.transpose(win_v_s, (0, 3, 1, 2))

    return (y_prompt, y_sample,
            wkv_p[None], x_prompt[:, -1][None], win_k_p[None], win_v_p[None],
            wkv_s[None], x_sample[:, -1][None], win_k_s[None], win_v_s[None])
```
